```python
import math
import jax, jax.numpy as jnp
from jax import lax
import numpy as np

D_MODEL = 4096
BATCH = 4
SEQ = 2048
DEPTH = 1

RET_HEADS = 8
RET_DIM = 256
RET_WIDTH = RET_HEADS * RET_DIM
RET_CHUNK = 128
NSA_HEADS = 16
NSA_KV_GROUPS = 4
NSA_REP = NSA_HEADS // NSA_KV_GROUPS
NSA_DIM = 128
NSA_WIDTH = NSA_HEADS * NSA_DIM
NSA_KV_WIDTH = NSA_KV_GROUPS * NSA_DIM
CMP_LEN = 32
CMP_STRIDE = 16
CMP_HIDDEN = 128
SLC_LEN = 64
SLC_TOPK = 16
SLC_QBLOCK = 32
WIN_LEN = 512
WIN_QBLOCK = 128
MIX_WIDTH = RET_WIDTH + NSA_WIDTH
IN_SIZES = (RET_WIDTH, RET_WIDTH, RET_WIDTH, RET_WIDTH,
            NSA_WIDTH, NSA_KV_WIDTH, NSA_KV_WIDTH, NSA_KV_WIDTH,
            NSA_KV_WIDTH, NSA_KV_WIDTH, NSA_KV_WIDTH, 3 * NSA_HEADS)
IN_WIDTH = sum(IN_SIZES)
MEM_LEN = 256
XA_HEADS = 4
XA_DIM = 128
XA_WIDTH = XA_HEADS * XA_DIM
D_FF = 4 * D_MODEL
LN_EPS = 1e-5
DN_ALPHA = (2.0 * DEPTH) ** 0.25
DN_BETA = (8.0 * DEPTH) ** -0.25
NEG_INF = -1e30
FORCE = 1e9

kernel_name = "hybrid_retention_nsa_deepnorm"


def layer_norm(x, g, b):
    xf = x.astype(jnp.float32)
    mu = jnp.mean(xf, axis=-1, keepdims=True)
    var = jnp.mean(jnp.square(xf - mu), axis=-1, keepdims=True)
    return ((xf - mu) * lax.rsqrt(var + LN_EPS) * g + b).astype(x.dtype)


def alibi_slopes(n):
    return 2.0 ** (-8.0 * jnp.arange(1, n + 1, dtype=jnp.float32) / n)


def retention(q, k, v):
    f32 = jnp.float32
    b, t, _ = q.shape
    n = t // RET_CHUNK

    def split(z):
        return z.astype(f32).reshape(b, n, RET_CHUNK, RET_HEADS, RET_DIM).transpose(0, 3, 1, 2, 4)

    q, k, v = split(q), split(k) * RET_DIM ** -0.5, split(v)
    log_g = jnp.log1p(-(2.0 ** (-5.0 - jnp.arange(RET_HEADS, dtype=f32))))
    pos = jnp.arange(RET_CHUNK, dtype=f32)
    diff = pos[:, None] - pos[None, :]
    decay = jnp.where(diff >= 0, jnp.exp(log_g[:, None, None] * jnp.maximum(diff, 0.0)), 0.0)
    scores = jnp.einsum('bhncd,bhnsd->bhncs', q, k) * decay[None, :, None]
    o_inner = jnp.einsum('bhncs,bhnse->bhnce', scores, v)
    k_tail = k * jnp.exp(log_g[:, None] * (RET_CHUNK - 1 - pos))[None, :, None, :, None]
    kv = jnp.einsum('bhncd,bhnce->nbhde', k_tail, v)
    chunk_decay = jnp.exp(log_g * RET_CHUNK)[None, :, None, None]

    def step(state, kv_n):
        return chunk_decay * state + kv_n, state

    _, states = lax.scan(step, jnp.zeros(kv.shape[1:], f32), kv)
    q_head = q * jnp.exp(log_g[:, None] * (pos + 1.0))[None, :, None, :, None]
    o = o_inner + jnp.einsum('bhncd,nbhde->bhnce', q_head, states)
    mu = jnp.mean(o, axis=-1, keepdims=True)
    var = jnp.mean(jnp.square(o - mu), axis=-1, keepdims=True)
    o = (o - mu) * lax.rsqrt(var + LN_EPS)
    return o.transpose(0, 2, 3, 1, 4).reshape(b, t, RET_WIDTH)


def compress(z, pos_emb, w1, w2):
    b, g, t, d = z.shape
    n_sub = CMP_LEN // CMP_STRIDE
    nc = t // CMP_STRIDE - n_sub + 1
    sub = z.reshape(b, g, t // CMP_STRIDE, CMP_STRIDE, d)
    blocks = jnp.concatenate([sub[:, :, i:i + nc] for i in range(n_sub)], axis=3) + pos_emb
    hid = jax.nn.gelu(blocks.reshape(b, g, nc, CMP_LEN * d) @ w1)
    return (hid @ w2).astype(jnp.float32)


def nsa(q, k_cmp, v_cmp, k_slc, v_slc, k_win, v_win, gates, cmp_pos, cmp_w1, cmp_w2):
    f32 = jnp.float32
    b, t, _ = q.shape
    G, R, dh = NSA_KV_GROUPS, NSA_REP, NSA_DIM
    q = q.astype(f32).reshape(b, t, G, R, dh).transpose(0, 2, 3, 1, 4) * dh ** -0.5

    def kv_heads(z):
        return z.astype(f32).reshape(b, t, G, dh).transpose(0, 2, 1, 3)

    slopes = alibi_slopes(NSA_HEADS).reshape(G, R)
    tpos = jnp.arange(t)

    kc = compress(kv_heads(k_cmp), cmp_pos[0], cmp_w1[0], cmp_w2[0])
    vc = compress(kv_heads(v_cmp), cmp_pos[1], cmp_w1[1], cmp_w2[1])
    nc = kc.shape[2]
    c_start = jnp.arange(nc) * CMP_STRIDE
    c_dist = (tpos[:, None] - (c_start + CMP_LEN - 1)[None, :]).astype(f32)
    c_mask = c_dist >= 0
    s = jnp.einsum('bgrtd,bgcd->bgrtc', q, kc) - slopes[:, :, None, None] * c_dist
    p_cmp = jnp.where(c_mask, jax.nn.softmax(jnp.where(c_mask, s, NEG_INF), axis=-1), 0.0)
    o_cmp = jnp.einsum('bgrtc,bgcd->bgrtd', p_cmp, vc)

    ns = t // SLC_LEN
    s_start = jnp.arange(ns) * SLC_LEN
    overlap = jnp.clip(jnp.minimum(c_start[:, None] + CMP_LEN, s_start[None, :] + SLC_LEN)
                       - jnp.maximum(c_start[:, None], s_start[None, :]), 0, None).astype(f32) / CMP_LEN
    imp = jnp.einsum('bgrtc,cs->bgts', p_cmp, overlap)
    blk = jnp.arange(ns)[None, :]
    cur = (tpos // SLC_LEN)[:, None]
    forced = (blk == 0) | (blk == cur) | (blk == cur - 1)
    imp = jnp.where(forced, FORCE, jnp.where(blk > cur, -FORCE, imp))
    n_top = min(SLC_TOPK, ns)
    _, idx = lax.top_k(imp, n_top)

    ks = kv_heads(k_slc).reshape(b, G, ns, SLC_LEN, dh)
    vs = kv_heads(v_slc).reshape(b, G, ns, SLC_LEN, dh)
    nq = t // SLC_QBLOCK
    q_blocks = q.reshape(b, G, R, nq, SLC_QBLOCK, dh).transpose(3, 0, 1, 2, 4, 5)
    idx_blocks = idx.reshape(b, G, nq, SLC_QBLOCK, n_top).transpose(2, 0, 1, 3, 4)
    starts = jnp.arange(nq) * SLC_QBLOCK
    bi = jnp.arange(b)[:, None, None, None]
    gi = jnp.arange(G)[None, :, None, None]

    def slc_block(args):
        qb, ib, start = args
        kg = ks[bi, gi, ib]
        vg = vs[bi, gi, ib]
        kpos = ib[..., None] * SLC_LEN + jnp.arange(SLC_LEN)
        qpos = start + jnp.arange(SLC_QBLOCK)
        dist = (qpos[None, None, :, None, None] - kpos).astype(f32)[:, :, None]
        sc = jnp.einsum('bgrqd,bgqkld->bgrqkl', qb, kg) - slopes[None, :, :, None, None, None] * dist
        sc = jnp.where(dist >= 0, sc, NEG_INF).reshape(b, G, R, SLC_QBLOCK, n_top * SLC_LEN)
        pr = jax.nn.softmax(sc, axis=-1).reshape(b, G, R, SLC_QBLOCK, n_top, SLC_LEN)
        return jnp.einsum('bgrqkl,bgqkld->bgrqd', pr, vg)

    o_slc = lax.map(slc_block, (q_blocks, idx_blocks, starts))
    o_slc = o_slc.transpose(1, 2, 3, 0, 4, 5).reshape(b, G, R, t, dh)

    nb = t // WIN_QBLOCK
    nband = WIN_LEN // WIN_QBLOCK + 1

    def band(z):
        zp = jnp.pad(z, ((0, 0), (0, 0), (WIN_LEN, 0), (0, 0))).reshape(b, G, nb + nband - 1, WIN_QBLOCK, dh)
        return jnp.concatenate([zp[:, :, i:i + nb] for i in range(nband)], axis=3)

    kb, vb = band(kv_heads(k_win)), band(kv_heads(v_win))
    qw = q.reshape(b, G, R, nb, WIN_QBLOCK, dh)
    qpos = jnp.arange(nb)[:, None] * WIN_QBLOCK + jnp.arange(WIN_QBLOCK)
    kpos = jnp.arange(nb)[:, None] * WIN_QBLOCK - WIN_LEN + jnp.arange(nband * WIN_QBLOCK)
    dist = qpos[:, :, None] - kpos[:, None, :]
    w_mask = (dist >= 0) & (dist < WIN_LEN) & (kpos[:, None, :] >= 0)
    sw = jnp.einsum('bgrnqd,bgnkd->bgrnqk', qw, kb) - slopes[:, :, None, None, None] * dist.astype(f32)
    pw = jax.nn.softmax(jnp.where(w_mask, sw, NEG_INF), axis=-1)
    o_win = jnp.einsum('bgrnqk,bgnkd->bgrnqd', pw, vb).reshape(b, G, R, t, dh)

    gt = jax.nn.sigmoid(gates.astype(f32)).reshape(b, t, 3, G, R).transpose(2, 0, 3, 4, 1)[..., None]
    o = gt[0] * o_cmp + gt[1] * o_slc + gt[2] * o_win
    return o.transpose(0, 3, 1, 2, 4).reshape(b, t, NSA_WIDTH)


def memory_attention(h, mem, wq, wkv, wo):
    b, t, _ = h.shape
    m = mem.shape[1]
    q = (h @ wq).reshape(b, t, XA_HEADS, XA_DIM)
    kv = (mem @ wkv).reshape(b, m, 2, XA_HEADS, XA_DIM)
    s = jnp.einsum('bthd,bmhd->bhtm', q, kv[:, :, 0]).astype(jnp.float32) * XA_DIM ** -0.5
    p = jax.nn.softmax(s, axis=-1).astype(h.dtype)
    o = jnp.einsum('bhtm,bmhd->bthd', p, kv[:, :, 1]).reshape(b, t, XA_WIDTH)
    return o @ wo


def setup_inputs(seed: int = 0) -> dict:
    key = jax.random.key(seed)
    ks = jax.random.split(key, 16)
    nrm = lambda k, shape, scale: jax.random.normal(k, shape, jnp.float32) * scale
    return {
        "x": nrm(ks[0], (BATCH, SEQ, D_MODEL), 1.0),
        "mem": nrm(ks[1], (BATCH, MEM_LEN, D_MODEL), 1.0),
        "w_in": nrm(ks[2], (DEPTH, D_MODEL, IN_WIDTH), D_MODEL ** -0.5),
        "w_out": nrm(ks[3], (DEPTH, MIX_WIDTH, D_MODEL), MIX_WIDTH ** -0.5 * DN_BETA),
        "cmp_pos": nrm(ks[4], (DEPTH, 2, CMP_LEN, NSA_DIM), 0.1),
        "cmp_w1": nrm(ks[5], (DEPTH, 2, CMP_LEN * NSA_DIM, CMP_HIDDEN), (CMP_LEN * NSA_DIM) ** -0.5),
        "cmp_w2": nrm(ks[6], (DEPTH, 2, CMP_HIDDEN, NSA_DIM), CMP_HIDDEN ** -0.5),
        "xa_wq": nrm(ks[7], (DEPTH, D_MODEL, XA_WIDTH), D_MODEL ** -0.5),
        "xa_wkv": nrm(ks[8], (DEPTH, D_MODEL, 2 * XA_WIDTH), D_MODEL ** -0.5),
        "xa_wo": nrm(ks[9], (DEPTH, XA_WIDTH, D_MODEL), XA_WIDTH ** -0.5 * DN_BETA),
        "w_ff1": nrm(ks[10], (DEPTH, D_MODEL, D_FF), D_MODEL ** -0.5),
        "w_ff2": nrm(ks[11], (DEPTH, D_FF, D_MODEL), D_FF ** -0.5 * DN_BETA),
        "ln_g": 1.0 + nrm(ks[12], (DEPTH, 3, D_MODEL), 0.02),
        "ln_b": nrm(ks[13], (DEPTH, 3, D_MODEL), 0.02),
    }


def reference(x, mem, w_in, w_out, cmp_pos, cmp_w1, cmp_w2, xa_wq, xa_wkv, xa_wo,
              w_ff1, w_ff2, ln_g, ln_b):
    offsets = np.cumsum(IN_SIZES)[:-1].tolist()
    h = x
    for l in range(DEPTH):
        proj = h @ w_in[l]
        (rq, rk, rv, rg, nq, kc, vc, ksl, vsl, kw, vw, gates) = jnp.split(proj, offsets, axis=-1)
        ret = retention(rq, rk, rv) * jax.nn.silu(rg.astype(jnp.float32))
        sparse = nsa(nq, kc, vc, ksl, vsl, kw, vw, gates, cmp_pos[l], cmp_w1[l], cmp_w2[l])
        mix = jnp.concatenate([ret, sparse], axis=-1).astype(h.dtype) @ w_out[l]
        h = layer_norm(DN_ALPHA * h + mix, ln_g[l, 0], ln_b[l, 0])
        h = layer_norm(DN_ALPHA * h + memory_attention(h, mem, xa_wq[l], xa_wkv[l], xa_wo[l]),
                       ln_g[l, 1], ln_b[l, 1])
        ff = jnp.square(jax.nn.relu(h @ w_ff1[l])) @ w_ff2[l]
        h = layer_norm(DN_ALPHA * h + ff, ln_g[l, 2], ln_b[l, 2])
    return h
```

```python
import functools
import math

import jax
import jax.numpy as jnp
from jax import lax
from jax.experimental import pallas as pl
from jax.experimental.pallas import tpu as pltpu

F32 = jnp.float32
BF16 = jnp.bfloat16

D_MODEL = 4096
RET_HEADS = 8
RET_DIM = 256
RET_WIDTH = RET_HEADS * RET_DIM
RET_CHUNK = 128
NSA_HEADS = 16
NSA_KV_GROUPS = 4
NSA_REP = NSA_HEADS // NSA_KV_GROUPS
NSA_DIM = 128
NSA_WIDTH = NSA_HEADS * NSA_DIM
NSA_KV_WIDTH = NSA_KV_GROUPS * NSA_DIM
CMP_LEN = 32
CMP_STRIDE = 16
CMP_SUB = CMP_LEN // CMP_STRIDE
SLC_LEN = 64
SLC_TOPK = 16
WIN_LEN = 512
MIX_WIDTH = RET_WIDTH + NSA_WIDTH
GATE_WIDTH = 3 * NSA_HEADS
OFF_RQ, OFF_RK, OFF_RV, OFF_RG = 0, RET_WIDTH, 2 * RET_WIDTH, 3 * RET_WIDTH
OFF_NQ = 4 * RET_WIDTH
OFF_KC = OFF_NQ + NSA_WIDTH
OFF_VC = OFF_KC + NSA_KV_WIDTH
OFF_KS = OFF_VC + NSA_KV_WIDTH
OFF_VS = OFF_KS + NSA_KV_WIDTH
OFF_KW = OFF_VS + NSA_KV_WIDTH
OFF_VW = OFF_KW + NSA_KV_WIDTH
OFF_GATES = OFF_VW + NSA_KV_WIDTH
XA_HEADS = 4
XA_DIM = 128
XA_WIDTH = XA_HEADS * XA_DIM
LN_EPS = 1e-5
DEPTH = 1
DN_ALPHA = (2.0 * DEPTH) ** 0.25
NEG_INF = -1e30
FORCE = 1e9

LANES = 128
NSA_TQ = 128
NSA_TK = 128
LN_ROWS = 16
FFN_TN = 1024
VMEM_LIMIT = 56 * 1024 * 1024

_NT = (((1,), (1,)), ((), ()))
_TN = (((0,), (0,)), ((), ()))


def _params(sem):
    return pltpu.CompilerParams(dimension_semantics=sem, vmem_limit_bytes=VMEM_LIMIT)


def _layer_norm(y, g, b):
    mu = jnp.mean(y, axis=-1, keepdims=True)
    d = y - mu
    var = jnp.mean(d * d, axis=-1, keepdims=True)
    return d * lax.rsqrt(var + LN_EPS) * g + b


def _layer_norm_rows(o_ref, pre_norm, g_ref, b_ref):
    def body(c, carry):
        rows = pl.ds(pl.multiple_of(c * LN_ROWS, LN_ROWS), LN_ROWS)
        o_ref[rows, :] = _layer_norm(pre_norm(rows), g_ref[...], b_ref[...])
        return carry

    lax.fori_loop(0, o_ref.shape[0] // LN_ROWS, body, 0)


def _mm_kernel(x_ref, w_ref, o_ref):
    o_ref[...] = jnp.dot(x_ref[...], w_ref[...], preferred_element_type=F32).astype(o_ref.dtype)


def _matmul(x, w, n_out, tm, tn, out_dtype):
    m, k = x.shape
    return pl.pallas_call(
        _mm_kernel,
        grid=(m // tm, n_out // tn),
        in_specs=[pl.BlockSpec((tm, k), lambda i, j: (i, 0)),
                  pl.BlockSpec((k, tn), lambda i, j: (0, j))],
        out_specs=pl.BlockSpec((tm, tn), lambda i, j: (i, j)),
        out_shape=jax.ShapeDtypeStruct((m, n_out), out_dtype),
        compiler_params=_params(("parallel", "parallel")),
        name="matmul",
    )(x, w)


def _retention_kernel(q_ref, k_ref, v_ref, g_ref, o_ref, state_ref):
    c = RET_CHUNK

    @pl.when(pl.program_id(1) == 0)
    def _():
        state_ref[...] = jnp.zeros_like(state_ref)

    diff = (lax.broadcasted_iota(jnp.int32, (c, c), 0)
            - lax.broadcasted_iota(jnp.int32, (c, c), 1)).astype(F32)
    pos = lax.broadcasted_iota(jnp.int32, (c, RET_DIM), 0).astype(F32)
    for h in range(RET_HEADS):
        log_g = math.log1p(-(2.0 ** (-5.0 - h)))
        sl = slice(h * RET_DIM, (h + 1) * RET_DIM)
        q = q_ref[:, sl].astype(F32)
        k = k_ref[:, sl].astype(F32) * (RET_DIM ** -0.5)
        v = v_ref[:, sl]
        decay = jnp.where(diff >= 0, jnp.exp(log_g * jnp.maximum(diff, 0.0)), 0.0)
        scores = lax.dot_general(q.astype(BF16), k.astype(BF16), _NT,
                                 preferred_element_type=F32) * decay
        o_inner = jnp.dot(scores.astype(BF16), v, preferred_element_type=F32)
        k_tail = k * jnp.exp(log_g * (c - 1.0 - pos))
        kv = lax.dot_general(k_tail.astype(BF16), v, _TN, preferred_element_type=F32)
        q_head = q * jnp.exp(log_g * (pos + 1.0))
        state = state_ref[h]
        o = o_inner + jnp.dot(q_head.astype(BF16), state.astype(BF16), preferred_element_type=F32)
        state_ref[h] = math.exp(log_g * c) * state + kv
        mu = jnp.mean(o, axis=-1, keepdims=True)
        d = o - mu
        var = jnp.mean(d * d, axis=-1, keepdims=True)
        o = d * lax.rsqrt(var + LN_EPS)
        gate = g_ref[:, sl].astype(F32)
        o_ref[:, sl] = (o * (gate * jax.nn.sigmoid(gate))).astype(o_ref.dtype)


def _retention(proj, batch, seq):
    n = seq // RET_CHUNK
    blk = lambda col: pl.BlockSpec((RET_CHUNK, RET_WIDTH), lambda b, i: (b * n + i, col))
    return pl.pallas_call(
        _retention_kernel,
        grid=(batch, n),
        in_specs=[blk(OFF_RQ // RET_WIDTH), blk(OFF_RK // RET_WIDTH),
                  blk(OFF_RV // RET_WIDTH), blk(OFF_RG // RET_WIDTH)],
        out_specs=pl.BlockSpec((RET_CHUNK, RET_WIDTH), lambda b, i: (b * n + i, 0)),
        out_shape=jax.ShapeDtypeStruct((batch * seq, RET_WIDTH), BF16),
        scratch_shapes=[pltpu.VMEM((RET_HEADS, RET_DIM, RET_DIM), F32)],
        compiler_params=_params(("parallel", "arbitrary")),
        name="retention",
    )(proj, proj, proj, proj)


def _gelu_tanh(x):
    return 0.5 * x * (1.0 + jnp.tanh(math.sqrt(2.0 / math.pi) * (x + 0.044715 * (x * x * x))))


def _compress_kernel(zk_ref, zv_ref, pos_ref, w1_ref, w2_ref, kc_ref, vc_ref, zf_ref):
    nsub = zf_ref.shape[0] // CMP_STRIDE
    for kv, (z_ref, o_ref) in enumerate(((zk_ref, kc_ref), (zv_ref, vc_ref))):
        zf_ref[...] = z_ref[...].astype(F32)
        hid_a = jnp.zeros((nsub, LANES), F32)
        hid_b = jnp.zeros((nsub, LANES), F32)
        for i in range(CMP_STRIDE):
            rows = zf_ref[pl.ds(i, nsub, stride=CMP_STRIDE), :]
            xa = (rows + pos_ref[kv, i:i + 1, :]).astype(BF16)
            xb = (rows + pos_ref[kv, CMP_STRIDE + i:CMP_STRIDE + i + 1, :]).astype(BF16)
            wa = w1_ref[kv, i * NSA_DIM:(i + 1) * NSA_DIM, :]
            wb = w1_ref[kv, (CMP_STRIDE + i) * NSA_DIM:(CMP_STRIDE + i + 1) * NSA_DIM, :]
            hid_a = hid_a + jnp.dot(xa, wa, preferred_element_type=F32)
            hid_b = hid_b + jnp.dot(xb, wb, preferred_element_type=F32)
        hid = _gelu_tanh(hid_a + pltpu.roll(hid_b, nsub - 1, 0))
        o_ref[...] = jnp.dot(hid.astype(BF16), w2_ref[kv], preferred_element_type=F32).astype(o_ref.dtype)


def _compress(proj, cmp_pos, cmp_w1, cmp_w2, batch, seq):
    nsub = seq // CMP_STRIDE
    zspec = lambda col0: pl.BlockSpec((seq, NSA_DIM), lambda b, g: (b, col0 + g))
    full = lambda a: pl.BlockSpec(a.shape, lambda b, g: (0,) * a.ndim)
    ospec = pl.BlockSpec((None, None, nsub, NSA_DIM), lambda b, g: (b, g, 0, 0))
    oshape = jax.ShapeDtypeStruct((batch, NSA_KV_GROUPS, nsub, NSA_DIM), BF16)
    return pl.pallas_call(
        _compress_kernel,
        grid=(batch, NSA_KV_GROUPS),
        in_specs=[zspec(OFF_KC // NSA_DIM), zspec(OFF_VC // NSA_DIM),
                  full(cmp_pos), full(cmp_w1), full(cmp_w2)],
        out_specs=[ospec, ospec],
        out_shape=[oshape, oshape],
        scratch_shapes=[pltpu.VMEM((seq, NSA_DIM), F32)],
        compiler_params=_params(("parallel", "parallel")),
        name="compress",
    )(proj, proj, cmp_pos, cmp_w1, cmp_w2)


def _nsa_kernel(q_ref, kc_ref, vc_ref, ks_ref, vs_ref, kw_ref, vw_ref, gate_ref, o_ref):
    tq, tk, rep = NSA_TQ, NSA_TK, NSA_REP
    rows = rep * tq
    g = pl.program_id(1)
    i = pl.program_id(2)
    t0 = i * tq

    q4 = jnp.concatenate([q_ref[:, r * NSA_DIM:(r + 1) * NSA_DIM] for r in range(rep)], axis=0)
    q4 = (q4.astype(F32) * (NSA_DIM ** -0.5)).astype(BF16)
    row = lax.broadcasted_iota(jnp.int32, (rows, LANES), 0)
    lane = lax.broadcasted_iota(jnp.int32, (rows, LANES), 1)
    tpos = t0 + (row & (tq - 1))
    head = g * rep + (row >> (tq.bit_length() - 1))
    slope = jnp.exp((head + 1).astype(F32) * (-0.5 * math.log(2.0)))

    s = lax.dot_general(q4, kc_ref[...], _NT, preferred_element_type=F32)
    c_dist_i = tpos - (lane * CMP_STRIDE + CMP_LEN - 1)
    c_mask = c_dist_i >= 0
    s = jnp.where(c_mask, s - slope * c_dist_i.astype(F32), NEG_INF)
    e = jnp.exp(s - jnp.max(s, axis=-1, keepdims=True))
    p_cmp = jnp.where(c_mask, e / jnp.sum(e, axis=-1, keepdims=True), 0.0)
    o_cmp = jnp.dot(p_cmp.astype(BF16), vc_ref[...], preferred_element_type=F32)

    p_sum = p_cmp[0:tq]
    for r in range(1, rep):
        p_sum = p_sum + p_cmp[r * tq:(r + 1) * tq]
    c_idx = lax.broadcasted_iota(jnp.int32, (LANES, LANES), 0)
    s_idx = lax.broadcasted_iota(jnp.int32, (LANES, LANES), 1)
    overlap = jnp.maximum(jnp.minimum(c_idx * CMP_STRIDE + CMP_LEN, s_idx * SLC_LEN + SLC_LEN)
                          - jnp.maximum(c_idx * CMP_STRIDE, s_idx * SLC_LEN), 0)
    overlap = (overlap.astype(F32) * (1.0 / CMP_LEN)).astype(BF16)
    p_hi = p_sum.astype(BF16)
    p_lo = (p_sum - p_hi.astype(F32)).astype(BF16)
    imp = (jnp.dot(p_hi, overlap, preferred_element_type=F32)
           + jnp.dot(p_lo, overlap, preferred_element_type=F32))

    blk = lax.broadcasted_iota(jnp.int32, (tq, LANES), 1)
    cur = (t0 + lax.broadcasted_iota(jnp.int32, (tq, LANES), 0)) >> (SLC_LEN.bit_length() - 1)
    forced = (blk == 0) | (blk == cur) | (blk == cur - 1)
    imp = jnp.where(forced, FORCE, jnp.where(blk > cur, -FORCE, imp))
    rank = jnp.zeros((tq, LANES), F32)
    for j in range(ks_ref.shape[0] // SLC_LEN):
        col = imp[:, j:j + 1]
        before = (col > imp) | ((col == imp) & (blk > j))
        rank = rank + jnp.where(before, 1.0, 0.0)
    sel = jnp.where(rank < SLC_TOPK, 1.0, 0.0).astype(BF16)

    def attend(k_ref, v_ref, first_tile, selected):
        def body(kt, carry):
            m, l, acc = carry
            off = pl.multiple_of(kt * tk, tk)
            k = k_ref[pl.ds(off, tk), :]
            v = v_ref[pl.ds(off, tk), :]
            sc = lax.dot_general(q4, k, _NT, preferred_element_type=F32)
            dist_i = tpos - (off + lane)
            if selected:
                expand = jnp.where(c_idx == ((off + s_idx) >> (SLC_LEN.bit_length() - 1)), 1.0, 0.0)
                km = jnp.dot(sel, expand.astype(BF16), preferred_element_type=F32)
                km = jnp.concatenate([km] * rep, axis=0)
                valid = (km > 0.5) & (dist_i >= 0)
            else:
                valid = (dist_i >= 0) & (dist_i < WIN_LEN)
            sc = jnp.where(valid, sc - slope * dist_i.astype(F32), NEG_INF)
            m_new = jnp.maximum(m, jnp.max(sc, axis=-1, keepdims=True))
            a = jnp.exp(m - m_new)
            p = jnp.where(valid, jnp.exp(sc - m_new), 0.0)
            l = a * l + jnp.sum(p, axis=-1, keepdims=True)
            acc = a * acc + jnp.dot(p.astype(BF16), v, preferred_element_type=F32)
            return m_new, l, acc

        init = (jnp.full((rows, 1), NEG_INF, F32), jnp.zeros((rows, 1), F32),
                jnp.zeros((rows, NSA_DIM), F32))
        _, l, acc = lax.fori_loop(first_tile, i + 1, body, init)
        return acc / l

    o_slc = attend(ks_ref, vs_ref, 0, True)
    o_win = attend(kw_ref, vw_ref, jnp.maximum(i - WIN_LEN // tk, 0), False)

    sig = jax.nn.sigmoid(gate_ref[...])

    def gate_col(idx):
        return jnp.sum(jnp.where(blk == idx, sig, 0.0), axis=-1, keepdims=True)

    for r in range(rep):
        h = g * rep + r
        rs = slice(r * tq, (r + 1) * tq)
        o = (gate_col(h) * o_cmp[rs] + gate_col(NSA_HEADS + h) * o_slc[rs]
             + gate_col(2 * NSA_HEADS + h) * o_win[rs])
        o_ref[:, r * NSA_DIM:(r + 1) * NSA_DIM] = o.astype(o_ref.dtype)


def _nsa(proj, gates, kc, vc, batch, seq):
    nq = seq // NSA_TQ
    gw = NSA_REP * NSA_DIM
    kvspec = lambda col0: pl.BlockSpec((seq, NSA_DIM), lambda b, g, i: (b, col0 + g))
    cspec = pl.BlockSpec((None, None, seq // CMP_STRIDE, NSA_DIM), lambda b, g, i: (b, g, 0, 0))
    return pl.pallas_call(
        _nsa_kernel,
        grid=(batch, NSA_KV_GROUPS, nq),
        in_specs=[pl.BlockSpec((NSA_TQ, gw), lambda b, g, i: (b * nq + i, OFF_NQ // gw + g)),
                  cspec, cspec,
                  kvspec(OFF_KS // NSA_DIM), kvspec(OFF_VS // NSA_DIM),
                  kvspec(OFF_KW // NSA_DIM), kvspec(OFF_VW // NSA_DIM),
                  pl.BlockSpec((NSA_TQ, LANES), lambda b, g, i: (b * nq + i, 0))],
        out_specs=pl.BlockSpec((NSA_TQ, gw), lambda b, g, i: (b * nq + i, g)),
        out_shape=jax.ShapeDtypeStruct((batch * seq, NSA_WIDTH), BF16),
        compiler_params=_params(("parallel", "parallel", "arbitrary")),
        name="nsa",
    )(proj, kc, vc, proj, proj, proj, proj, gates)


def _outproj_kernel(ret_ref, nsa_ref, wr_ref, wn_ref, x_ref, g_ref, b_ref, o_ref):
    j = pl.program_id(1)
    nj = pl.num_programs(1)
    tn = x_ref.shape[1]
    y = (jnp.dot(ret_ref[...], wr_ref[...], preferred_element_type=F32)
         + jnp.dot(nsa_ref[...], wn_ref[...], preferred_element_type=F32))
    y = DN_ALPHA * x_ref[...] + y
    for jj in range(o_ref.shape[1] // tn):
        @pl.when(j == jj)
        def _(jj=jj):
            o_ref[:, jj * tn:(jj + 1) * tn] = y

    @pl.when(j == nj - 1)
    def _():
        _layer_norm_rows(o_ref, lambda rows: o_ref[rows, :], g_ref, b_ref)


def _outproj(ret, nsa, w_out, x, g, b, tm, tn):
    m, d = x.shape
    vec = pl.BlockSpec((1, d), lambda i, j: (0, 0))
    return pl.pallas_call(
        _outproj_kernel,
        grid=(m // tm, d // tn),
        in_specs=[pl.BlockSpec((tm, RET_WIDTH), lambda i, j: (i, 0)),
                  pl.BlockSpec((tm, NSA_WIDTH), lambda i, j: (i, 0)),
                  pl.BlockSpec((RET_WIDTH, tn), lambda i, j: (0, j)),
                  pl.BlockSpec((NSA_WIDTH, tn), lambda i, j: (RET_WIDTH // NSA_WIDTH, j)),
                  pl.BlockSpec((tm, tn), lambda i, j: (i, j)),
                  vec, vec],
        out_specs=pl.BlockSpec((tm, d), lambda i, j: (i, 0)),
        out_shape=jax.ShapeDtypeStruct((m, d), F32),
        compiler_params=_params(("parallel", "arbitrary")),
        name="outproj_ln",
    )(ret, nsa, w_out, w_out, x, g, b)


def _xattn_kernel(h_ref, kv_ref, wq_ref, wo_ref, g_ref, b_ref, o_ref):
    q = jnp.dot(h_ref[...].astype(BF16), wq_ref[...], preferred_element_type=F32).astype(BF16)
    outs = []
    for hd in range(XA_HEADS):
        sl = slice(hd * XA_DIM, (hd + 1) * XA_DIM)
        k = kv_ref[:, sl]
        v = kv_ref[:, XA_WIDTH + hd * XA_DIM:XA_WIDTH + (hd + 1) * XA_DIM]
        s = lax.dot_general(q[:, sl], k, _NT, preferred_element_type=F32) * (XA_DIM ** -0.5)
        e = jnp.exp(s - jnp.max(s, axis=-1, keepdims=True))
        p = e / jnp.sum(e, axis=-1, keepdims=True)
        outs.append(jnp.dot(p.astype(BF16), v, preferred_element_type=F32))
    o = jnp.concatenate(outs, axis=-1).astype(BF16)
    o_ref[...] = jnp.dot(o, wo_ref[...], preferred_element_type=F32)
    _layer_norm_rows(o_ref, lambda rows: DN_ALPHA * h_ref[rows, :] + o_ref[rows, :], g_ref, b_ref)


def _xattn(h, kv, wq, wo, g, b, seq, mem_len, tm):
    m, d = h.shape
    per_batch = seq // tm
    full = lambda a: pl.BlockSpec(a.shape, lambda i: (0, 0))
    return pl.pallas_call(
        _xattn_kernel,
        grid=(m // tm,),
        in_specs=[pl.BlockSpec((tm, d), lambda i: (i, 0)),
                  pl.BlockSpec((mem_len, 2 * XA_WIDTH), lambda i: (i // per_batch, 0)),
                  full(wq), full(wo), full(g), full(b)],
        out_specs=pl.BlockSpec((tm, d), lambda i: (i, 0)),
        out_shape=jax.ShapeDtypeStruct((m, d), F32),
        compiler_params=_params(("parallel",)),
        name="xattn_ln",
    )(h, kv, wq, wo, g, b)


def _ffn_kernel(h_ref, w1_ref, w2_ref, g_ref, b_ref, o_ref, hb_ref):
    f = pl.program_id(1)
    nf = pl.num_programs(1)

    @pl.when(f == 0)
    def _():
        hb_ref[...] = h_ref[...].astype(BF16)
        o_ref[...] = jnp.zeros_like(o_ref)

    u = jnp.maximum(jnp.dot(hb_ref[...], w1_ref[...], preferred_element_type=F32), 0.0)
    u = (u * u).astype(BF16)
    for c in range(o_ref.shape[1] // FFN_TN):
        cols = slice(c * FFN_TN, (c + 1) * FFN_TN)
        o_ref[:, cols] += jnp.dot(u, w2_ref[:, cols], preferred_element_type=F32)

    @pl.when(f == nf - 1)
    def _():
        _layer_norm_rows(o_ref, lambda rows: DN_ALPHA * h_ref[rows, :] + o_ref[rows, :], g_ref, b_ref)


def _ffn(h, w1, w2, g, b, tm, tf):
    m, d = h.shape
    d_ff = w1.shape[1]
    vec = pl.BlockSpec((1, d), lambda i, f: (0, 0))
    return pl.pallas_call(
        _ffn_kernel,
        grid=(m // tm, d_ff // tf),
        in_specs=[pl.BlockSpec((tm, d), lambda i, f: (i, 0)),
                  pl.BlockSpec((d, tf), lambda i, f: (0, f)),
                  pl.BlockSpec((tf, d), lambda i, f: (f, 0)),
                  vec, vec],
        out_specs=pl.BlockSpec((tm, d), lambda i, f: (i, 0)),
        out_shape=jax.ShapeDtypeStruct((m, d), F32),
        scratch_shapes=[pltpu.VMEM((tm, d), BF16)],
        compiler_params=_params(("parallel", "arbitrary")),
        name="ffn_ln",
    )(h, w1, w2, g, b)


def kernel(x, mem, w_in, w_out, cmp_pos, cmp_w1, cmp_w2, xa_wq, xa_wkv, xa_wo,
           w_ff1, w_ff2, ln_g, ln_b):
    batch, seq, d = x.shape
    mem_len = mem.shape[1]
    h = x.reshape(batch * seq, d)
    mem2 = mem.reshape(batch * mem_len, d).astype(BF16)
    for l in range(DEPTH):
        hb = h.astype(BF16)
        w_in_b = w_in[l].astype(BF16)
        w_gate = jnp.pad(w_in_b[:, OFF_GATES:], ((0, 0), (0, LANES - GATE_WIDTH)))
        proj = _matmul(hb, w_in_b, OFF_GATES, 1024, 1024, BF16)
        gates = _matmul(hb, w_gate, LANES, 1024, LANES, F32)
        ret = _retention(proj, batch, seq)
        kc, vc = _compress(proj, cmp_pos[l], cmp_w1[l].astype(BF16), cmp_w2[l].astype(BF16),
                           batch, seq)
        sparse = _nsa(proj, gates, kc, vc, batch, seq)
        vecs = lambda a, k: a[l, k].reshape(1, d)
        h = _outproj(ret, sparse, w_out[l].astype(BF16), h, vecs(ln_g, 0), vecs(ln_b, 0), 512, 1024)
        kv = _matmul(mem2, xa_wkv[l].astype(BF16), 2 * XA_WIDTH, 512, 1024, BF16)
        h = _xattn(h, kv, xa_wq[l].astype(BF16), xa_wo[l].astype(BF16),
                   vecs(ln_g, 1), vecs(ln_b, 1), seq, mem_len, 256)
        h = _ffn(h, w_ff1[l].astype(BF16), w_ff2[l].astype(BF16),
                 vecs(ln_g, 2), vecs(ln_b, 2), 512, 512)
    return h.reshape(batch, seq, d)
```

```python
import functools
import math

import jax
import jax.numpy as jnp
from jax import lax
from jax.experimental import pallas as pl
from jax.experimental.pallas import tpu as pltpu

F32 = jnp.float32
BF16 = jnp.bfloat16

D_MODEL = 4096
RET_HEADS = 8
RET_DIM = 256
RET_WIDTH = RET_HEADS * RET_DIM
RET_CHUNK = 128
NSA_HEADS = 16
NSA_KV_GROUPS = 4
NSA_REP = NSA_HEADS // NSA_KV_GROUPS
NSA_DIM = 128
NSA_WIDTH = NSA_HEADS * NSA_DIM
NSA_KV_WIDTH = NSA_KV_GROUPS * NSA_DIM
CMP_LEN = 32
CMP_STRIDE = 16
CMP_SUB = CMP_LEN // CMP_STRIDE
SLC_LEN = 64
SLC_TOPK = 16
WIN_LEN = 512
MIX_WIDTH = RET_WIDTH + NSA_WIDTH
GATE_WIDTH = 3 * NSA_HEADS
OFF_RQ, OFF_RK, OFF_RV, OFF_RG = 0, RET_WIDTH, 2 * RET_WIDTH, 3 * RET_WIDTH
OFF_NQ = 4 * RET_WIDTH
OFF_KC = OFF_NQ + NSA_WIDTH
OFF_VC = OFF_KC + NSA_KV_WIDTH
OFF_KS = OFF_VC + NSA_KV_WIDTH
OFF_VS = OFF_KS + NSA_KV_WIDTH
OFF_KW = OFF_VS + NSA_KV_WIDTH
OFF_VW = OFF_KW + NSA_KV_WIDTH
OFF_GATES = OFF_VW + NSA_KV_WIDTH
XA_HEADS = 4
XA_DIM = 128
XA_WIDTH = XA_HEADS * XA_DIM
LN_EPS = 1e-5
DEPTH = 1
DN_ALPHA = (2.0 * DEPTH) ** 0.25
NEG_INF = -1e30
MASKED = 2.0 * NEG_INF
FORCE = 1e9

LANES = 128
NSA_TQ = 128
NSA_TK = 128
LN_ROWS = 8
LN_UNROLL = 4
FFN_TN = 1024
VMEM_LIMIT = 56 * 1024 * 1024

_NT = (((1,), (1,)), ((), ()))
_TN = (((0,), (0,)), ((), ()))


def _params(sem):
    return pltpu.CompilerParams(dimension_semantics=sem, vmem_limit_bytes=VMEM_LIMIT)


def _row_stats(tm):
    return [pltpu.VMEM((tm, 1), F32), pltpu.VMEM((tm, 1), F32)]


def _layer_norm_rows(o_ref, pre_norm, g_ref, b_ref, mu_ref, rstd_ref):
    n = o_ref.shape[0] // LN_ROWS

    def stats(c, carry):
        rows = pl.ds(pl.multiple_of(c * LN_ROWS, LN_ROWS), LN_ROWS)
        y = pre_norm(rows)
        mu = jnp.mean(y, axis=-1, keepdims=True)
        d = y - mu
        mu_ref[rows, :] = mu
        rstd_ref[rows, :] = lax.rsqrt(jnp.mean(d * d, axis=-1, keepdims=True) + LN_EPS)
        return carry

    def apply(c, carry):
        rows = pl.ds(pl.multiple_of(c * LN_ROWS, LN_ROWS), LN_ROWS)
        o_ref[rows, :] = (pre_norm(rows) - mu_ref[rows, :]) * rstd_ref[rows, :] * g_ref[...] + b_ref[...]
        return carry

    lax.fori_loop(0, n, stats, 0, unroll=LN_UNROLL)
    lax.fori_loop(0, n, apply, 0, unroll=LN_UNROLL)


def _mm_kernel(x_ref, w_ref, o_ref):
    o_ref[...] = jnp.dot(x_ref[...], w_ref[...], preferred_element_type=F32).astype(o_ref.dtype)


def _matmul(x, w, n_out, tm, tn, out_dtype):
    m, k = x.shape
    return pl.pallas_call(
        _mm_kernel,
        grid=(m // tm, n_out // tn),
        in_specs=[pl.BlockSpec((tm, k), lambda i, j: (i, 0)),
                  pl.BlockSpec((k, tn), lambda i, j: (0, j))],
        out_specs=pl.BlockSpec((tm, tn), lambda i, j: (i, j)),
        out_shape=jax.ShapeDtypeStruct((m, n_out), out_dtype),
        compiler_params=_params(("parallel", "parallel")),
        name="matmul",
    )(x, w)


def _mm_nt_kernel(w_ref, x_ref, o_ref):
    o_ref[...] = lax.dot_general(w_ref[...], x_ref[...], _NT, preferred_element_type=F32)


def _matmul_nt(w, x, tm):
    n, k = w.shape
    m = x.shape[0]
    return pl.pallas_call(
        _mm_nt_kernel,
        grid=(m // tm,),
        in_specs=[pl.BlockSpec((n, k), lambda i: (0, 0)),
                  pl.BlockSpec((tm, k), lambda i: (i, 0))],
        out_specs=pl.BlockSpec((n, tm), lambda i: (0, i)),
        out_shape=jax.ShapeDtypeStruct((n, m), F32),
        compiler_params=_params(("parallel",)),
        name="matmul_nt",
    )(w, x)


def _retention_kernel(q_ref, k_ref, v_ref, g_ref, o_ref, state_ref):
    c = RET_CHUNK

    @pl.when(pl.program_id(1) == 0)
    def _():
        state_ref[...] = jnp.zeros_like(state_ref)

    diff = (lax.broadcasted_iota(jnp.int32, (c, c), 0)
            - lax.broadcasted_iota(jnp.int32, (c, c), 1)).astype(F32)
    pos = lax.broadcasted_iota(jnp.int32, (c, RET_DIM), 0).astype(F32)
    for h in range(RET_HEADS):
        log_g = math.log1p(-(2.0 ** (-5.0 - h)))
        sl = slice(h * RET_DIM, (h + 1) * RET_DIM)
        q = q_ref[:, sl].astype(F32)
        k = k_ref[:, sl].astype(F32) * (RET_DIM ** -0.5)
        v = v_ref[:, sl]
        decay = jnp.where(diff >= 0, jnp.exp(log_g * jnp.maximum(diff, 0.0)), 0.0)
        scores = lax.dot_general(q.astype(BF16), k.astype(BF16), _NT,
                                 preferred_element_type=F32) * decay
        o_inner = jnp.dot(scores.astype(BF16), v, preferred_element_type=F32)
        k_tail = k * jnp.exp(log_g * (c - 1.0 - pos))
        kv = lax.dot_general(k_tail.astype(BF16), v, _TN, preferred_element_type=F32)
        q_head = q * jnp.exp(log_g * (pos + 1.0))
        state = state_ref[h]
        o = o_inner + jnp.dot(q_head.astype(BF16), state.astype(BF16), preferred_element_type=F32)
        state_ref[h] = math.exp(log_g * c) * state + kv
        mu = jnp.mean(o, axis=-1, keepdims=True)
        d = o - mu
        var = jnp.mean(d * d, axis=-1, keepdims=True)
        o = d * lax.rsqrt(var + LN_EPS)
        gate = g_ref[:, sl].astype(F32)
        o_ref[:, sl] = (o * (gate * jax.nn.sigmoid(gate))).astype(o_ref.dtype)


def _retention(proj, batch, seq):
    n = seq // RET_CHUNK
    blk = lambda col: pl.BlockSpec((RET_CHUNK, RET_WIDTH), lambda b, i: (b * n + i, col))
    return pl.pallas_call(
        _retention_kernel,
        grid=(batch, n),
        in_specs=[blk(OFF_RQ // RET_WIDTH), blk(OFF_RK // RET_WIDTH),
                  blk(OFF_RV // RET_WIDTH), blk(OFF_RG // RET_WIDTH)],
        out_specs=pl.BlockSpec((RET_CHUNK, RET_WIDTH), lambda b, i: (b * n + i, 0)),
        out_shape=jax.ShapeDtypeStruct((batch * seq, RET_WIDTH), BF16),
        scratch_shapes=[pltpu.VMEM((RET_HEADS, RET_DIM, RET_DIM), F32)],
        compiler_params=_params(("parallel", "arbitrary")),
        name="retention",
    )(proj, proj, proj, proj)


def _gelu_tanh(x):
    return 0.5 * x * (1.0 + jnp.tanh(math.sqrt(2.0 / math.pi) * (x + 0.044715 * (x * x * x))))


def _compress_kernel(zk_ref, zv_ref, pos_ref, w1_ref, w2_ref, kc_ref, vc_ref, zf_ref):
    nsub = zf_ref.shape[0] // CMP_STRIDE
    for kv, (z_ref, o_ref) in enumerate(((zk_ref, kc_ref), (zv_ref, vc_ref))):
        zf_ref[...] = z_ref[...].astype(F32)
        hid_a = jnp.zeros((nsub, LANES), F32)
        hid_b = jnp.zeros((nsub, LANES), F32)
        for i in range(CMP_STRIDE):
            rows = zf_ref[pl.ds(i, nsub, stride=CMP_STRIDE), :]
            xa = (rows + pos_ref[kv, i:i + 1, :]).astype(BF16)
            xb = (rows + pos_ref[kv, CMP_STRIDE + i:CMP_STRIDE + i + 1, :]).astype(BF16)
            wa = w1_ref[kv, i * NSA_DIM:(i + 1) * NSA_DIM, :]
            wb = w1_ref[kv, (CMP_STRIDE + i) * NSA_DIM:(CMP_STRIDE + i + 1) * NSA_DIM, :]
            hid_a = hid_a + jnp.dot(xa, wa, preferred_element_type=F32)
            hid_b = hid_b + jnp.dot(xb, wb, preferred_element_type=F32)
        hid = _gelu_tanh(hid_a + pltpu.roll(hid_b, nsub - 1, 0))
        o_ref[...] = jnp.dot(hid.astype(BF16), w2_ref[kv], preferred_element_type=F32).astype(o_ref.dtype)


def _compress(proj, cmp_pos, cmp_w1, cmp_w2, batch, seq):
    nsub = seq // CMP_STRIDE
    zspec = lambda col0: pl.BlockSpec((seq, NSA_DIM), lambda b, g: (b, col0 + g))
    full = lambda a: pl.BlockSpec(a.shape, lambda b, g: (0,) * a.ndim)
    ospec = pl.BlockSpec((None, None, nsub, NSA_DIM), lambda b, g: (b, g, 0, 0))
    oshape = jax.ShapeDtypeStruct((batch, NSA_KV_GROUPS, nsub, NSA_DIM), BF16)
    return pl.pallas_call(
        _compress_kernel,
        grid=(batch, NSA_KV_GROUPS),
        in_specs=[zspec(OFF_KC // NSA_DIM), zspec(OFF_VC // NSA_DIM),
                  full(cmp_pos), full(cmp_w1), full(cmp_w2)],
        out_specs=[ospec, ospec],
        out_shape=[oshape, oshape],
        scratch_shapes=[pltpu.VMEM((seq, NSA_DIM), F32)],
        compiler_params=_params(("parallel", "parallel")),
        name="compress",
    )(proj, proj, cmp_pos, cmp_w1, cmp_w2)


def _nsa_kernel(q_ref, kc_ref, vc_ref, ks_ref, vs_ref, kw_ref, vw_ref, gt_ref, o_ref,
                q4_ref, vct_ref, vst_ref, vwt_ref, sk_ref, bias_ref, sig_ref, m_ref, l_ref, acc_ref, out_ref):
    tq, tk, rep, groups = NSA_TQ, NSA_TK, NSA_REP, NSA_KV_GROUPS
    cols = rep * tq
    n_tiles = ks_ref.shape[0] // tk
    n_blocks = ks_ref.shape[0] // SLC_LEN
    per_tile = tk // SLC_LEN
    slc, win = 0, 1
    i = pl.program_id(1)
    t0 = i * tq
    gcols = lambda g: slice(g * NSA_DIM, (g + 1) * NSA_DIM)

    @pl.when(i == 0)
    def _():
        for g in range(groups):
            vct_ref[g] = vc_ref[g].astype(F32).T.astype(BF16)

        def transpose_tile(t, carry):
            rows = pl.ds(pl.multiple_of(t * tk, tk), tk)
            for g in range(groups):
                vst_ref[g, t] = vs_ref[rows, gcols(g)].astype(F32).T.astype(BF16)
                vwt_ref[g, t] = vw_ref[rows, gcols(g)].astype(F32).T.astype(BF16)
            return carry

        lax.fori_loop(0, n_tiles, transpose_tile, 0)

    sub = lax.broadcasted_iota(jnp.int32, (tk, cols), 0)
    t_loc = lax.broadcasted_iota(jnp.int32, (tk, cols), 1) & (tq - 1)
    lane_head = lax.broadcasted_iota(jnp.int32, (1, cols), 1) >> (tq.bit_length() - 1)
    causal = sub <= t_loc
    c_end = sub * CMP_STRIDE + (CMP_LEN - 1)
    c_mask = c_end <= t0 + t_loc
    s_idx = lax.broadcasted_iota(jnp.int32, (n_blocks, tk), 0)
    c_idx = lax.broadcasted_iota(jnp.int32, (n_blocks, tk), 1)
    overlap = jnp.maximum(jnp.minimum(c_idx * CMP_STRIDE + CMP_LEN, s_idx * SLC_LEN + SLC_LEN)
                          - jnp.maximum(c_idx * CMP_STRIDE, s_idx * SLC_LEN), 0)
    overlap = (overlap.astype(F32) * (1.0 / CMP_LEN)).astype(BF16)
    blk = lax.broadcasted_iota(jnp.int32, (n_blocks, tq), 0)
    cur = (t0 + lax.broadcasted_iota(jnp.int32, (n_blocks, tq), 1)) >> (SLC_LEN.bit_length() - 1)
    forced = (blk == 0) | (blk == cur) | (blk == cur - 1)
    future = blk > cur
    tile_start = (lax.broadcasted_iota(jnp.int32, (n_blocks, cols), 0) // per_tile) * tk

    sig_ref[...] = jax.nn.sigmoid(gt_ref[...])

    def gate_row(g, branch):
        first = branch * NSA_HEADS + g * rep
        return jnp.concatenate([sig_ref[first + r:first + r + 1, :] for r in range(rep)], axis=1)

    def slope_row(g):
        return sk_ref[g, 1:2, :]

    for g in range(groups):
        q4 = jnp.concatenate([q_ref[:, (g * rep + r) * NSA_DIM:(g * rep + r + 1) * NSA_DIM]
                              for r in range(rep)], axis=0)
        q4_ref[g] = (q4.astype(F32) * (NSA_DIM ** -0.5)).astype(BF16)
        slope = jnp.exp((g * rep + lane_head + 1).astype(F32) * (-0.5 * math.log(2.0)))
        sk_ref[g] = slope * sub.astype(F32)

        st = lax.dot_general(kc_ref[g], q4_ref[g], _NT, preferred_element_type=F32)
        st = jnp.where(c_mask, st + slope * c_end.astype(F32), MASKED)
        e = jnp.exp(st - jnp.max(st, axis=0, keepdims=True))
        p_cmp = jnp.where(c_mask, e * (1.0 / jnp.sum(e, axis=0, keepdims=True)), 0.0)
        o_cmp = jnp.dot(vct_ref[g], p_cmp.astype(BF16), preferred_element_type=F32)
        out_ref[g] = gate_row(g, 0) * o_cmp

        p_sum = p_cmp[:, 0:tq]
        for r in range(1, rep):
            p_sum = p_sum + p_cmp[:, r * tq:(r + 1) * tq]
        p_hi = p_sum.astype(BF16)
        p_lo = (p_sum - p_hi.astype(F32)).astype(BF16)
        imp = (jnp.dot(overlap, p_hi, preferred_element_type=F32)
               + jnp.dot(overlap, p_lo, preferred_element_type=F32))
        imp = jnp.where(forced, FORCE, jnp.where(future, -FORCE, imp))
        rank = jnp.zeros((n_blocks, tq), F32)
        for j in range(n_blocks):
            row = imp[j:j + 1, :]
            before = (row > imp) | ((row == imp) & (blk > j))
            rank = rank + jnp.where(before, 1.0, 0.0)
        sel_bias = jnp.where(rank < SLC_TOPK, 0.0, MASKED)
        bias_ref[g] = jnp.concatenate([sel_bias] * rep, axis=1) + slope * tile_start.astype(F32)

    def tile(branch, g, k_ref, vt_ref, kt, bias_top, bias_bot, keep, first):
        off = pl.multiple_of(kt * tk, tk)
        st = lax.dot_general(k_ref[pl.ds(off, tk), gcols(g)], q4_ref[g], _NT, preferred_element_type=F32)
        st = st + sk_ref[g]
        half = tk // 2
        st = jnp.concatenate([st[:half] + bias_top, st[half:] + bias_bot], axis=0)
        if keep is not None:
            st = jnp.where(keep, st, MASKED)
        m_tile = jnp.max(st, axis=0, keepdims=True)
        if first:
            p = jnp.exp(st - m_tile)
            m_ref[branch, g] = m_tile
            l_ref[branch, g] = jnp.sum(p, axis=0, keepdims=True)
            acc_ref[branch, g] = jnp.dot(vt_ref[g, kt], p.astype(BF16), preferred_element_type=F32)
        else:
            m_old = m_ref[branch, g]
            m_new = jnp.maximum(m_old, m_tile)
            a = jnp.exp(m_old - m_new)
            p = jnp.exp(st - m_new)
            m_ref[branch, g] = m_new
            l_ref[branch, g] = a * l_ref[branch, g] + jnp.sum(p, axis=0, keepdims=True)
            acc_ref[branch, g] = a * acc_ref[branch, g] + jnp.dot(
                vt_ref[g, kt], p.astype(BF16), preferred_element_type=F32)

    def slc_tile(g, kt, keep, first):
        tile(slc, g, ks_ref, vst_ref, kt, bias_ref[g, pl.ds(per_tile * kt, 1), :],
             bias_ref[g, pl.ds(per_tile * kt + 1, 1), :], keep, first)

    def win_tile(g, kt, keep, first):
        b = slope_row(g) * (kt * tk).astype(F32)
        tile(win, g, kw_ref, vwt_ref, kt, b, b, keep, first)

    for g in range(groups):
        slc_tile(g, i, causal, True)
        win_tile(g, i, causal, True)

    def slc_body(kt, carry):
        for g in range(groups):
            slc_tile(g, kt, None, False)
        return carry

    lax.fori_loop(0, i, slc_body, 0)

    n_back = WIN_LEN // tk

    @pl.when(i >= n_back)
    def _():
        for g in range(groups):
            win_tile(g, i - n_back, sub > t_loc, False)

    def win_body(kt, carry):
        for g in range(groups):
            win_tile(g, kt, None, False)
        return carry

    lax.fori_loop(jnp.maximum(i - (n_back - 1), 0), i, win_body, 0)

    for g in range(groups):
        o = out_ref[g]
        for branch, state in ((1, slc), (2, win)):
            o = o + (gate_row(g, branch) * (1.0 / l_ref[state, g])) * acc_ref[state, g]
        for r in range(rep):
            h = g * rep + r
            o_ref[:, h * NSA_DIM:(h + 1) * NSA_DIM] = o[:, r * tq:(r + 1) * tq].T.astype(o_ref.dtype)


def _nsa(proj, gates_t, kc, vc, batch, seq):
    nq = seq // NSA_TQ
    groups, gw = NSA_KV_GROUPS, NSA_REP * NSA_DIM
    n_tiles = seq // NSA_TK
    kvspec = lambda off: pl.BlockSpec((seq, NSA_KV_WIDTH), lambda b, i: (b, off // NSA_KV_WIDTH))
    cspec = lambda a: pl.BlockSpec((None,) + a.shape[1:], lambda b, i: (b, 0, 0, 0))
    return pl.pallas_call(
        _nsa_kernel,
        grid=(batch, nq),
        in_specs=[pl.BlockSpec((NSA_TQ, NSA_WIDTH), lambda b, i: (b * nq + i, OFF_NQ // NSA_WIDTH)),
                  cspec(kc), cspec(vc),
                  kvspec(OFF_KS), kvspec(OFF_VS), kvspec(OFF_KW), kvspec(OFF_VW),
                  pl.BlockSpec((LANES, NSA_TQ), lambda b, i: (0, b * nq + i))],
        out_specs=pl.BlockSpec((NSA_TQ, NSA_WIDTH), lambda b, i: (b * nq + i, 0)),
        out_shape=jax.ShapeDtypeStruct((batch * seq, NSA_WIDTH), BF16),
        scratch_shapes=[pltpu.VMEM((groups, gw, NSA_DIM), BF16),
                        pltpu.VMEM((groups, NSA_DIM, seq // CMP_STRIDE), BF16),
                        pltpu.VMEM((groups, n_tiles, NSA_DIM, NSA_TK), BF16),
                        pltpu.VMEM((groups, n_tiles, NSA_DIM, NSA_TK), BF16),
                        pltpu.VMEM((groups, NSA_TK, gw), F32),
                        pltpu.VMEM((groups, seq // SLC_LEN, gw), F32),
                        pltpu.VMEM((LANES, NSA_TQ), F32),
                        pltpu.VMEM((2, groups, 1, gw), F32),
                        pltpu.VMEM((2, groups, 1, gw), F32),
                        pltpu.VMEM((2, groups, NSA_DIM, gw), F32),
                        pltpu.VMEM((groups, NSA_DIM, gw), F32)],
        compiler_params=_params(("parallel", "arbitrary")),
        name="nsa",
    )(proj, kc, vc, proj, proj, proj, proj, gates_t)


def _outproj_kernel(ret_ref, nsa_ref, wr_ref, wn_ref, x_ref, g_ref, b_ref, o_ref, mu_ref, rstd_ref):
    j = pl.program_id(1)
    nj = pl.num_programs(1)
    tn = x_ref.shape[1]
    y = (jnp.dot(ret_ref[...], wr_ref[...], preferred_element_type=F32)
         + jnp.dot(nsa_ref[...], wn_ref[...], preferred_element_type=F32))
    y = DN_ALPHA * x_ref[...] + y
    for jj in range(o_ref.shape[1] // tn):
        @pl.when(j == jj)
        def _(jj=jj):
            o_ref[:, jj * tn:(jj + 1) * tn] = y

    @pl.when(j == nj - 1)
    def _():
        _layer_norm_rows(o_ref, lambda rows: o_ref[rows, :], g_ref, b_ref, mu_ref, rstd_ref)


def _outproj(ret, nsa, w_out, x, g, b, tm, tn):
    m, d = x.shape
    vec = pl.BlockSpec((1, d), lambda i, j: (0, 0))
    return pl.pallas_call(
        _outproj_kernel,
        grid=(m // tm, d // tn),
        in_specs=[pl.BlockSpec((tm, RET_WIDTH), lambda i, j: (i, 0)),
                  pl.BlockSpec((tm, NSA_WIDTH), lambda i, j: (i, 0)),
                  pl.BlockSpec((RET_WIDTH, tn), lambda i, j: (0, j)),
                  pl.BlockSpec((NSA_WIDTH, tn), lambda i, j: (RET_WIDTH // NSA_WIDTH, j)),
                  pl.BlockSpec((tm, tn), lambda i, j: (i, j)),
                  vec, vec],
        out_specs=pl.BlockSpec((tm, d), lambda i, j: (i, 0)),
        out_shape=jax.ShapeDtypeStruct((m, d), F32),
        scratch_shapes=_row_stats(tm),
        compiler_params=_params(("parallel", "arbitrary")),
        name="outproj_ln",
    )(ret, nsa, w_out, w_out, x, g, b)


def _xattn_kernel(h_ref, kv_ref, wq_ref, wo_ref, g_ref, b_ref, o_ref, mu_ref, rstd_ref):
    q = jnp.dot(h_ref[...].astype(BF16), wq_ref[...], preferred_element_type=F32).astype(BF16)
    outs = []
    for hd in range(XA_HEADS):
        sl = slice(hd * XA_DIM, (hd + 1) * XA_DIM)
        k = kv_ref[:, sl]
        v = kv_ref[:, XA_WIDTH + hd * XA_DIM:XA_WIDTH + (hd + 1) * XA_DIM]
        s = lax.dot_general(q[:, sl], k, _NT, preferred_element_type=F32) * (XA_DIM ** -0.5)
        e = jnp.exp(s - jnp.max(s, axis=-1, keepdims=True))
        p = e / jnp.sum(e, axis=-1, keepdims=True)
        outs.append(jnp.dot(p.astype(BF16), v, preferred_element_type=F32))
    o = jnp.concatenate(outs, axis=-1).astype(BF16)
    o_ref[...] = jnp.dot(o, wo_ref[...], preferred_element_type=F32)
    _layer_norm_rows(o_ref, lambda rows: DN_ALPHA * h_ref[rows, :] + o_ref[rows, :], g_ref, b_ref,
                     mu_ref, rstd_ref)


def _xattn(h, kv, wq, wo, g, b, seq, mem_len, tm):
    m, d = h.shape
    per_batch = seq // tm
    full = lambda a: pl.BlockSpec(a.shape, lambda i: (0, 0))
    return pl.pallas_call(
        _xattn_kernel,
        grid=(m // tm,),
        in_specs=[pl.BlockSpec((tm, d), lambda i: (i, 0)),
                  pl.BlockSpec((mem_len, 2 * XA_WIDTH), lambda i: (i // per_batch, 0)),
                  full(wq), full(wo), full(g), full(b)],
        out_specs=pl.BlockSpec((tm, d), lambda i: (i, 0)),
        out_shape=jax.ShapeDtypeStruct((m, d), F32),
        scratch_shapes=_row_stats(tm),
        compiler_params=_params(("parallel",)),
        name="xattn_ln",
    )(h, kv, wq, wo, g, b)


def _ffn_kernel(h_ref, w1_ref, w2_ref, g_ref, b_ref, o_ref, hb_ref, mu_ref, rstd_ref):
    f = pl.program_id(1)
    nf = pl.num_programs(1)

    @pl.when(f == 0)
    def _():
        hb_ref[...] = h_ref[...].astype(BF16)
        o_ref[...] = jnp.zeros_like(o_ref)

    u = jnp.maximum(jnp.dot(hb_ref[...], w1_ref[...], preferred_element_type=F32), 0.0)
    u = (u * u).astype(BF16)
    for c in range(o_ref.shape[1] // FFN_TN):
        cols = slice(c * FFN_TN, (c + 1) * FFN_TN)
        o_ref[:, cols] += jnp.dot(u, w2_ref[:, cols], preferred_element_type=F32)

    @pl.when(f == nf - 1)
    def _():
        _layer_norm_rows(o_ref, lambda rows: DN_ALPHA * h_ref[rows, :] + o_ref[rows, :], g_ref, b_ref,
                     mu_ref, rstd_ref)


def _ffn(h, w1, w2, g, b, tm, tf):
    m, d = h.shape
    d_ff = w1.shape[1]
    vec = pl.BlockSpec((1, d), lambda i, f: (0, 0))
    return pl.pallas_call(
        _ffn_kernel,
        grid=(m // tm, d_ff // tf),
        in_specs=[pl.BlockSpec((tm, d), lambda i, f: (i, 0)),
                  pl.BlockSpec((d, tf), lambda i, f: (0, f)),
                  pl.BlockSpec((tf, d), lambda i, f: (f, 0)),
                  vec, vec],
        out_specs=pl.BlockSpec((tm, d), lambda i, f: (i, 0)),
        out_shape=jax.ShapeDtypeStruct((m, d), F32),
        scratch_shapes=[pltpu.VMEM((tm, d), BF16)] + _row_stats(tm),
        compiler_params=_params(("parallel", "arbitrary")),
        name="ffn_ln",
    )(h, w1, w2, g, b)


def kernel(x, mem, w_in, w_out, cmp_pos, cmp_w1, cmp_w2, xa_wq, xa_wkv, xa_wo,
           w_ff1, w_ff2, ln_g, ln_b):
    batch, seq, d = x.shape
    mem_len = mem.shape[1]
    h = x.reshape(batch * seq, d)
    mem2 = mem.reshape(batch * mem_len, d).astype(BF16)
    for l in range(DEPTH):
        hb = h.astype(BF16)
        w_in_b = w_in[l].astype(BF16)
        w_gate_t = jnp.pad(w_in_b[:, OFF_GATES:].T, ((0, LANES - GATE_WIDTH), (0, 0)))
        proj = _matmul(hb, w_in_b, OFF_GATES, 1024, 1024, BF16)
        gates_t = _matmul_nt(w_gate_t, hb, 1024)
        ret = _retention(proj, batch, seq)
        kc, vc = _compress(proj, cmp_pos[l], cmp_w1[l].astype(BF16), cmp_w2[l].astype(BF16),
                           batch, seq)
        sparse = _nsa(proj, gates_t, kc, vc, batch, seq)
        vecs = lambda a, k: a[l, k].reshape(1, d)
        h = _outproj(ret, sparse, w_out[l].astype(BF16), h, vecs(ln_g, 0), vecs(ln_b, 0), 512, 1024)
        kv = _matmul(mem2, xa_wkv[l].astype(BF16), 2 * XA_WIDTH, 512, 1024, BF16)
        h = _xattn(h, kv, xa_wq[l].astype(BF16), xa_wo[l].astype(BF16),
                   vecs(ln_g, 1), vecs(ln_b, 1), seq, mem_len, 256)
        h = _ffn(h, w_ff1[l].astype(BF16), w_ff2[l].astype(BF16),
                 vecs(ln_g, 2), vecs(ln_b, 2), 512, 512)
    return h.reshape(batch, seq, d)
```

```python
import functools
import math

import jax
import jax.numpy as jnp
from jax import lax
from jax.experimental import pallas as pl
from jax.experimental.pallas import tpu as pltpu

F32 = jnp.float32
BF16 = jnp.bfloat16

D_MODEL = 4096
RET_HEADS = 8
RET_DIM = 256
RET_WIDTH = RET_HEADS * RET_DIM
RET_CHUNK = 128
NSA_HEADS = 16
NSA_KV_GROUPS = 4
NSA_REP = NSA_HEADS // NSA_KV_GROUPS
NSA_DIM = 128
NSA_WIDTH = NSA_HEADS * NSA_DIM
NSA_KV_WIDTH = NSA_KV_GROUPS * NSA_DIM
CMP_LEN = 32
CMP_STRIDE = 16
CMP_SUB = CMP_LEN // CMP_STRIDE
SLC_LEN = 64
SLC_TOPK = 16
WIN_LEN = 512
MIX_WIDTH = RET_WIDTH + NSA_WIDTH
GATE_WIDTH = 3 * NSA_HEADS
OFF_RQ, OFF_RK, OFF_RV, OFF_RG = 0, RET_WIDTH, 2 * RET_WIDTH, 3 * RET_WIDTH
OFF_NQ = 4 * RET_WIDTH
OFF_KC = OFF_NQ + NSA_WIDTH
OFF_VC = OFF_KC + NSA_KV_WIDTH
OFF_KS = OFF_VC + NSA_KV_WIDTH
OFF_VS = OFF_KS + NSA_KV_WIDTH
OFF_KW = OFF_VS + NSA_KV_WIDTH
OFF_VW = OFF_KW + NSA_KV_WIDTH
OFF_GATES = OFF_VW + NSA_KV_WIDTH
XA_HEADS = 4
XA_DIM = 128
XA_WIDTH = XA_HEADS * XA_DIM
LN_EPS = 1e-5
DEPTH = 1
DN_ALPHA = (2.0 * DEPTH) ** 0.25
NEG_INF = -1e30
MASKED = 2.0 * NEG_INF
FORCE = 1e9

LANES = 128
NSA_TQ = 256
NSA_TK = 256
LN_ROWS = 8
LN_STATS_UNROLL = 16
LN_APPLY_UNROLL = 4
FFN_TN = 1024
VMEM_LIMIT = 56 * 1024 * 1024

_NT = (((1,), (1,)), ((), ()))
_TN = (((0,), (0,)), ((), ()))


def _params(sem):
    return pltpu.CompilerParams(dimension_semantics=sem, vmem_limit_bytes=VMEM_LIMIT)


def _row_stats(tm):
    return [pltpu.VMEM((tm, LANES), F32), pltpu.VMEM((tm, LANES), F32)]


def _layer_norm_rows(o_ref, pre_norm, g_ref, b_ref, mu_ref, rstd_ref):
    tm, d = o_ref.shape
    n = tm // LN_ROWS

    def stats(c, carry):
        rows = pl.ds(pl.multiple_of(c * LN_ROWS, LN_ROWS), LN_ROWS)
        y = pre_norm(rows, slice(None))
        mu = jnp.mean(y, axis=-1, keepdims=True)
        dev = y - mu
        rstd = lax.rsqrt(jnp.mean(dev * dev, axis=-1, keepdims=True) + LN_EPS)
        mu_ref[rows, :] = jnp.broadcast_to(mu, (LN_ROWS, LANES))
        rstd_ref[rows, :] = jnp.broadcast_to(rstd, (LN_ROWS, LANES))
        return carry

    def apply(c, carry):
        rows = pl.ds(pl.multiple_of(c * LN_ROWS, LN_ROWS), LN_ROWS)
        mu, rstd = mu_ref[rows, :], rstd_ref[rows, :]
        for j in range(d // LANES):
            cols = slice(j * LANES, (j + 1) * LANES)
            o_ref[rows, cols] = (pre_norm(rows, cols) - mu) * rstd * g_ref[:, cols] + b_ref[:, cols]
        return carry

    lax.fori_loop(0, n, stats, 0, unroll=min(LN_STATS_UNROLL, n))
    lax.fori_loop(0, n, apply, 0, unroll=LN_APPLY_UNROLL)


def _mm_kernel(x_ref, w_ref, o_ref):
    o_ref[...] = jnp.dot(x_ref[...], w_ref[...], preferred_element_type=F32).astype(o_ref.dtype)


def _matmul(x, w, n_out, tm, tn, out_dtype):
    m, k = x.shape
    return pl.pallas_call(
        _mm_kernel,
        grid=(m // tm, n_out // tn),
        in_specs=[pl.BlockSpec((tm, k), lambda i, j: (i, 0)),
                  pl.BlockSpec((k, tn), lambda i, j: (0, j))],
        out_specs=pl.BlockSpec((tm, tn), lambda i, j: (i, j)),
        out_shape=jax.ShapeDtypeStruct((m, n_out), out_dtype),
        compiler_params=_params(("parallel", "parallel")),
        name="matmul",
    )(x, w)


def _mm_wcast_kernel(x_ref, w_ref, o_ref, wb_ref):
    @pl.when(pl.program_id(1) == 0)
    def _():
        wb_ref[...] = w_ref[...].astype(BF16)

    o_ref[...] = jnp.dot(x_ref[...], wb_ref[...], preferred_element_type=F32).astype(o_ref.dtype)


def _matmul_wcast(x, w, n_out, tm, tn, out_dtype):
    m, k = x.shape
    return pl.pallas_call(
        _mm_wcast_kernel,
        grid=(n_out // tn, m // tm),
        in_specs=[pl.BlockSpec((tm, k), lambda j, i: (i, 0)),
                  pl.BlockSpec((k, tn), lambda j, i: (0, j))],
        out_specs=pl.BlockSpec((tm, tn), lambda j, i: (i, j)),
        out_shape=jax.ShapeDtypeStruct((m, n_out), out_dtype),
        scratch_shapes=[pltpu.VMEM((k, tn), BF16)],
        compiler_params=_params(("parallel", "arbitrary")),
        name="matmul_wcast",
    )(x, w)


def _mm_nt_kernel(w_ref, x_ref, o_ref):
    o_ref[...] = lax.dot_general(w_ref[...], x_ref[...], _NT, preferred_element_type=F32)


def _matmul_nt(w, x, tm):
    n, k = w.shape
    m = x.shape[0]
    return pl.pallas_call(
        _mm_nt_kernel,
        grid=(m // tm,),
        in_specs=[pl.BlockSpec((n, k), lambda i: (0, 0)),
                  pl.BlockSpec((tm, k), lambda i: (i, 0))],
        out_specs=pl.BlockSpec((n, tm), lambda i: (0, i)),
        out_shape=jax.ShapeDtypeStruct((n, m), F32),
        compiler_params=_params(("parallel",)),
        name="matmul_nt",
    )(w, x)


def _retention_kernel(q_ref, k_ref, v_ref, g_ref, o_ref, state_ref):
    c = RET_CHUNK

    @pl.when(pl.program_id(1) == 0)
    def _():
        state_ref[...] = jnp.zeros_like(state_ref)

    diff = (lax.broadcasted_iota(jnp.int32, (c, c), 0)
            - lax.broadcasted_iota(jnp.int32, (c, c), 1)).astype(F32)
    pos = lax.broadcasted_iota(jnp.int32, (c, RET_DIM), 0).astype(F32)
    for h in range(RET_HEADS):
        log_g = math.log1p(-(2.0 ** (-5.0 - h)))
        sl = slice(h * RET_DIM, (h + 1) * RET_DIM)
        q = q_ref[:, sl].astype(F32)
        k = k_ref[:, sl].astype(F32) * (RET_DIM ** -0.5)
        v = v_ref[:, sl]
        decay = jnp.where(diff >= 0, jnp.exp(log_g * jnp.maximum(diff, 0.0)), 0.0)
        scores = lax.dot_general(q.astype(BF16), k.astype(BF16), _NT,
                                 preferred_element_type=F32) * decay
        o_inner = jnp.dot(scores.astype(BF16), v, preferred_element_type=F32)
        k_tail = k * jnp.exp(log_g * (c - 1.0 - pos))
        kv = lax.dot_general(k_tail.astype(BF16), v, _TN, preferred_element_type=F32)
        q_head = q * jnp.exp(log_g * (pos + 1.0))
        state = state_ref[h]
        o = o_inner + jnp.dot(q_head.astype(BF16), state.astype(BF16), preferred_element_type=F32)
        state_ref[h] = math.exp(log_g * c) * state + kv
        mu = jnp.mean(o, axis=-1, keepdims=True)
        d = o - mu
        var = jnp.mean(d * d, axis=-1, keepdims=True)
        o = d * lax.rsqrt(var + LN_EPS)
        gate = g_ref[:, sl].astype(F32)
        o_ref[:, sl] = (o * (gate * jax.nn.sigmoid(gate))).astype(o_ref.dtype)


def _retention(proj, batch, seq):
    n = seq // RET_CHUNK
    blk = lambda col: pl.BlockSpec((RET_CHUNK, RET_WIDTH), lambda b, i: (b * n + i, col))
    return pl.pallas_call(
        _retention_kernel,
        grid=(batch, n),
        in_specs=[blk(OFF_RQ // RET_WIDTH), blk(OFF_RK // RET_WIDTH),
                  blk(OFF_RV // RET_WIDTH), blk(OFF_RG // RET_WIDTH)],
        out_specs=pl.BlockSpec((RET_CHUNK, RET_WIDTH), lambda b, i: (b * n + i, 0)),
        out_shape=jax.ShapeDtypeStruct((batch * seq, RET_WIDTH), BF16),
        scratch_shapes=[pltpu.VMEM((RET_HEADS, RET_DIM, RET_DIM), F32)],
        compiler_params=_params(("parallel", "arbitrary")),
        name="retention",
    )(proj, proj, proj, proj)


def _gelu_tanh(x):
    return 0.5 * x * (1.0 + jnp.tanh(math.sqrt(2.0 / math.pi) * (x + 0.044715 * (x * x * x))))


def _compress_kernel(zk_ref, zv_ref, pos_ref, w1_ref, w2_ref, kc_ref, vc_ref, zf_ref):
    nsub = zf_ref.shape[0] // CMP_STRIDE
    for kv, (z_ref, o_ref) in enumerate(((zk_ref, kc_ref), (zv_ref, vc_ref))):
        zf_ref[...] = z_ref[...].astype(F32)
        hid_a = jnp.zeros((nsub, LANES), F32)
        hid_b = jnp.zeros((nsub, LANES), F32)
        for i in range(CMP_STRIDE):
            rows = zf_ref[pl.ds(i, nsub, stride=CMP_STRIDE), :]
            xa = (rows + pos_ref[kv, i:i + 1, :]).astype(BF16)
            xb = (rows + pos_ref[kv, CMP_STRIDE + i:CMP_STRIDE + i + 1, :]).astype(BF16)
            wa = w1_ref[kv, i * NSA_DIM:(i + 1) * NSA_DIM, :]
            wb = w1_ref[kv, (CMP_STRIDE + i) * NSA_DIM:(CMP_STRIDE + i + 1) * NSA_DIM, :]
            hid_a = hid_a + jnp.dot(xa, wa, preferred_element_type=F32)
            hid_b = hid_b + jnp.dot(xb, wb, preferred_element_type=F32)
        hid = _gelu_tanh(hid_a + pltpu.roll(hid_b, nsub - 1, 0))
        o_ref[...] = jnp.dot(hid.astype(BF16), w2_ref[kv], preferred_element_type=F32).astype(o_ref.dtype)


def _compress(proj, cmp_pos, cmp_w1, cmp_w2, batch, seq):
    nsub = seq // CMP_STRIDE
    zspec = lambda col0: pl.BlockSpec((seq, NSA_DIM), lambda b, g: (b, col0 + g))
    full = lambda a: pl.BlockSpec(a.shape, lambda b, g: (0,) * a.ndim)
    ospec = pl.BlockSpec((None, None, nsub, NSA_DIM), lambda b, g: (b, g, 0, 0))
    oshape = jax.ShapeDtypeStruct((batch, NSA_KV_GROUPS, nsub, NSA_DIM), BF16)
    return pl.pallas_call(
        _compress_kernel,
        grid=(batch, NSA_KV_GROUPS),
        in_specs=[zspec(OFF_KC // NSA_DIM), zspec(OFF_VC // NSA_DIM),
                  full(cmp_pos), full(cmp_w1), full(cmp_w2)],
        out_specs=[ospec, ospec],
        out_shape=[oshape, oshape],
        scratch_shapes=[pltpu.VMEM((seq, NSA_DIM), F32)],
        compiler_params=_params(("parallel", "parallel")),
        name="compress",
    )(proj, proj, cmp_pos, cmp_w1, cmp_w2)


def _nsa_kernel(q_ref, kc_ref, vc_ref, ks_ref, vs_ref, kw_ref, vw_ref, gt_ref, o_ref,
                q4_ref, vct_ref, vst_ref, vwt_ref, sk_ref, bias_ref, sig_ref, m_ref, l_ref, acc_ref, out_ref):
    tq, tk, rep, groups = NSA_TQ, NSA_TK, NSA_REP, NSA_KV_GROUPS
    cols = rep * tq
    n_tiles = ks_ref.shape[0] // tk
    n_blocks = ks_ref.shape[0] // SLC_LEN
    n_cmp = kc_ref.shape[1]
    per_tile = tk // SLC_LEN
    slc, win = 0, 1
    i = pl.program_id(1)
    t0 = i * tq
    gcols = lambda g: slice(g * NSA_DIM, (g + 1) * NSA_DIM)

    @pl.when(i == 0)
    def _():
        for g in range(groups):
            vct_ref[g] = vc_ref[g].astype(F32).T.astype(BF16)

        def transpose_tile(t, carry):
            rows = pl.ds(pl.multiple_of(t * tk, tk), tk)
            for g in range(groups):
                vst_ref[g, t] = vs_ref[rows, gcols(g)].astype(F32).T.astype(BF16)
                vwt_ref[g, t] = vw_ref[rows, gcols(g)].astype(F32).T.astype(BF16)
            return carry

        lax.fori_loop(0, n_tiles, transpose_tile, 0)

    sub = lax.broadcasted_iota(jnp.int32, (tk, cols), 0)
    t_loc = lax.broadcasted_iota(jnp.int32, (tk, cols), 1) & (tq - 1)
    lane_head = lax.broadcasted_iota(jnp.int32, (1, cols), 1) >> (tq.bit_length() - 1)
    causal = sub <= t_loc
    c_end = lax.broadcasted_iota(jnp.int32, (n_cmp, cols), 0) * CMP_STRIDE + (CMP_LEN - 1)
    c_mask = c_end <= t0 + (lax.broadcasted_iota(jnp.int32, (n_cmp, cols), 1) & (tq - 1))
    s_idx = lax.broadcasted_iota(jnp.int32, (n_blocks, n_cmp), 0)
    c_idx = lax.broadcasted_iota(jnp.int32, (n_blocks, n_cmp), 1)
    overlap = jnp.maximum(jnp.minimum(c_idx * CMP_STRIDE + CMP_LEN, s_idx * SLC_LEN + SLC_LEN)
                          - jnp.maximum(c_idx * CMP_STRIDE, s_idx * SLC_LEN), 0)
    overlap = (overlap.astype(F32) * (1.0 / CMP_LEN)).astype(BF16)
    blk = lax.broadcasted_iota(jnp.int32, (n_blocks, tq), 0)
    cur = (t0 + lax.broadcasted_iota(jnp.int32, (n_blocks, tq), 1)) >> (SLC_LEN.bit_length() - 1)
    forced = (blk == 0) | (blk == cur) | (blk == cur - 1)
    future = blk > cur
    tile_start = (lax.broadcasted_iota(jnp.int32, (n_blocks, cols), 0) // per_tile) * tk

    sig_ref[...] = jax.nn.sigmoid(gt_ref[...])

    def gate_row(g, branch):
        first = branch * NSA_HEADS + g * rep
        return jnp.concatenate([sig_ref[first + r:first + r + 1, :] for r in range(rep)], axis=1)

    def slope_row(g):
        return sk_ref[g, 1:2, :]

    for g in range(groups):
        q4 = jnp.concatenate([q_ref[:, (g * rep + r) * NSA_DIM:(g * rep + r + 1) * NSA_DIM]
                              for r in range(rep)], axis=0)
        q4_ref[g] = (q4.astype(F32) * (NSA_DIM ** -0.5)).astype(BF16)
        slope = jnp.exp((g * rep + lane_head + 1).astype(F32) * (-0.5 * math.log(2.0)))
        sk_ref[g] = slope * sub.astype(F32)

        st = lax.dot_general(kc_ref[g], q4_ref[g], _NT, preferred_element_type=F32)
        st = jnp.where(c_mask, st + slope * c_end.astype(F32), MASKED)
        e = jnp.exp(st - jnp.max(st, axis=0, keepdims=True))
        p_cmp = jnp.where(c_mask, e * (1.0 / jnp.sum(e, axis=0, keepdims=True)), 0.0)
        o_cmp = jnp.dot(vct_ref[g], p_cmp.astype(BF16), preferred_element_type=F32)
        out_ref[g] = gate_row(g, 0) * o_cmp

        p_sum = p_cmp[:, 0:tq]
        for r in range(1, rep):
            p_sum = p_sum + p_cmp[:, r * tq:(r + 1) * tq]
        p_hi = p_sum.astype(BF16)
        p_lo = (p_sum - p_hi.astype(F32)).astype(BF16)
        imp = (jnp.dot(overlap, p_hi, preferred_element_type=F32)
               + jnp.dot(overlap, p_lo, preferred_element_type=F32))
        imp = jnp.where(forced, FORCE, jnp.where(future, -FORCE, imp))
        rank = jnp.zeros((n_blocks, tq), F32)
        for j in range(n_blocks):
            row = imp[j:j + 1, :]
            before = (row > imp) | ((row == imp) & (blk > j))
            rank = rank + jnp.where(before, 1.0, 0.0)
        sel_bias = jnp.where(rank < SLC_TOPK, 0.0, MASKED)
        bias_ref[g] = jnp.concatenate([sel_bias] * rep, axis=1) + slope * tile_start.astype(F32)

    def scores(g, k):
        return lax.dot_general(k, q4_ref[g], _NT, preferred_element_type=F32)

    def update(branch, g, st, vt, biases, keep, first):
        st = st + sk_ref[g]
        part = tk // len(biases)
        st = jnp.concatenate([st[n * part:(n + 1) * part] + b for n, b in enumerate(biases)], axis=0)
        if keep is not None:
            st = jnp.where(keep, st, MASKED)
        m_tile = jnp.max(st, axis=0, keepdims=True)
        if first:
            p = jnp.exp(st - m_tile)
            m_ref[branch, g] = m_tile
            l_ref[branch, g] = jnp.sum(p, axis=0, keepdims=True)
            acc_ref[branch, g] = jnp.dot(vt, p.astype(BF16), preferred_element_type=F32)
        else:
            m_old = m_ref[branch, g]
            m_new = jnp.maximum(m_old, m_tile)
            a = jnp.exp(m_old - m_new)
            p = jnp.exp(st - m_new)
            m_ref[branch, g] = m_new
            l_ref[branch, g] = a * l_ref[branch, g] + jnp.sum(p, axis=0, keepdims=True)
            acc_ref[branch, g] = a * acc_ref[branch, g] + jnp.dot(vt, p.astype(BF16), preferred_element_type=F32)

    def slc_chains(kt):
        rows = pl.ds(pl.multiple_of(kt * tk, tk), tk)
        return [(slc, g, ks_ref[rows, gcols(g)], vst_ref[g, kt],
                 [bias_ref[g, pl.ds(per_tile * kt + n, 1), :] for n in range(per_tile)])
                for g in range(groups)]

    def win_chains(kt):
        rows = pl.ds(pl.multiple_of(kt * tk, tk), tk)
        start = (kt * tk).astype(F32)
        return [(win, g, kw_ref[rows, gcols(g)], vwt_ref[g, kt], [slope_row(g) * start])
                for g in range(groups)]

    def run(chains, keep, first):
        sts = [scores(g, k) for (_, g, k, _, _) in chains]
        for st, (branch, g, _, vt, biases) in zip(sts, chains):
            update(branch, g, st, vt, biases, keep, first)

    run(slc_chains(i) + win_chains(i), causal, True)

    def slc_body(kt, carry):
        run(slc_chains(kt), None, False)
        return carry

    lax.fori_loop(0, i, slc_body, 0)

    n_back = WIN_LEN // tk

    @pl.when(i >= n_back)
    def _():
        run(win_chains(i - n_back), sub > t_loc, False)

    def win_body(kt, carry):
        run(win_chains(kt), None, False)
        return carry

    lax.fori_loop(jnp.maximum(i - (n_back - 1), 0), i, win_body, 0)

    for g in range(groups):
        o = out_ref[g]
        for branch, state in ((1, slc), (2, win)):
            o = o + (gate_row(g, branch) * (1.0 / l_ref[state, g])) * acc_ref[state, g]
        for r in range(rep):
            h = g * rep + r
            o_ref[:, h * NSA_DIM:(h + 1) * NSA_DIM] = o[:, r * tq:(r + 1) * tq].T.astype(o_ref.dtype)


def _nsa(proj, gates_t, kc, vc, batch, seq):
    nq = seq // NSA_TQ
    groups, cols = NSA_KV_GROUPS, NSA_REP * NSA_TQ
    n_tiles = seq // NSA_TK
    kvspec = lambda off: pl.BlockSpec((seq, NSA_KV_WIDTH), lambda b, i: (b, off // NSA_KV_WIDTH))
    cspec = lambda a: pl.BlockSpec((None,) + a.shape[1:], lambda b, i: (b, 0, 0, 0))
    return pl.pallas_call(
        _nsa_kernel,
        grid=(batch, nq),
        in_specs=[pl.BlockSpec((NSA_TQ, NSA_WIDTH), lambda b, i: (b * nq + i, OFF_NQ // NSA_WIDTH)),
                  cspec(kc), cspec(vc),
                  kvspec(OFF_KS), kvspec(OFF_VS), kvspec(OFF_KW), kvspec(OFF_VW),
                  pl.BlockSpec((LANES, NSA_TQ), lambda b, i: (0, b * nq + i))],
        out_specs=pl.BlockSpec((NSA_TQ, NSA_WIDTH), lambda b, i: (b * nq + i, 0)),
        out_shape=jax.ShapeDtypeStruct((batch * seq, NSA_WIDTH), BF16),
        scratch_shapes=[pltpu.VMEM((groups, cols, NSA_DIM), BF16),
                        pltpu.VMEM((groups, NSA_DIM, seq // CMP_STRIDE), BF16),
                        pltpu.VMEM((groups, n_tiles, NSA_DIM, NSA_TK), BF16),
                        pltpu.VMEM((groups, n_tiles, NSA_DIM, NSA_TK), BF16),
                        pltpu.VMEM((groups, NSA_TK, cols), F32),
                        pltpu.VMEM((groups, seq // SLC_LEN, cols), F32),
                        pltpu.VMEM((LANES, NSA_TQ), F32),
                        pltpu.VMEM((2, groups, 1, cols), F32),
                        pltpu.VMEM((2, groups, 1, cols), F32),
                        pltpu.VMEM((2, groups, NSA_DIM, cols), F32),
                        pltpu.VMEM((groups, NSA_DIM, cols), F32)],
        compiler_params=_params(("parallel", "arbitrary")),
        name="nsa",
    )(proj, kc, vc, proj, proj, proj, proj, gates_t)


def _outproj_kernel(ret_ref, nsa_ref, wr_ref, wn_ref, x_ref, g_ref, b_ref, o_ref, mu_ref, rstd_ref):
    j = pl.program_id(1)
    nj = pl.num_programs(1)
    tn = x_ref.shape[1]
    y = (jnp.dot(ret_ref[...], wr_ref[...], preferred_element_type=F32)
         + jnp.dot(nsa_ref[...], wn_ref[...], preferred_element_type=F32))
    y = DN_ALPHA * x_ref[...] + y
    for jj in range(o_ref.shape[1] // tn):
        @pl.when(j == jj)
        def _(jj=jj):
            o_ref[:, jj * tn:(jj + 1) * tn] = y

    @pl.when(j == nj - 1)
    def _():
        _layer_norm_rows(o_ref, lambda rows, cols: o_ref[rows, cols], g_ref, b_ref, mu_ref, rstd_ref)


def _outproj(ret, nsa, w_out, x, g, b, tm, tn):
    m, d = x.shape
    vec = pl.BlockSpec((1, d), lambda i, j: (0, 0))
    return pl.pallas_call(
        _outproj_kernel,
        grid=(m // tm, d // tn),
        in_specs=[pl.BlockSpec((tm, RET_WIDTH), lambda i, j: (i, 0)),
                  pl.BlockSpec((tm, NSA_WIDTH), lambda i, j: (i, 0)),
                  pl.BlockSpec((RET_WIDTH, tn), lambda i, j: (0, j)),
                  pl.BlockSpec((NSA_WIDTH, tn), lambda i, j: (RET_WIDTH // NSA_WIDTH, j)),
                  pl.BlockSpec((tm, tn), lambda i, j: (i, j)),
                  vec, vec],
        out_specs=pl.BlockSpec((tm, d), lambda i, j: (i, 0)),
        out_shape=jax.ShapeDtypeStruct((m, d), F32),
        scratch_shapes=_row_stats(tm),
        compiler_params=_params(("parallel", "arbitrary")),
        name="outproj_ln",
    )(ret, nsa, w_out, w_out, x, g, b)


def _xattn_kernel(h_ref, kv_ref, wq_ref, wo_ref, g_ref, b_ref, o_ref, mu_ref, rstd_ref):
    q = jnp.dot(h_ref[...].astype(BF16), wq_ref[...], preferred_element_type=F32).astype(BF16)
    outs = []
    for hd in range(XA_HEADS):
        sl = slice(hd * XA_DIM, (hd + 1) * XA_DIM)
        k = kv_ref[:, sl]
        v = kv_ref[:, XA_WIDTH + hd * XA_DIM:XA_WIDTH + (hd + 1) * XA_DIM]
        s = lax.dot_general(q[:, sl], k, _NT, preferred_element_type=F32) * (XA_DIM ** -0.5)
        e = jnp.exp(s - jnp.max(s, axis=-1, keepdims=True))
        p = e / jnp.sum(e, axis=-1, keepdims=True)
        outs.append(jnp.dot(p.astype(BF16), v, preferred_element_type=F32))
    o = jnp.concatenate(outs, axis=-1).astype(BF16)
    o_ref[...] = jnp.dot(o, wo_ref[...], preferred_element_type=F32)
    _layer_norm_rows(o_ref, lambda rows, cols: DN_ALPHA * h_ref[rows, cols] + o_ref[rows, cols], g_ref, b_ref,
                     mu_ref, rstd_ref)


def _xattn(h, kv, wq, wo, g, b, seq, mem_len, tm):
    m, d = h.shape
    per_batch = seq // tm
    full = lambda a: pl.BlockSpec(a.shape, lambda i: (0, 0))
    return pl.pallas_call(
        _xattn_kernel,
        grid=(m // tm,),
        in_specs=[pl.BlockSpec((tm, d), lambda i: (i, 0)),
                  pl.BlockSpec((mem_len, 2 * XA_WIDTH), lambda i: (i // per_batch, 0)),
                  full(wq), full(wo), full(g), full(b)],
        out_specs=pl.BlockSpec((tm, d), lambda i: (i, 0)),
        out_shape=jax.ShapeDtypeStruct((m, d), F32),
        scratch_shapes=_row_stats(tm),
        compiler_params=_params(("parallel",)),
        name="xattn_ln",
    )(h, kv, wq, wo, g, b)


def _ffn_kernel(h_ref, w1_ref, w2_ref, g_ref, b_ref, o_ref, hb_ref, mu_ref, rstd_ref):
    f = pl.program_id(1)
    nf = pl.num_programs(1)

    @pl.when(f == 0)
    def _():
        hb_ref[...] = h_ref[...].astype(BF16)
        o_ref[...] = jnp.zeros_like(o_ref)

    u = jnp.maximum(jnp.dot(hb_ref[...], w1_ref[...], preferred_element_type=F32), 0.0)
    u = (u * u).astype(BF16)
    for c in range(o_ref.shape[1] // FFN_TN):
        cols = slice(c * FFN_TN, (c + 1) * FFN_TN)
        o_ref[:, cols] += jnp.dot(u, w2_ref[:, cols], preferred_element_type=F32)

    @pl.when(f == nf - 1)
    def _():
        _layer_norm_rows(o_ref, lambda rows, cols: DN_ALPHA * h_ref[rows, cols] + o_ref[rows, cols], g_ref, b_ref,
                     mu_ref, rstd_ref)


def _ffn(h, w1, w2, g, b, tm, tf):
    m, d = h.shape
    d_ff = w1.shape[1]
    vec = pl.BlockSpec((1, d), lambda i, f: (0, 0))
    return pl.pallas_call(
        _ffn_kernel,
        grid=(m // tm, d_ff // tf),
        in_specs=[pl.BlockSpec((tm, d), lambda i, f: (i, 0)),
                  pl.BlockSpec((d, tf), lambda i, f: (0, f)),
                  pl.BlockSpec((tf, d), lambda i, f: (f, 0)),
                  vec, vec],
        out_specs=pl.BlockSpec((tm, d), lambda i, f: (i, 0)),
        out_shape=jax.ShapeDtypeStruct((m, d), F32),
        scratch_shapes=[pltpu.VMEM((tm, d), BF16)] + _row_stats(tm),
        compiler_params=_params(("parallel", "arbitrary")),
        name="ffn_ln",
    )(h, w1, w2, g, b)


def kernel(x, mem, w_in, w_out, cmp_pos, cmp_w1, cmp_w2, xa_wq, xa_wkv, xa_wo,
           w_ff1, w_ff2, ln_g, ln_b):
    batch, seq, d = x.shape
    mem_len = mem.shape[1]
    h = x.reshape(batch * seq, d)
    mem2 = mem.reshape(batch * mem_len, d).astype(BF16)
    for l in range(DEPTH):
        hb = h.astype(BF16)
        w_gate_t = jnp.pad(w_in[l][:, OFF_GATES:].T.astype(BF16), ((0, LANES - GATE_WIDTH), (0, 0)))
        proj = _matmul_wcast(hb, w_in[l], OFF_GATES, 1024, 512, BF16)
        gates_t = _matmul_nt(w_gate_t, hb, 1024)
        ret = _retention(proj, batch, seq)
        kc, vc = _compress(proj, cmp_pos[l], cmp_w1[l].astype(BF16), cmp_w2[l].astype(BF16),
                           batch, seq)
        sparse = _nsa(proj, gates_t, kc, vc, batch, seq)
        vecs = lambda a, k: a[l, k].reshape(1, d)
        h = _outproj(ret, sparse, w_out[l].astype(BF16), h, vecs(ln_g, 0), vecs(ln_b, 0), 512, 1024)
        kv = _matmul(mem2, xa_wkv[l].astype(BF16), 2 * XA_WIDTH, 512, 1024, BF16)
        h = _xattn(h, kv, xa_wq[l].astype(BF16), xa_wo[l].astype(BF16),
                   vecs(ln_g, 1), vecs(ln_b, 1), seq, mem_len, 256)
        h = _ffn(h, w_ff1[l].astype(BF16), w_ff2[l].astype(BF16),
                 vecs(ln_g, 2), vecs(ln_b, 2), 512, 512)
    return h.reshape(batch, seq, d)
```

```python
import functools
import math

import jax
import jax.numpy as jnp
from jax import lax
from jax.experimental import pallas as pl
from jax.experimental.pallas import tpu as pltpu

F32 = jnp.float32
BF16 = jnp.bfloat16

D_MODEL = 4096
RET_HEADS = 8
RET_DIM = 256
RET_WIDTH = RET_HEADS * RET_DIM
RET_CHUNK = 128
NSA_HEADS = 16
NSA_KV_GROUPS = 4
NSA_REP = NSA_HEADS // NSA_KV_GROUPS
NSA_DIM = 128
NSA_WIDTH = NSA_HEADS * NSA_DIM
NSA_KV_WIDTH = NSA_KV_GROUPS * NSA_DIM
CMP_LEN = 32
CMP_STRIDE = 16
CMP_SUB = CMP_LEN // CMP_STRIDE
SLC_LEN = 64
SLC_TOPK = 16
WIN_LEN = 512
MIX_WIDTH = RET_WIDTH + NSA_WIDTH
GATE_WIDTH = 3 * NSA_HEADS
OFF_RQ, OFF_RK, OFF_RV, OFF_RG = 0, RET_WIDTH, 2 * RET_WIDTH, 3 * RET_WIDTH
OFF_NQ = 4 * RET_WIDTH
OFF_KC = OFF_NQ + NSA_WIDTH
OFF_VC = OFF_KC + NSA_KV_WIDTH
OFF_KS = OFF_VC + NSA_KV_WIDTH
OFF_VS = OFF_KS + NSA_KV_WIDTH
OFF_KW = OFF_VS + NSA_KV_WIDTH
OFF_VW = OFF_KW + NSA_KV_WIDTH
OFF_GATES = OFF_VW + NSA_KV_WIDTH
XA_HEADS = 4
XA_DIM = 128
XA_WIDTH = XA_HEADS * XA_DIM
LN_EPS = 1e-5
DEPTH = 1
DN_ALPHA = (2.0 * DEPTH) ** 0.25
NEG_INF = -1e30
MASKED = 2.0 * NEG_INF
FORCE = 1e9

LANES = 128
NSA_TQ = 256
NSA_TK = 256
LN_ROWS = 8
LN_STATS_UNROLL = 16
LN_APPLY_UNROLL = 4
FFN_TN = 1024
VMEM_LIMIT = 56 * 1024 * 1024

_NT = (((1,), (1,)), ((), ()))
_TN = (((0,), (0,)), ((), ()))


def _params(sem):
    return pltpu.CompilerParams(dimension_semantics=sem, vmem_limit_bytes=VMEM_LIMIT)


def _cast_slab_specs(w, layer, steps, step_index):
    rows, cols = w.shape[1] // steps, w.shape[2]
    in_spec = pl.BlockSpec((None, rows, cols), lambda *ids: (layer, step_index(*ids), 0))
    out_spec = pl.BlockSpec((rows, cols), lambda *ids: (step_index(*ids), 0))
    return in_spec, out_spec, jax.ShapeDtypeStruct(w.shape[1:], BF16)


def _row_stats(tm):
    return [pltpu.VMEM((tm, LANES), F32), pltpu.VMEM((tm, LANES), F32)]


def _layer_norm_rows(o_ref, pre_norm, g_ref, b_ref, mu_ref, rstd_ref):
    tm, d = o_ref.shape
    n = tm // LN_ROWS

    def stats(c, carry):
        rows = pl.ds(pl.multiple_of(c * LN_ROWS, LN_ROWS), LN_ROWS)
        y = pre_norm(rows, slice(None))
        mu = jnp.mean(y, axis=-1, keepdims=True)
        dev = y - mu
        rstd = lax.rsqrt(jnp.mean(dev * dev, axis=-1, keepdims=True) + LN_EPS)
        mu_ref[rows, :] = jnp.broadcast_to(mu, (LN_ROWS, LANES))
        rstd_ref[rows, :] = jnp.broadcast_to(rstd, (LN_ROWS, LANES))
        return carry

    def apply(c, carry):
        rows = pl.ds(pl.multiple_of(c * LN_ROWS, LN_ROWS), LN_ROWS)
        mu, rstd = mu_ref[rows, :], rstd_ref[rows, :]
        for j in range(d // LANES):
            cols = slice(j * LANES, (j + 1) * LANES)
            o_ref[rows, cols] = (pre_norm(rows, cols) - mu) * rstd * g_ref[:, cols] + b_ref[:, cols]
        return carry

    lax.fori_loop(0, n, stats, 0, unroll=min(LN_STATS_UNROLL, n))
    lax.fori_loop(0, n, apply, 0, unroll=LN_APPLY_UNROLL)


def _mm_kernel(x_ref, w_ref, o_ref):
    o_ref[...] = jnp.dot(x_ref[...], w_ref[...], preferred_element_type=F32).astype(o_ref.dtype)


def _matmul(x, w, n_out, tm, tn, out_dtype):
    m, k = x.shape
    return pl.pallas_call(
        _mm_kernel,
        grid=(m // tm, n_out // tn),
        in_specs=[pl.BlockSpec((tm, k), lambda i, j: (i, 0)),
                  pl.BlockSpec((k, tn), lambda i, j: (0, j))],
        out_specs=pl.BlockSpec((tm, tn), lambda i, j: (i, j)),
        out_shape=jax.ShapeDtypeStruct((m, n_out), out_dtype),
        compiler_params=_params(("parallel", "parallel")),
        name="matmul",
    )(x, w)


def _retention_kernel(q_ref, k_ref, v_ref, g_ref, o_ref, state_ref):
    c = RET_CHUNK

    @pl.when(pl.program_id(1) == 0)
    def _():
        state_ref[...] = jnp.zeros_like(state_ref)

    diff = (lax.broadcasted_iota(jnp.int32, (c, c), 0)
            - lax.broadcasted_iota(jnp.int32, (c, c), 1)).astype(F32)
    pos = lax.broadcasted_iota(jnp.int32, (c, RET_DIM), 0).astype(F32)
    for h in range(RET_HEADS):
        log_g = math.log1p(-(2.0 ** (-5.0 - h)))
        sl = slice(h * RET_DIM, (h + 1) * RET_DIM)
        q = q_ref[:, sl].astype(F32)
        k = k_ref[:, sl].astype(F32) * (RET_DIM ** -0.5)
        v = v_ref[:, sl]
        decay = jnp.where(diff >= 0, jnp.exp(log_g * jnp.maximum(diff, 0.0)), 0.0)
        scores = lax.dot_general(q.astype(BF16), k.astype(BF16), _NT,
                                 preferred_element_type=F32) * decay
        o_inner = jnp.dot(scores.astype(BF16), v, preferred_element_type=F32)
        k_tail = k * jnp.exp(log_g * (c - 1.0 - pos))
        kv = lax.dot_general(k_tail.astype(BF16), v, _TN, preferred_element_type=F32)
        q_head = q * jnp.exp(log_g * (pos + 1.0))
        state = state_ref[h]
        o = o_inner + jnp.dot(q_head.astype(BF16), state.astype(BF16), preferred_element_type=F32)
        state_ref[h] = math.exp(log_g * c) * state + kv
        mu = jnp.mean(o, axis=-1, keepdims=True)
        d = o - mu
        var = jnp.mean(d * d, axis=-1, keepdims=True)
        o = d * lax.rsqrt(var + LN_EPS)
        gate = g_ref[:, sl].astype(F32)
        o_ref[:, sl] = (o * (gate * jax.nn.sigmoid(gate))).astype(o_ref.dtype)


def _retention(proj, batch, seq):
    n = seq // RET_CHUNK
    blk = lambda col: pl.BlockSpec((RET_CHUNK, RET_WIDTH), lambda b, i: (b * n + i, col))
    return pl.pallas_call(
        _retention_kernel,
        grid=(batch, n),
        in_specs=[blk(OFF_RQ // RET_WIDTH), blk(OFF_RK // RET_WIDTH),
                  blk(OFF_RV // RET_WIDTH), blk(OFF_RG // RET_WIDTH)],
        out_specs=pl.BlockSpec((RET_CHUNK, RET_WIDTH), lambda b, i: (b * n + i, 0)),
        out_shape=jax.ShapeDtypeStruct((batch * seq, RET_WIDTH), BF16),
        scratch_shapes=[pltpu.VMEM((RET_HEADS, RET_DIM, RET_DIM), F32)],
        compiler_params=_params(("parallel", "arbitrary")),
        name="retention",
    )(proj, proj, proj, proj)


def _gelu_tanh(x):
    return 0.5 * x * (1.0 + jnp.tanh(math.sqrt(2.0 / math.pi) * (x + 0.044715 * (x * x * x))))


def _compress_kernel(zk_ref, zv_ref, pos_ref, w1_ref, w2_ref, kc_ref, vc_ref, zf_ref):
    nsub = zf_ref.shape[0] // CMP_STRIDE
    for kv, (z_ref, o_ref) in enumerate(((zk_ref, kc_ref), (zv_ref, vc_ref))):
        zf_ref[...] = z_ref[...].astype(F32)
        hid_a = jnp.zeros((nsub, LANES), F32)
        hid_b = jnp.zeros((nsub, LANES), F32)
        for i in range(CMP_STRIDE):
            rows = zf_ref[pl.ds(i, nsub, stride=CMP_STRIDE), :]
            xa = (rows + pos_ref[kv, i:i + 1, :]).astype(BF16)
            xb = (rows + pos_ref[kv, CMP_STRIDE + i:CMP_STRIDE + i + 1, :]).astype(BF16)
            wa = w1_ref[kv, i * NSA_DIM:(i + 1) * NSA_DIM, :]
            wb = w1_ref[kv, (CMP_STRIDE + i) * NSA_DIM:(CMP_STRIDE + i + 1) * NSA_DIM, :]
            hid_a = hid_a + jnp.dot(xa, wa, preferred_element_type=F32)
            hid_b = hid_b + jnp.dot(xb, wb, preferred_element_type=F32)
        hid = _gelu_tanh(hid_a + pltpu.roll(hid_b, nsub - 1, 0))
        o_ref[...] = jnp.dot(hid.astype(BF16), w2_ref[kv], preferred_element_type=F32).astype(o_ref.dtype)


def _compress(proj, cmp_pos, cmp_w1, cmp_w2, batch, seq):
    nsub = seq // CMP_STRIDE
    zspec = lambda col0: pl.BlockSpec((seq, NSA_DIM), lambda b, g: (b, col0 + g))
    full = lambda a: pl.BlockSpec(a.shape, lambda b, g: (0,) * a.ndim)
    ospec = pl.BlockSpec((None, None, nsub, NSA_DIM), lambda b, g: (b, g, 0, 0))
    oshape = jax.ShapeDtypeStruct((batch, NSA_KV_GROUPS, nsub, NSA_DIM), BF16)
    return pl.pallas_call(
        _compress_kernel,
        grid=(batch, NSA_KV_GROUPS),
        in_specs=[zspec(OFF_KC // NSA_DIM), zspec(OFF_VC // NSA_DIM),
                  full(cmp_pos), full(cmp_w1), full(cmp_w2)],
        out_specs=[ospec, ospec],
        out_shape=[oshape, oshape],
        scratch_shapes=[pltpu.VMEM((seq, NSA_DIM), F32)],
        compiler_params=_params(("parallel", "parallel")),
        name="compress",
    )(proj, proj, cmp_pos, cmp_w1, cmp_w2)


def _nsa_kernel(q_ref, kc_ref, vc_ref, ks_ref, vs_ref, kw_ref, vw_ref, gate_ref, wo_ref, o_ref, wob_ref,
                q4_ref, vct_ref, vst_ref, vwt_ref, sk_ref, bias_ref, sig_ref, m_ref, l_ref, acc_ref, out_ref):
    tq, tk, rep, groups = NSA_TQ, NSA_TK, NSA_REP, NSA_KV_GROUPS
    cols = rep * tq
    n_tiles = ks_ref.shape[0] // tk
    n_blocks = ks_ref.shape[0] // SLC_LEN
    n_cmp = kc_ref.shape[1]
    per_tile = tk // SLC_LEN
    slc, win = 0, 1
    i = pl.program_id(1)
    t0 = i * tq
    gcols = lambda g: slice(g * NSA_DIM, (g + 1) * NSA_DIM)

    wob_ref[...] = wo_ref[...].astype(BF16)

    @pl.when(i == 0)
    def _():
        for g in range(groups):
            vct_ref[g] = vc_ref[g].astype(F32).T.astype(BF16)

        def transpose_tile(t, carry):
            rows = pl.ds(pl.multiple_of(t * tk, tk), tk)
            for g in range(groups):
                vst_ref[g, t] = vs_ref[rows, gcols(g)].astype(F32).T.astype(BF16)
                vwt_ref[g, t] = vw_ref[rows, gcols(g)].astype(F32).T.astype(BF16)
            return carry

        lax.fori_loop(0, n_tiles, transpose_tile, 0)

    sub = lax.broadcasted_iota(jnp.int32, (tk, cols), 0)
    t_loc = lax.broadcasted_iota(jnp.int32, (tk, cols), 1) & (tq - 1)
    lane_head = lax.broadcasted_iota(jnp.int32, (1, cols), 1) >> (tq.bit_length() - 1)
    causal = sub <= t_loc
    c_end = lax.broadcasted_iota(jnp.int32, (n_cmp, cols), 0) * CMP_STRIDE + (CMP_LEN - 1)
    c_mask = c_end <= t0 + (lax.broadcasted_iota(jnp.int32, (n_cmp, cols), 1) & (tq - 1))
    s_idx = lax.broadcasted_iota(jnp.int32, (n_blocks, n_cmp), 0)
    c_idx = lax.broadcasted_iota(jnp.int32, (n_blocks, n_cmp), 1)
    overlap = jnp.maximum(jnp.minimum(c_idx * CMP_STRIDE + CMP_LEN, s_idx * SLC_LEN + SLC_LEN)
                          - jnp.maximum(c_idx * CMP_STRIDE, s_idx * SLC_LEN), 0)
    overlap = (overlap.astype(F32) * (1.0 / CMP_LEN)).astype(BF16)
    blk = lax.broadcasted_iota(jnp.int32, (n_blocks, tq), 0)
    cur = (t0 + lax.broadcasted_iota(jnp.int32, (n_blocks, tq), 1)) >> (SLC_LEN.bit_length() - 1)
    forced = (blk == 0) | (blk == cur) | (blk == cur - 1)
    future = blk > cur
    tile_start = (lax.broadcasted_iota(jnp.int32, (n_blocks, cols), 0) // per_tile) * tk

    sig_ref[...] = jax.nn.sigmoid(gate_ref[...]).T

    def gate_row(g, branch):
        first = branch * NSA_HEADS + g * rep
        return jnp.concatenate([sig_ref[first + r:first + r + 1, :] for r in range(rep)], axis=1)

    def slope_row(g):
        return sk_ref[g, 1:2, :]

    for g in range(groups):
        q4 = jnp.concatenate([q_ref[:, (g * rep + r) * NSA_DIM:(g * rep + r + 1) * NSA_DIM]
                              for r in range(rep)], axis=0)
        q4_ref[g] = (q4.astype(F32) * (NSA_DIM ** -0.5)).astype(BF16)
        slope = jnp.exp((g * rep + lane_head + 1).astype(F32) * (-0.5 * math.log(2.0)))
        sk_ref[g] = slope * sub.astype(F32)

        st = lax.dot_general(kc_ref[g], q4_ref[g], _NT, preferred_element_type=F32)
        st = jnp.where(c_mask, st + slope * c_end.astype(F32), MASKED)
        e = jnp.exp(st - jnp.max(st, axis=0, keepdims=True))
        p_cmp = jnp.where(c_mask, e * (1.0 / jnp.sum(e, axis=0, keepdims=True)), 0.0)
        o_cmp = jnp.dot(vct_ref[g], p_cmp.astype(BF16), preferred_element_type=F32)
        out_ref[g] = gate_row(g, 0) * o_cmp

        p_sum = p_cmp[:, 0:tq]
        for r in range(1, rep):
            p_sum = p_sum + p_cmp[:, r * tq:(r + 1) * tq]
        p_hi = p_sum.astype(BF16)
        p_lo = (p_sum - p_hi.astype(F32)).astype(BF16)
        imp = (jnp.dot(overlap, p_hi, preferred_element_type=F32)
               + jnp.dot(overlap, p_lo, preferred_element_type=F32))
        imp = jnp.where(forced, FORCE, jnp.where(future, -FORCE, imp))
        rank = jnp.zeros((n_blocks, tq), F32)
        for j in range(n_blocks):
            row = imp[j:j + 1, :]
            before = (row > imp) | ((row == imp) & (blk > j))
            rank = rank + jnp.where(before, 1.0, 0.0)
        sel_bias = jnp.where(rank < SLC_TOPK, 0.0, MASKED)
        bias_ref[g] = jnp.concatenate([sel_bias] * rep, axis=1) + slope * tile_start.astype(F32)

    def scores(g, k):
        return lax.dot_general(k, q4_ref[g], _NT, preferred_element_type=F32)

    def update(branch, g, st, vt, biases, keep, first):
        st = st + sk_ref[g]
        part = tk // len(biases)
        st = jnp.concatenate([st[n * part:(n + 1) * part] + b for n, b in enumerate(biases)], axis=0)
        if keep is not None:
            st = jnp.where(keep, st, MASKED)
        m_tile = jnp.max(st, axis=0, keepdims=True)
        if first:
            p = jnp.exp(st - m_tile)
            m_ref[branch, g] = m_tile
            l_ref[branch, g] = jnp.sum(p, axis=0, keepdims=True)
            acc_ref[branch, g] = jnp.dot(vt, p.astype(BF16), preferred_element_type=F32)
        else:
            m_old = m_ref[branch, g]
            m_new = jnp.maximum(m_old, m_tile)
            a = jnp.exp(m_old - m_new)
            p = jnp.exp(st - m_new)
            m_ref[branch, g] = m_new
            l_ref[branch, g] = a * l_ref[branch, g] + jnp.sum(p, axis=0, keepdims=True)
            acc_ref[branch, g] = a * acc_ref[branch, g] + jnp.dot(vt, p.astype(BF16), preferred_element_type=F32)

    def slc_chains(kt):
        rows = pl.ds(pl.multiple_of(kt * tk, tk), tk)
        return [(slc, g, ks_ref[rows, gcols(g)], vst_ref[g, kt],
                 [bias_ref[g, pl.ds(per_tile * kt + n, 1), :] for n in range(per_tile)])
                for g in range(groups)]

    def win_chains(kt):
        rows = pl.ds(pl.multiple_of(kt * tk, tk), tk)
        start = (kt * tk).astype(F32)
        return [(win, g, kw_ref[rows, gcols(g)], vwt_ref[g, kt], [slope_row(g) * start])
                for g in range(groups)]

    def run(chains, keep, first):
        sts = [scores(g, k) for (_, g, k, _, _) in chains]
        for st, (branch, g, _, vt, biases) in zip(sts, chains):
            update(branch, g, st, vt, biases, keep, first)

    run(slc_chains(i) + win_chains(i), causal, True)

    def slc_body(kt, carry):
        run(slc_chains(kt), None, False)
        return carry

    lax.fori_loop(0, i, slc_body, 0)

    n_back = WIN_LEN // tk

    @pl.when(i >= n_back)
    def _():
        run(win_chains(i - n_back), sub > t_loc, False)

    def win_body(kt, carry):
        run(win_chains(kt), None, False)
        return carry

    lax.fori_loop(jnp.maximum(i - (n_back - 1), 0), i, win_body, 0)

    for g in range(groups):
        o = out_ref[g]
        for branch, state in ((1, slc), (2, win)):
            o = o + (gate_row(g, branch) * (1.0 / l_ref[state, g])) * acc_ref[state, g]
        for r in range(rep):
            h = g * rep + r
            o_ref[:, h * NSA_DIM:(h + 1) * NSA_DIM] = o[:, r * tq:(r + 1) * tq].T.astype(o_ref.dtype)


def _nsa(proj, gates, kc, vc, w_out, layer, batch, seq):
    nq = seq // NSA_TQ
    wo_in, wo_out, wo_shape = _cast_slab_specs(w_out, layer, batch * nq, lambda b, i: b * nq + i)
    groups, cols = NSA_KV_GROUPS, NSA_REP * NSA_TQ
    n_tiles = seq // NSA_TK
    kvspec = lambda off: pl.BlockSpec((seq, NSA_KV_WIDTH), lambda b, i: (b, off // NSA_KV_WIDTH))
    cspec = lambda a: pl.BlockSpec((None,) + a.shape[1:], lambda b, i: (b, 0, 0, 0))
    return pl.pallas_call(
        _nsa_kernel,
        grid=(batch, nq),
        in_specs=[pl.BlockSpec((NSA_TQ, NSA_WIDTH), lambda b, i: (b * nq + i, OFF_NQ // NSA_WIDTH)),
                  cspec(kc), cspec(vc),
                  kvspec(OFF_KS), kvspec(OFF_VS), kvspec(OFF_KW), kvspec(OFF_VW),
                  pl.BlockSpec((NSA_TQ, LANES), lambda b, i: (b * nq + i, 0)), wo_in],
        out_specs=[pl.BlockSpec((NSA_TQ, NSA_WIDTH), lambda b, i: (b * nq + i, 0)), wo_out],
        out_shape=[jax.ShapeDtypeStruct((batch * seq, NSA_WIDTH), BF16), wo_shape],
        scratch_shapes=[pltpu.VMEM((groups, cols, NSA_DIM), BF16),
                        pltpu.VMEM((groups, NSA_DIM, seq // CMP_STRIDE), BF16),
                        pltpu.VMEM((groups, n_tiles, NSA_DIM, NSA_TK), BF16),
                        pltpu.VMEM((groups, n_tiles, NSA_DIM, NSA_TK), BF16),
                        pltpu.VMEM((groups, NSA_TK, cols), F32),
                        pltpu.VMEM((groups, seq // SLC_LEN, cols), F32),
                        pltpu.VMEM((LANES, NSA_TQ), F32),
                        pltpu.VMEM((2, groups, 1, cols), F32),
                        pltpu.VMEM((2, groups, 1, cols), F32),
                        pltpu.VMEM((2, groups, NSA_DIM, cols), F32),
                        pltpu.VMEM((groups, NSA_DIM, cols), F32)],
        compiler_params=_params(("parallel", "arbitrary")),
        name="nsa",
    )(proj, kc, vc, proj, proj, proj, proj, gates, w_out)


def _outproj_kernel(ret_ref, nsa_ref, wr_ref, wn_ref, x_ref, g_ref, b_ref, o_ref, mu_ref, rstd_ref):
    j = pl.program_id(1)
    nj = pl.num_programs(1)
    tn = x_ref.shape[1]
    y = (jnp.dot(ret_ref[...], wr_ref[...], preferred_element_type=F32)
         + jnp.dot(nsa_ref[...], wn_ref[...], preferred_element_type=F32))
    y = DN_ALPHA * x_ref[...] + y
    for jj in range(o_ref.shape[1] // tn):
        @pl.when(j == jj)
        def _(jj=jj):
            o_ref[:, jj * tn:(jj + 1) * tn] = y

    @pl.when(j == nj - 1)
    def _():
        _layer_norm_rows(o_ref, lambda rows, cols: o_ref[rows, cols], g_ref, b_ref, mu_ref, rstd_ref)


def _outproj(ret, nsa, w_out, x, g, b, tm, tn):
    m, d = x.shape
    vec = pl.BlockSpec((1, d), lambda i, j: (0, 0))
    return pl.pallas_call(
        _outproj_kernel,
        grid=(m // tm, d // tn),
        in_specs=[pl.BlockSpec((tm, RET_WIDTH), lambda i, j: (i, 0)),
                  pl.BlockSpec((tm, NSA_WIDTH), lambda i, j: (i, 0)),
                  pl.BlockSpec((RET_WIDTH, tn), lambda i, j: (0, j)),
                  pl.BlockSpec((NSA_WIDTH, tn), lambda i, j: (RET_WIDTH // NSA_WIDTH, j)),
                  pl.BlockSpec((tm, tn), lambda i, j: (i, j)),
                  vec, vec],
        out_specs=pl.BlockSpec((tm, d), lambda i, j: (i, 0)),
        out_shape=jax.ShapeDtypeStruct((m, d), F32),
        scratch_shapes=_row_stats(tm),
        compiler_params=_params(("parallel", "arbitrary")),
        name="outproj_ln",
    )(ret, nsa, w_out, w_out, x, g, b)


def _xattn_kernel(h_ref, kv_ref, wq_ref, wo_ref, g_ref, b_ref, w1_ref, w2_ref,
                  o_ref, w1b_ref, w2b_ref, mu_ref, rstd_ref):
    w1b_ref[...] = w1_ref[...].astype(BF16)
    w2b_ref[...] = w2_ref[...].astype(BF16)

    q = jnp.dot(h_ref[...].astype(BF16), wq_ref[...], preferred_element_type=F32).astype(BF16)
    outs = []
    for hd in range(XA_HEADS):
        sl = slice(hd * XA_DIM, (hd + 1) * XA_DIM)
        k = kv_ref[:, sl]
        v = kv_ref[:, XA_WIDTH + hd * XA_DIM:XA_WIDTH + (hd + 1) * XA_DIM]
        s = lax.dot_general(q[:, sl], k, _NT, preferred_element_type=F32) * (XA_DIM ** -0.5)
        e = jnp.exp(s - jnp.max(s, axis=-1, keepdims=True))
        p = e / jnp.sum(e, axis=-1, keepdims=True)
        outs.append(jnp.dot(p.astype(BF16), v, preferred_element_type=F32))
    o = jnp.concatenate(outs, axis=-1).astype(BF16)
    o_ref[...] = jnp.dot(o, wo_ref[...], preferred_element_type=F32)
    _layer_norm_rows(o_ref, lambda rows, cols: DN_ALPHA * h_ref[rows, cols] + o_ref[rows, cols], g_ref, b_ref,
                     mu_ref, rstd_ref)


def _xattn(h, kv, wq, wo, g, b, w_ff1, w_ff2, layer, seq, mem_len, tm):
    m, d = h.shape
    per_batch = seq // tm
    steps = m // tm
    full = lambda a: pl.BlockSpec(a.shape, lambda i: (0, 0))
    w1_in, w1_out, w1_shape = _cast_slab_specs(w_ff1, layer, steps, lambda i: i)
    w2_in, w2_out, w2_shape = _cast_slab_specs(w_ff2, layer, steps, lambda i: i)
    return pl.pallas_call(
        _xattn_kernel,
        grid=(steps,),
        in_specs=[pl.BlockSpec((tm, d), lambda i: (i, 0)),
                  pl.BlockSpec((mem_len, 2 * XA_WIDTH), lambda i: (i // per_batch, 0)),
                  full(wq), full(wo), full(g), full(b), w1_in, w2_in],
        out_specs=[pl.BlockSpec((tm, d), lambda i: (i, 0)), w1_out, w2_out],
        out_shape=[jax.ShapeDtypeStruct((m, d), F32), w1_shape, w2_shape],
        scratch_shapes=_row_stats(tm),
        compiler_params=_params(("parallel",)),
        name="xattn_ln",
    )(h, kv, wq, wo, g, b, w_ff1, w_ff2)


def _ffn_kernel(h_ref, w1_ref, w2_ref, g_ref, b_ref, o_ref, hb_ref, mu_ref, rstd_ref):
    f = pl.program_id(1)
    nf = pl.num_programs(1)

    @pl.when(f == 0)
    def _():
        hb_ref[...] = h_ref[...].astype(BF16)
        o_ref[...] = jnp.zeros_like(o_ref)

    u = jnp.maximum(jnp.dot(hb_ref[...], w1_ref[...], preferred_element_type=F32), 0.0)
    u = (u * u).astype(BF16)
    for c in range(o_ref.shape[1] // FFN_TN):
        cols = slice(c * FFN_TN, (c + 1) * FFN_TN)
        o_ref[:, cols] += jnp.dot(u, w2_ref[:, cols], preferred_element_type=F32)

    @pl.when(f == nf - 1)
    def _():
        _layer_norm_rows(o_ref, lambda rows, cols: DN_ALPHA * h_ref[rows, cols] + o_ref[rows, cols], g_ref, b_ref,
                     mu_ref, rstd_ref)


def _ffn(h, w1, w2, g, b, tm, tf):
    m, d = h.shape
    d_ff = w1.shape[1]
    vec = pl.BlockSpec((1, d), lambda i, f: (0, 0))
    return pl.pallas_call(
        _ffn_kernel,
        grid=(m // tm, d_ff // tf),
        in_specs=[pl.BlockSpec((tm, d), lambda i, f: (i, 0)),
                  pl.BlockSpec((d, tf), lambda i, f: (0, f)),
                  pl.BlockSpec((tf, d), lambda i, f: (f, 0)),
                  vec, vec],
        out_specs=pl.BlockSpec((tm, d), lambda i, f: (i, 0)),
        out_shape=jax.ShapeDtypeStruct((m, d), F32),
        scratch_shapes=[pltpu.VMEM((tm, d), BF16)] + _row_stats(tm),
        compiler_params=_params(("parallel", "arbitrary")),
        name="ffn_ln",
    )(h, w1, w2, g, b)


def kernel(x, mem, w_in, w_out, cmp_pos, cmp_w1, cmp_w2, xa_wq, xa_wkv, xa_wo,
           w_ff1, w_ff2, ln_g, ln_b):
    batch, seq, d = x.shape
    mem_len = mem.shape[1]
    h = x.reshape(batch * seq, d)
    mem2 = mem.reshape(batch * mem_len, d).astype(BF16)
    for l in range(DEPTH):
        hb = h.astype(BF16)
        w_in_b = w_in[l].astype(BF16)
        w_gate = jnp.pad(w_in_b[:, OFF_GATES:], ((0, 0), (0, LANES - GATE_WIDTH)))
        proj = _matmul(hb, w_in_b, OFF_GATES, 1024, 1024, BF16)
        gates = _matmul(hb, w_gate, LANES, 1024, LANES, F32)
        ret = _retention(proj, batch, seq)
        kc, vc = _compress(proj, cmp_pos[l], cmp_w1[l].astype(BF16), cmp_w2[l].astype(BF16),
                           batch, seq)
        sparse, w_out_b = _nsa(proj, gates, kc, vc, w_out, l, batch, seq)
        vecs = lambda a, k: a[l, k].reshape(1, d)
        h = _outproj(ret, sparse, w_out_b, h, vecs(ln_g, 0), vecs(ln_b, 0), 512, 1024)
        kv = _matmul(mem2, xa_wkv[l].astype(BF16), 2 * XA_WIDTH, 512, 1024, BF16)
        h, w_ff1_b, w_ff2_b = _xattn(h, kv, xa_wq[l].astype(BF16), xa_wo[l].astype(BF16),
                                     vecs(ln_g, 1), vecs(ln_b, 1), w_ff1, w_ff2, l, seq, mem_len, 128)
        h = _ffn(h, w_ff1_b, w_ff2_b, vecs(ln_g, 2), vecs(ln_b, 2), 512, 512)
    return h.reshape(batch, seq, d)
```

```python
import functools
import math

import jax
import jax.numpy as jnp
from jax import lax
from jax.experimental import pallas as pl
from jax.experimental.pallas import tpu as pltpu

F32 = jnp.float32
BF16 = jnp.bfloat16

D_MODEL = 4096
RET_HEADS = 8
RET_DIM = 256
RET_WIDTH = RET_HEADS * RET_DIM
RET_CHUNK = 128
NSA_HEADS = 16
NSA_KV_GROUPS = 4
NSA_REP = NSA_HEADS // NSA_KV_GROUPS
NSA_DIM = 128
NSA_WIDTH = NSA_HEADS * NSA_DIM
NSA_KV_WIDTH = NSA_KV_GROUPS * NSA_DIM
CMP_LEN = 32
CMP_STRIDE = 16
CMP_SUB = CMP_LEN // CMP_STRIDE
SLC_LEN = 64
SLC_TOPK = 16
WIN_LEN = 512
MIX_WIDTH = RET_WIDTH + NSA_WIDTH
GATE_WIDTH = 3 * NSA_HEADS
OFF_RQ, OFF_RK, OFF_RV, OFF_RG = 0, RET_WIDTH, 2 * RET_WIDTH, 3 * RET_WIDTH
OFF_NQ = 4 * RET_WIDTH
OFF_KC = OFF_NQ + NSA_WIDTH
OFF_VC = OFF_KC + NSA_KV_WIDTH
OFF_KS = OFF_VC + NSA_KV_WIDTH
OFF_VS = OFF_KS + NSA_KV_WIDTH
OFF_KW = OFF_VS + NSA_KV_WIDTH
OFF_VW = OFF_KW + NSA_KV_WIDTH
OFF_GATES = OFF_VW + NSA_KV_WIDTH
XA_HEADS = 4
XA_DIM = 128
XA_WIDTH = XA_HEADS * XA_DIM
LN_EPS = 1e-5
DEPTH = 1
DN_ALPHA = (2.0 * DEPTH) ** 0.25
NEG_INF = -1e30
MASKED = 2.0 * NEG_INF
FORCE = 1e9
LOG2E = math.log2(math.e)
SLOPE_TERMS = 3

LANES = 128
NSA_TQ = 256
NSA_TK = 256
LN_ROWS = 8
LN_STATS_UNROLL = 16
LN_APPLY_UNROLL = 4
INPROJ_SIDE_STEPS = 8
FFN_TN = 1024
VMEM_LIMIT = 56 * 1024 * 1024

_NT = (((1,), (1,)), ((), ()))
_TN = (((0,), (0,)), ((), ()))


def _params(sem):
    return pltpu.CompilerParams(dimension_semantics=sem, vmem_limit_bytes=VMEM_LIMIT)


def _cast_slab_specs(w, layer, steps, step_index):
    rows, cols = w.shape[1] // steps, w.shape[2]
    in_spec = pl.BlockSpec((None, rows, cols), lambda *ids: (layer, step_index(*ids), 0))
    out_spec = pl.BlockSpec((rows, cols), lambda *ids: (step_index(*ids), 0))
    return in_spec, out_spec, jax.ShapeDtypeStruct(w.shape[1:], BF16)


def _row_stats(tm):
    return [pltpu.VMEM((tm, LANES), F32), pltpu.VMEM((tm, LANES), F32)]


def _layer_norm_rows(o_ref, pre_norm, g_ref, b_ref, mu_ref, rstd_ref):
    tm, d = o_ref.shape
    n = tm // LN_ROWS

    def stats(c, carry):
        rows = pl.ds(pl.multiple_of(c * LN_ROWS, LN_ROWS), LN_ROWS)
        y = pre_norm(rows, slice(None))
        mu = jnp.mean(y, axis=-1, keepdims=True)
        dev = y - mu
        rstd = lax.rsqrt(jnp.mean(dev * dev, axis=-1, keepdims=True) + LN_EPS)
        mu_ref[rows, :] = jnp.broadcast_to(mu, (LN_ROWS, LANES))
        rstd_ref[rows, :] = jnp.broadcast_to(rstd, (LN_ROWS, LANES))
        return carry

    def apply(c, carry):
        rows = pl.ds(pl.multiple_of(c * LN_ROWS, LN_ROWS), LN_ROWS)
        mu, rstd = mu_ref[rows, :], rstd_ref[rows, :]
        for j in range(d // LANES):
            cols = slice(j * LANES, (j + 1) * LANES)
            o_ref[rows, cols] = (pre_norm(rows, cols) - mu) * rstd * g_ref[:, cols] + b_ref[:, cols]
        return carry

    lax.fori_loop(0, n, stats, 0, unroll=min(LN_STATS_UNROLL, n))
    lax.fori_loop(0, n, apply, 0, unroll=LN_APPLY_UNROLL)


def _mm_kernel(x_ref, w_ref, o_ref):
    o_ref[...] = jnp.dot(x_ref[...], w_ref[...], preferred_element_type=F32).astype(o_ref.dtype)


def _matmul(x, w, n_out, tm, tn, out_dtype):
    m, k = x.shape
    return pl.pallas_call(
        _mm_kernel,
        grid=(m // tm, n_out // tn),
        in_specs=[pl.BlockSpec((tm, k), lambda i, j: (i, 0)),
                  pl.BlockSpec((k, tn), lambda i, j: (0, j))],
        out_specs=pl.BlockSpec((tm, tn), lambda i, j: (i, j)),
        out_shape=jax.ShapeDtypeStruct((m, n_out), out_dtype),
        compiler_params=_params(("parallel", "parallel")),
        name="matmul",
    )(x, w)


def _inproj_kernel(x_ref, w_ref, w1_ref, o_ref, w1b_ref):
    @pl.when(pl.program_id(1) < INPROJ_SIDE_STEPS)
    def _():
        w1b_ref[...] = w1_ref[...].astype(BF16)

    o_ref[...] = jnp.dot(x_ref[...], w_ref[...], preferred_element_type=F32).astype(o_ref.dtype)


def _inproj(x, w, n_out, w_ff1, layer, tm, tn):
    m, k = x.shape
    side = INPROJ_SIDE_STEPS
    slab = lambda i, j: i * side + jnp.minimum(j, side - 1)
    w1_in, w1_out, w1_shape = _cast_slab_specs(w_ff1, layer, (m // tm) * side, slab)
    return pl.pallas_call(
        _inproj_kernel,
        grid=(m // tm, n_out // tn),
        in_specs=[pl.BlockSpec((tm, k), lambda i, j: (i, 0)),
                  pl.BlockSpec((k, tn), lambda i, j: (0, j)), w1_in],
        out_specs=[pl.BlockSpec((tm, tn), lambda i, j: (i, j)), w1_out],
        out_shape=[jax.ShapeDtypeStruct((m, n_out), BF16), w1_shape],
        compiler_params=_params(("parallel", "arbitrary")),
        name="inproj",
    )(x, w, w_ff1)


def _retention_kernel(q_ref, k_ref, v_ref, g_ref, o_ref, state_ref):
    c = RET_CHUNK

    @pl.when(pl.program_id(1) == 0)
    def _():
        state_ref[...] = jnp.zeros_like(state_ref)

    diff = (lax.broadcasted_iota(jnp.int32, (c, c), 0)
            - lax.broadcasted_iota(jnp.int32, (c, c), 1)).astype(F32)
    pos = lax.broadcasted_iota(jnp.int32, (c, RET_DIM), 0).astype(F32)
    for h in range(RET_HEADS):
        log_g = math.log1p(-(2.0 ** (-5.0 - h)))
        sl = slice(h * RET_DIM, (h + 1) * RET_DIM)
        q = q_ref[:, sl].astype(F32)
        k = k_ref[:, sl].astype(F32) * (RET_DIM ** -0.5)
        v = v_ref[:, sl]
        decay = jnp.where(diff >= 0, jnp.exp(log_g * jnp.maximum(diff, 0.0)), 0.0)
        scores = lax.dot_general(q.astype(BF16), k.astype(BF16), _NT,
                                 preferred_element_type=F32) * decay
        o_inner = jnp.dot(scores.astype(BF16), v, preferred_element_type=F32)
        k_tail = k * jnp.exp(log_g * (c - 1.0 - pos))
        kv = lax.dot_general(k_tail.astype(BF16), v, _TN, preferred_element_type=F32)
        q_head = q * jnp.exp(log_g * (pos + 1.0))
        state = state_ref[h]
        o = o_inner + jnp.dot(q_head.astype(BF16), state.astype(BF16), preferred_element_type=F32)
        state_ref[h] = math.exp(log_g * c) * state + kv
        mu = jnp.mean(o, axis=-1, keepdims=True)
        d = o - mu
        var = jnp.mean(d * d, axis=-1, keepdims=True)
        o = d * lax.rsqrt(var + LN_EPS)
        gate = g_ref[:, sl].astype(F32)
        o_ref[:, sl] = (o * (gate * jax.nn.sigmoid(gate))).astype(o_ref.dtype)


def _retention(proj, batch, seq):
    n = seq // RET_CHUNK
    blk = lambda col: pl.BlockSpec((RET_CHUNK, RET_WIDTH), lambda b, i: (b * n + i, col))
    return pl.pallas_call(
        _retention_kernel,
        grid=(batch, n),
        in_specs=[blk(OFF_RQ // RET_WIDTH), blk(OFF_RK // RET_WIDTH),
                  blk(OFF_RV // RET_WIDTH), blk(OFF_RG // RET_WIDTH)],
        out_specs=pl.BlockSpec((RET_CHUNK, RET_WIDTH), lambda b, i: (b * n + i, 0)),
        out_shape=jax.ShapeDtypeStruct((batch * seq, RET_WIDTH), BF16),
        scratch_shapes=[pltpu.VMEM((RET_HEADS, RET_DIM, RET_DIM), F32)],
        compiler_params=_params(("parallel", "arbitrary")),
        name="retention",
    )(proj, proj, proj, proj)


def _gelu_tanh(x):
    return 0.5 * x * (1.0 + jnp.tanh(math.sqrt(2.0 / math.pi) * (x + 0.044715 * (x * x * x))))


def _compress_kernel(zk_ref, zv_ref, pos_ref, w1_ref, w2_ref, kc_ref, vc_ref, zf_ref):
    nsub = zf_ref.shape[0] // CMP_STRIDE
    for kv, (z_ref, o_ref) in enumerate(((zk_ref, kc_ref), (zv_ref, vc_ref))):
        zf_ref[...] = z_ref[...].astype(F32)
        hid_a = jnp.zeros((nsub, LANES), F32)
        hid_b = jnp.zeros((nsub, LANES), F32)
        for i in range(CMP_STRIDE):
            rows = zf_ref[pl.ds(i, nsub, stride=CMP_STRIDE), :]
            xa = (rows + pos_ref[kv, i:i + 1, :]).astype(BF16)
            xb = (rows + pos_ref[kv, CMP_STRIDE + i:CMP_STRIDE + i + 1, :]).astype(BF16)
            wa = w1_ref[kv, i * NSA_DIM:(i + 1) * NSA_DIM, :]
            wb = w1_ref[kv, (CMP_STRIDE + i) * NSA_DIM:(CMP_STRIDE + i + 1) * NSA_DIM, :]
            hid_a = hid_a + jnp.dot(xa, wa, preferred_element_type=F32)
            hid_b = hid_b + jnp.dot(xb, wb, preferred_element_type=F32)
        hid = _gelu_tanh(hid_a + pltpu.roll(hid_b, nsub - 1, 0))
        o_ref[...] = jnp.dot(hid.astype(BF16), w2_ref[kv], preferred_element_type=F32).astype(o_ref.dtype)


def _compress(proj, cmp_pos, cmp_w1, cmp_w2, batch, seq):
    nsub = seq // CMP_STRIDE
    zspec = lambda col0: pl.BlockSpec((seq, NSA_DIM), lambda b, g: (b, col0 + g))
    full = lambda a: pl.BlockSpec(a.shape, lambda b, g: (0,) * a.ndim)
    ospec = pl.BlockSpec((None, None, nsub, NSA_DIM), lambda b, g: (b, g, 0, 0))
    oshape = jax.ShapeDtypeStruct((batch, NSA_KV_GROUPS, nsub, NSA_DIM), BF16)
    return pl.pallas_call(
        _compress_kernel,
        grid=(batch, NSA_KV_GROUPS),
        in_specs=[zspec(OFF_KC // NSA_DIM), zspec(OFF_VC // NSA_DIM),
                  full(cmp_pos), full(cmp_w1), full(cmp_w2)],
        out_specs=[ospec, ospec],
        out_shape=[oshape, oshape],
        scratch_shapes=[pltpu.VMEM((seq, NSA_DIM), F32)],
        compiler_params=_params(("parallel", "parallel")),
        name="compress",
    )(proj, proj, cmp_pos, cmp_w1, cmp_w2)


def _nsa_kernel(q_ref, kc_ref, vc_ref, ks_ref, vs_ref, kw_ref, vw_ref, gate_ref, wo_ref, o_ref, wob_ref,
                q4_ref, vct_ref, vst_ref, vwt_ref, slope_ref, bias_ref, sig_ref, m_ref, l_ref, acc_ref, out_ref):
    tq, tk, rep, groups = NSA_TQ, NSA_TK, NSA_REP, NSA_KV_GROUPS
    cols = rep * tq
    n_tiles = ks_ref.shape[0] // tk
    n_blocks = ks_ref.shape[0] // SLC_LEN
    n_cmp = kc_ref.shape[1]
    per_tile = tk // SLC_LEN
    slc, win = 0, 1
    i = pl.program_id(1)
    t0 = i * tq
    gcols = lambda g: slice(g * NSA_DIM, (g + 1) * NSA_DIM)

    wob_ref[...] = wo_ref[...].astype(BF16)

    @pl.when(i == 0)
    def _():
        for g in range(groups):
            vct_ref[g] = vc_ref[g].astype(F32).T.astype(BF16)

        def transpose_tile(t, carry):
            rows = pl.ds(pl.multiple_of(t * tk, tk), tk)
            for g in range(groups):
                vst_ref[g, t] = vs_ref[rows, gcols(g)].astype(F32).T.astype(BF16)
                vwt_ref[g, t] = vw_ref[rows, gcols(g)].astype(F32).T.astype(BF16)
            return carry

        lax.fori_loop(0, n_tiles, transpose_tile, 0)

    sub = lax.broadcasted_iota(jnp.int32, (tk, cols), 0)
    t_loc = lax.broadcasted_iota(jnp.int32, (tk, cols), 1) & (tq - 1)
    lane_head = lax.broadcasted_iota(jnp.int32, (1, cols), 1) >> (tq.bit_length() - 1)
    causal = sub <= t_loc
    c_end = lax.broadcasted_iota(jnp.int32, (n_cmp, cols), 0) * CMP_STRIDE + (CMP_LEN - 1)
    c_mask = c_end <= t0 + (lax.broadcasted_iota(jnp.int32, (n_cmp, cols), 1) & (tq - 1))
    s_idx = lax.broadcasted_iota(jnp.int32, (n_blocks, n_cmp), 0)
    c_idx = lax.broadcasted_iota(jnp.int32, (n_blocks, n_cmp), 1)
    overlap = jnp.maximum(jnp.minimum(c_idx * CMP_STRIDE + CMP_LEN, s_idx * SLC_LEN + SLC_LEN)
                          - jnp.maximum(c_idx * CMP_STRIDE, s_idx * SLC_LEN), 0)
    overlap = (overlap.astype(F32) * (1.0 / CMP_LEN)).astype(BF16)
    blk = lax.broadcasted_iota(jnp.int32, (n_blocks, tq), 0)
    cur = (t0 + lax.broadcasted_iota(jnp.int32, (n_blocks, tq), 1)) >> (SLC_LEN.bit_length() - 1)
    forced = (blk == 0) | (blk == cur) | (blk == cur - 1)
    future = blk > cur
    tile_start = (lax.broadcasted_iota(jnp.int32, (n_blocks, cols), 0) // per_tile) * tk

    sig_ref[...] = jax.nn.sigmoid(gate_ref[...]).T

    def gate_row(g, branch):
        first = branch * NSA_HEADS + g * rep
        return jnp.concatenate([sig_ref[first + r:first + r + 1, :] for r in range(rep)], axis=1)

    pos = lax.broadcasted_iota(jnp.int32, (tk, NSA_DIM), 0)
    pos_lane = lax.broadcasted_iota(jnp.int32, (tk, NSA_DIM), 1)
    pos_cols = jnp.where((pos_lane & 1) == 0, (pos // SLC_LEN) * SLC_LEN, pos % SLC_LEN)
    pos_cols = jnp.where(pos_lane < 2 * SLOPE_TERMS, pos_cols, 0).astype(F32).astype(BF16)
    row_head = lax.broadcasted_iota(jnp.int32, (cols, NSA_DIM), 0) >> (tq.bit_length() - 1)
    row_lane = lax.broadcasted_iota(jnp.int32, (cols, NSA_DIM), 1)

    def alibi_slope(head):
        return LOG2E * jnp.exp((head + 1).astype(F32) * (-0.5 * math.log(2.0)))

    for g in range(groups):
        q4 = jnp.concatenate([q_ref[:, (g * rep + r) * NSA_DIM:(g * rep + r + 1) * NSA_DIM]
                              for r in range(rep)], axis=0)
        q4 = (q4.astype(F32) * (LOG2E * NSA_DIM ** -0.5)).astype(BF16)
        rest = alibi_slope(g * rep + row_head)
        slope_cols = jnp.zeros((cols, NSA_DIM), F32)
        for n in range(SLOPE_TERMS):
            term = rest.astype(BF16).astype(F32)
            rest = rest - term
            slope_cols = jnp.where((row_lane >> 1) == n, term, slope_cols)
        q4_ref[g] = jnp.concatenate([q4, slope_cols.astype(BF16)], axis=1)
        slope = alibi_slope(g * rep + lane_head)
        slope_ref[g] = slope

        st = lax.dot_general(kc_ref[g], q4, _NT, preferred_element_type=F32)
        st = jnp.where(c_mask, st + slope * c_end.astype(F32), MASKED)
        e = jnp.exp2(st - jnp.max(st, axis=0, keepdims=True))
        p_cmp = jnp.where(c_mask, e * (1.0 / jnp.sum(e, axis=0, keepdims=True)), 0.0)
        o_cmp = jnp.dot(vct_ref[g], p_cmp.astype(BF16), preferred_element_type=F32)
        out_ref[g] = gate_row(g, 0) * o_cmp

        p_sum = p_cmp[:, 0:tq]
        for r in range(1, rep):
            p_sum = p_sum + p_cmp[:, r * tq:(r + 1) * tq]
        p_hi = p_sum.astype(BF16)
        p_lo = (p_sum - p_hi.astype(F32)).astype(BF16)
        imp = (jnp.dot(overlap, p_hi, preferred_element_type=F32)
               + jnp.dot(overlap, p_lo, preferred_element_type=F32))
        imp = jnp.where(forced, FORCE, jnp.where(future, -FORCE, imp))
        rank = jnp.zeros((n_blocks, tq), F32)
        for j in range(n_blocks):
            row = imp[j:j + 1, :]
            before = (row > imp) | ((row == imp) & (blk > j))
            rank = rank + jnp.where(before, 1.0, 0.0)
        sel_bias = jnp.where(rank < SLC_TOPK, 0.0, MASKED)
        bias_ref[g] = jnp.concatenate([sel_bias] * rep, axis=1) + slope * tile_start.astype(F32)

    def scores(g, k):
        k = jnp.concatenate([k, pos_cols], axis=1)
        return lax.dot_general(k, q4_ref[g], _NT, preferred_element_type=F32)

    def update(branch, g, st, vt, biases, keep, first):
        part = tk // len(biases)
        st = jnp.concatenate([st[n * part:(n + 1) * part] + b for n, b in enumerate(biases)], axis=0)
        if keep is not None:
            st = jnp.where(keep, st, MASKED)
        m_tile = jnp.max(st, axis=0, keepdims=True)
        if first:
            p = jnp.exp2(st - m_tile)
            m_ref[branch, g] = m_tile
            l_ref[branch, g] = jnp.sum(p, axis=0, keepdims=True)
            acc_ref[branch, g] = jnp.dot(vt, p.astype(BF16), preferred_element_type=F32)
        else:
            m_old = m_ref[branch, g]
            m_new = jnp.maximum(m_old, m_tile)
            a = jnp.exp2(m_old - m_new)
            p = jnp.exp2(st - m_new)
            m_ref[branch, g] = m_new
            l_ref[branch, g] = a * l_ref[branch, g] + jnp.sum(p, axis=0, keepdims=True)
            acc_ref[branch, g] = a * acc_ref[branch, g] + jnp.dot(vt, p.astype(BF16), preferred_element_type=F32)

    def slc_chains(kt):
        rows = pl.ds(pl.multiple_of(kt * tk, tk), tk)
        return [(slc, g, ks_ref[rows, gcols(g)], vst_ref[g, kt],
                 [bias_ref[g, pl.ds(per_tile * kt + n, 1), :] for n in range(per_tile)])
                for g in range(groups)]

    def win_chains(kt):
        rows = pl.ds(pl.multiple_of(kt * tk, tk), tk)
        start = (kt * tk).astype(F32)
        return [(win, g, kw_ref[rows, gcols(g)], vwt_ref[g, kt], [slope_ref[g] * start])
                for g in range(groups)]

    def run(chains, keep, first):
        sts = [scores(g, k) for (_, g, k, _, _) in chains]
        for st, (branch, g, _, vt, biases) in zip(sts, chains):
            update(branch, g, st, vt, biases, keep, first)

    run(slc_chains(i) + win_chains(i), causal, True)

    def slc_body(kt, carry):
        run(slc_chains(kt), None, False)
        return carry

    lax.fori_loop(0, i, slc_body, 0)

    n_back = WIN_LEN // tk

    @pl.when(i >= n_back)
    def _():
        run(win_chains(i - n_back), sub > t_loc, False)

    def win_body(kt, carry):
        run(win_chains(kt), None, False)
        return carry

    lax.fori_loop(jnp.maximum(i - (n_back - 1), 0), i, win_body, 0)

    for g in range(groups):
        o = out_ref[g]
        for branch, state in ((1, slc), (2, win)):
            o = o + (gate_row(g, branch) * (1.0 / l_ref[state, g])) * acc_ref[state, g]
        for r in range(rep):
            h = g * rep + r
            o_ref[:, h * NSA_DIM:(h + 1) * NSA_DIM] = o[:, r * tq:(r + 1) * tq].T.astype(o_ref.dtype)


def _nsa(proj, gates, kc, vc, w_out, layer, batch, seq):
    nq = seq // NSA_TQ
    wo_in, wo_out, wo_shape = _cast_slab_specs(w_out, layer, batch * nq, lambda b, i: b * nq + i)
    groups, cols = NSA_KV_GROUPS, NSA_REP * NSA_TQ
    n_tiles = seq // NSA_TK
    kvspec = lambda off: pl.BlockSpec((seq, NSA_KV_WIDTH), lambda b, i: (b, off // NSA_KV_WIDTH))
    cspec = lambda a: pl.BlockSpec((None,) + a.shape[1:], lambda b, i: (b, 0, 0, 0))
    return pl.pallas_call(
        _nsa_kernel,
        grid=(batch, nq),
        in_specs=[pl.BlockSpec((NSA_TQ, NSA_WIDTH), lambda b, i: (b * nq + i, OFF_NQ // NSA_WIDTH)),
                  cspec(kc), cspec(vc),
                  kvspec(OFF_KS), kvspec(OFF_VS), kvspec(OFF_KW), kvspec(OFF_VW),
                  pl.BlockSpec((NSA_TQ, LANES), lambda b, i: (b * nq + i, 0)), wo_in],
        out_specs=[pl.BlockSpec((NSA_TQ, NSA_WIDTH), lambda b, i: (b * nq + i, 0)), wo_out],
        out_shape=[jax.ShapeDtypeStruct((batch * seq, NSA_WIDTH), BF16), wo_shape],
        scratch_shapes=[pltpu.VMEM((groups, cols, 2 * NSA_DIM), BF16),
                        pltpu.VMEM((groups, NSA_DIM, seq // CMP_STRIDE), BF16),
                        pltpu.VMEM((groups, n_tiles, NSA_DIM, NSA_TK), BF16),
                        pltpu.VMEM((groups, n_tiles, NSA_DIM, NSA_TK), BF16),
                        pltpu.VMEM((groups, 1, cols), F32),
                        pltpu.VMEM((groups, seq // SLC_LEN, cols), F32),
                        pltpu.VMEM((LANES, NSA_TQ), F32),
                        pltpu.VMEM((2, groups, 1, cols), F32),
                        pltpu.VMEM((2, groups, 1, cols), F32),
                        pltpu.VMEM((2, groups, NSA_DIM, cols), F32),
                        pltpu.VMEM((groups, NSA_DIM, cols), F32)],
        compiler_params=_params(("parallel", "arbitrary")),
        name="nsa",
    )(proj, kc, vc, proj, proj, proj, proj, gates, w_out)


def _outproj_kernel(ret_ref, nsa_ref, wr_ref, wn_ref, x_ref, g_ref, b_ref, o_ref, mu_ref, rstd_ref):
    j = pl.program_id(1)
    nj = pl.num_programs(1)
    tn = x_ref.shape[1]
    y = (jnp.dot(ret_ref[...], wr_ref[...], preferred_element_type=F32)
         + jnp.dot(nsa_ref[...], wn_ref[...], preferred_element_type=F32))
    y = DN_ALPHA * x_ref[...] + y
    for jj in range(o_ref.shape[1] // tn):
        @pl.when(j == jj)
        def _(jj=jj):
            o_ref[:, jj * tn:(jj + 1) * tn] = y

    @pl.when(j == nj - 1)
    def _():
        _layer_norm_rows(o_ref, lambda rows, cols: o_ref[rows, cols], g_ref, b_ref, mu_ref, rstd_ref)


def _outproj(ret, nsa, w_out, x, g, b, tm, tn):
    m, d = x.shape
    vec = pl.BlockSpec((1, d), lambda i, j: (0, 0))
    return pl.pallas_call(
        _outproj_kernel,
        grid=(m // tm, d // tn),
        in_specs=[pl.BlockSpec((tm, RET_WIDTH), lambda i, j: (i, 0)),
                  pl.BlockSpec((tm, NSA_WIDTH), lambda i, j: (i, 0)),
                  pl.BlockSpec((RET_WIDTH, tn), lambda i, j: (0, j)),
                  pl.BlockSpec((NSA_WIDTH, tn), lambda i, j: (RET_WIDTH // NSA_WIDTH, j)),
                  pl.BlockSpec((tm, tn), lambda i, j: (i, j)),
                  vec, vec],
        out_specs=pl.BlockSpec((tm, d), lambda i, j: (i, 0)),
        out_shape=jax.ShapeDtypeStruct((m, d), F32),
        scratch_shapes=_row_stats(tm),
        compiler_params=_params(("parallel", "arbitrary")),
        name="outproj_ln",
    )(ret, nsa, w_out, w_out, x, g, b)


def _xattn_kernel(h_ref, kv_ref, wq_ref, wo_ref, g_ref, b_ref, w2_ref, o_ref, w2b_ref, mu_ref, rstd_ref):
    w2b_ref[...] = w2_ref[...].astype(BF16)

    q = jnp.dot(h_ref[...].astype(BF16), wq_ref[...], preferred_element_type=F32).astype(BF16)
    outs = []
    for hd in range(XA_HEADS):
        sl = slice(hd * XA_DIM, (hd + 1) * XA_DIM)
        k = kv_ref[:, sl]
        v = kv_ref[:, XA_WIDTH + hd * XA_DIM:XA_WIDTH + (hd + 1) * XA_DIM]
        s = lax.dot_general(q[:, sl], k, _NT, preferred_element_type=F32) * (XA_DIM ** -0.5)
        e = jnp.exp(s - jnp.max(s, axis=-1, keepdims=True))
        p = e / jnp.sum(e, axis=-1, keepdims=True)
        outs.append(jnp.dot(p.astype(BF16), v, preferred_element_type=F32))
    o = jnp.concatenate(outs, axis=-1).astype(BF16)
    o_ref[...] = jnp.dot(o, wo_ref[...], preferred_element_type=F32)
    _layer_norm_rows(o_ref, lambda rows, cols: DN_ALPHA * h_ref[rows, cols] + o_ref[rows, cols], g_ref, b_ref,
                     mu_ref, rstd_ref)


def _xattn(h, kv, wq, wo, g, b, w_ff2, layer, seq, mem_len, tm):
    m, d = h.shape
    per_batch = seq // tm
    steps = m // tm
    full = lambda a: pl.BlockSpec(a.shape, lambda i: (0, 0))
    w2_in, w2_out, w2_shape = _cast_slab_specs(w_ff2, layer, steps, lambda i: i)
    return pl.pallas_call(
        _xattn_kernel,
        grid=(steps,),
        in_specs=[pl.BlockSpec((tm, d), lambda i: (i, 0)),
                  pl.BlockSpec((mem_len, 2 * XA_WIDTH), lambda i: (i // per_batch, 0)),
                  full(wq), full(wo), full(g), full(b), w2_in],
        out_specs=[pl.BlockSpec((tm, d), lambda i: (i, 0)), w2_out],
        out_shape=[jax.ShapeDtypeStruct((m, d), F32), w2_shape],
        scratch_shapes=_row_stats(tm),
        compiler_params=_params(("parallel",)),
        name="xattn_ln",
    )(h, kv, wq, wo, g, b, w_ff2)


def _ffn_kernel(h_ref, w1_ref, w2_ref, g_ref, b_ref, o_ref, hb_ref, mu_ref, rstd_ref):
    f = pl.program_id(1)
    nf = pl.num_programs(1)

    @pl.when(f == 0)
    def _():
        hb_ref[...] = h_ref[...].astype(BF16)
        o_ref[...] = jnp.zeros_like(o_ref)

    u = jnp.maximum(jnp.dot(hb_ref[...], w1_ref[...], preferred_element_type=F32), 0.0)
    u = (u * u).astype(BF16)
    for c in range(o_ref.shape[1] // FFN_TN):
        cols = slice(c * FFN_TN, (c + 1) * FFN_TN)
        o_ref[:, cols] += jnp.dot(u, w2_ref[:, cols], preferred_element_type=F32)

    @pl.when(f == nf - 1)
    def _():
        _layer_norm_rows(o_ref, lambda rows, cols: DN_ALPHA * h_ref[rows, cols] + o_ref[rows, cols], g_ref, b_ref,
                     mu_ref, rstd_ref)


def _ffn(h, w1, w2, g, b, tm, tf):
    m, d = h.shape
    d_ff = w1.shape[1]
    vec = pl.BlockSpec((1, d), lambda i, f: (0, 0))
    return pl.pallas_call(
        _ffn_kernel,
        grid=(m // tm, d_ff // tf),
        in_specs=[pl.BlockSpec((tm, d), lambda i, f: (i, 0)),
                  pl.BlockSpec((d, tf), lambda i, f: (0, f)),
                  pl.BlockSpec((tf, d), lambda i, f: (f, 0)),
                  vec, vec],
        out_specs=pl.BlockSpec((tm, d), lambda i, f: (i, 0)),
        out_shape=jax.ShapeDtypeStruct((m, d), F32),
        scratch_shapes=[pltpu.VMEM((tm, d), BF16)] + _row_stats(tm),
        compiler_params=_params(("parallel", "arbitrary")),
        name="ffn_ln",
    )(h, w1, w2, g, b)


def kernel(x, mem, w_in, w_out, cmp_pos, cmp_w1, cmp_w2, xa_wq, xa_wkv, xa_wo,
           w_ff1, w_ff2, ln_g, ln_b):
    batch, seq, d = x.shape
    mem_len = mem.shape[1]
    h = x.reshape(batch * seq, d)
    mem2 = mem.reshape(batch * mem_len, d).astype(BF16)
    for l in range(DEPTH):
        hb = h.astype(BF16)
        w_in_b = w_in[l].astype(BF16)
        w_gate = jnp.pad(w_in_b[:, OFF_GATES:], ((0, 0), (0, LANES - GATE_WIDTH)))
        proj, w_ff1_b = _inproj(hb, w_in_b, OFF_GATES, w_ff1, l, 1024, 1024)
        gates = _matmul(hb, w_gate, LANES, 1024, LANES, F32)
        ret = _retention(proj, batch, seq)
        kc, vc = _compress(proj, cmp_pos[l], cmp_w1[l].astype(BF16), cmp_w2[l].astype(BF16),
                           batch, seq)
        sparse, w_out_b = _nsa(proj, gates, kc, vc, w_out, l, batch, seq)
        vecs = lambda a, k: a[l, k].reshape(1, d)
        h = _outproj(ret, sparse, w_out_b, h, vecs(ln_g, 0), vecs(ln_b, 0), 512, 1024)
        kv = _matmul(mem2, xa_wkv[l].astype(BF16), 2 * XA_WIDTH, 512, 1024, BF16)
        h, w_ff2_b = _xattn(h, kv, xa_wq[l].astype(BF16), xa_wo[l].astype(BF16),
                            vecs(ln_g, 1), vecs(ln_b, 1), w_ff2, l, seq, mem_len, 128)
        h = _ffn(h, w_ff1_b, w_ff2_b, vecs(ln_g, 2), vecs(ln_b, 2), 512, 512)
    return h.reshape(batch, seq, d)
```

```python
import functools
import math

import jax
import jax.numpy as jnp
from jax import lax
from jax.experimental import pallas as pl
from jax.experimental.pallas import tpu as pltpu

F32 = jnp.float32
BF16 = jnp.bfloat16

D_MODEL = 4096
RET_HEADS = 8
RET_DIM = 256
RET_WIDTH = RET_HEADS * RET_DIM
RET_CHUNK = 128
NSA_HEADS = 16
NSA_KV_GROUPS = 4
NSA_REP = NSA_HEADS // NSA_KV_GROUPS
NSA_DIM = 128
NSA_WIDTH = NSA_HEADS * NSA_DIM
NSA_KV_WIDTH = NSA_KV_GROUPS * NSA_DIM
CMP_LEN = 32
CMP_STRIDE = 16
CMP_SUB = CMP_LEN // CMP_STRIDE
SLC_LEN = 64
SLC_TOPK = 16
WIN_LEN = 512
MIX_WIDTH = RET_WIDTH + NSA_WIDTH
GATE_WIDTH = 3 * NSA_HEADS
OFF_RQ, OFF_RK, OFF_RV, OFF_RG = 0, RET_WIDTH, 2 * RET_WIDTH, 3 * RET_WIDTH
OFF_NQ = 4 * RET_WIDTH
OFF_KC = OFF_NQ + NSA_WIDTH
OFF_VC = OFF_KC + NSA_KV_WIDTH
OFF_KS = OFF_VC + NSA_KV_WIDTH
OFF_VS = OFF_KS + NSA_KV_WIDTH
OFF_KW = OFF_VS + NSA_KV_WIDTH
OFF_VW = OFF_KW + NSA_KV_WIDTH
OFF_GATES = OFF_VW + NSA_KV_WIDTH
XA_HEADS = 4
XA_DIM = 128
XA_WIDTH = XA_HEADS * XA_DIM
LN_EPS = 1e-5
DEPTH = 1
DN_ALPHA = (2.0 * DEPTH) ** 0.25
NEG_INF = -1e30
MASKED = 2.0 * NEG_INF
FORCE = 1e9
LOG2E = math.log2(math.e)
SLOPE_TERMS = 3

LANES = 128
NSA_TQ = 256
NSA_TK = 256
LN_ROWS = 8
LN_STATS_UNROLL = 16
LN_APPLY_UNROLL = 4
INPROJ_SIDE_STEPS = 8
FFN_TN = 1024
VMEM_LIMIT = 56 * 1024 * 1024

_NT = (((1,), (1,)), ((), ()))
_TN = (((0,), (0,)), ((), ()))


def _params(sem):
    return pltpu.CompilerParams(dimension_semantics=sem, vmem_limit_bytes=VMEM_LIMIT)


def _cast_slab_specs(w, layer, steps, step_index):
    rows, cols = w.shape[1] // steps, w.shape[2]
    in_spec = pl.BlockSpec((None, rows, cols), lambda *ids: (layer, step_index(*ids), 0))
    out_spec = pl.BlockSpec((rows, cols), lambda *ids: (step_index(*ids), 0))
    return in_spec, out_spec, jax.ShapeDtypeStruct(w.shape[1:], BF16)


def _row_stats(tm):
    return [pltpu.VMEM((tm, LANES), F32), pltpu.VMEM((tm, LANES), F32)]


def _layer_norm_rows(o_ref, pre_norm, g_ref, b_ref, mu_ref, rstd_ref):
    tm, d = o_ref.shape
    n = tm // LN_ROWS

    def stats(c, carry):
        rows = pl.ds(pl.multiple_of(c * LN_ROWS, LN_ROWS), LN_ROWS)
        y = pre_norm(rows, slice(None))
        mu = jnp.mean(y, axis=-1, keepdims=True)
        dev = y - mu
        rstd = lax.rsqrt(jnp.mean(dev * dev, axis=-1, keepdims=True) + LN_EPS)
        mu_ref[rows, :] = jnp.broadcast_to(mu, (LN_ROWS, LANES))
        rstd_ref[rows, :] = jnp.broadcast_to(rstd, (LN_ROWS, LANES))
        return carry

    def apply(c, carry):
        rows = pl.ds(pl.multiple_of(c * LN_ROWS, LN_ROWS), LN_ROWS)
        mu, rstd = mu_ref[rows, :], rstd_ref[rows, :]
        for j in range(d // LANES):
            cols = slice(j * LANES, (j + 1) * LANES)
            o_ref[rows, cols] = (pre_norm(rows, cols) - mu) * rstd * g_ref[:, cols] + b_ref[:, cols]
        return carry

    lax.fori_loop(0, n, stats, 0, unroll=min(LN_STATS_UNROLL, n))
    lax.fori_loop(0, n, apply, 0, unroll=LN_APPLY_UNROLL)


def _mm_kernel(x_ref, w_ref, o_ref):
    o_ref[...] = jnp.dot(x_ref[...], w_ref[...], preferred_element_type=F32).astype(o_ref.dtype)


def _matmul(x, w, n_out, tm, tn, out_dtype):
    m, k = x.shape
    return pl.pallas_call(
        _mm_kernel,
        grid=(m // tm, n_out // tn),
        in_specs=[pl.BlockSpec((tm, k), lambda i, j: (i, 0)),
                  pl.BlockSpec((k, tn), lambda i, j: (0, j))],
        out_specs=pl.BlockSpec((tm, tn), lambda i, j: (i, j)),
        out_shape=jax.ShapeDtypeStruct((m, n_out), out_dtype),
        compiler_params=_params(("parallel", "parallel")),
        name="matmul",
    )(x, w)


def _cast_gates_kernel(x_ref, wgt_ref, xb_ref, g_ref):
    xb = x_ref[...].astype(BF16)
    xb_ref[...] = xb
    g_ref[...] = lax.dot_general(xb, wgt_ref[...], _NT, preferred_element_type=F32)


def _cast_gates(x, wg_t, tm):
    m, k = x.shape
    return pl.pallas_call(
        _cast_gates_kernel,
        grid=(m // tm,),
        in_specs=[pl.BlockSpec((tm, k), lambda i: (i, 0)),
                  pl.BlockSpec((LANES, k), lambda i: (0, 0), pipeline_mode=pl.Buffered(1))],
        out_specs=[pl.BlockSpec((tm, k), lambda i: (i, 0)), pl.BlockSpec((tm, LANES), lambda i: (i, 0))],
        out_shape=[jax.ShapeDtypeStruct((m, k), BF16), jax.ShapeDtypeStruct((m, LANES), F32)],
        compiler_params=_params(("parallel",)),
        name="cast_gates",
    )(x, wg_t)


def _inproj_kernel(x_ref, w_ref, w1_ref, o_ref, w1b_ref):
    @pl.when(pl.program_id(1) < INPROJ_SIDE_STEPS)
    def _():
        w1b_ref[...] = w1_ref[...].astype(BF16)

    o_ref[...] = lax.dot_general(x_ref[...], w_ref[...], _NT, preferred_element_type=F32).astype(o_ref.dtype)


def _inproj(x, w_t, n_out, w_ff1, layer, tm, tn):
    m, k = x.shape
    side = INPROJ_SIDE_STEPS
    slab = lambda i, j: i * side + jnp.minimum(j, side - 1)
    w1_in, w1_out, w1_shape = _cast_slab_specs(w_ff1, layer, (m // tm) * side, slab)
    return pl.pallas_call(
        _inproj_kernel,
        grid=(m // tm, n_out // tn),
        in_specs=[pl.BlockSpec((tm, k), lambda i, j: (i, 0)),
                  pl.BlockSpec((tn, k), lambda i, j: (j, 0)), w1_in],
        out_specs=[pl.BlockSpec((tm, tn), lambda i, j: (i, j)), w1_out],
        out_shape=[jax.ShapeDtypeStruct((m, n_out), BF16), w1_shape],
        compiler_params=_params(("parallel", "arbitrary")),
        name="inproj",
    )(x, w_t, w_ff1)


def _retention_kernel(q_ref, k_ref, v_ref, g_ref, o_ref, state_ref, decay_ref, ktail_ref, qhead_ref):
    c = RET_CHUNK

    @pl.when(pl.program_id(1) == 0)
    def _():
        state_ref[...] = jnp.zeros_like(state_ref)
        diff = (lax.broadcasted_iota(jnp.int32, (c, c), 0)
                - lax.broadcasted_iota(jnp.int32, (c, c), 1)).astype(F32)
        pos = lax.broadcasted_iota(jnp.int32, (c, RET_DIM), 0).astype(F32)
        for h in range(RET_HEADS):
            log_g = math.log1p(-(2.0 ** (-5.0 - h)))
            decay_ref[h] = jnp.where(diff >= 0, jnp.exp(log_g * jnp.maximum(diff, 0.0)), 0.0)
            ktail_ref[h] = jnp.exp(log_g * (c - 1.0 - pos))
            qhead_ref[h] = jnp.exp(log_g * (pos + 1.0))

    for h in range(RET_HEADS):
        log_g = math.log1p(-(2.0 ** (-5.0 - h)))
        sl = slice(h * RET_DIM, (h + 1) * RET_DIM)
        q = q_ref[:, sl].astype(F32)
        k = k_ref[:, sl].astype(F32) * (RET_DIM ** -0.5)
        v = v_ref[:, sl]
        scores = lax.dot_general(q.astype(BF16), k.astype(BF16), _NT,
                                 preferred_element_type=F32) * decay_ref[h]
        o_inner = jnp.dot(scores.astype(BF16), v, preferred_element_type=F32)
        k_tail = k * ktail_ref[h]
        kv = lax.dot_general(k_tail.astype(BF16), v, _TN, preferred_element_type=F32)
        q_head = q * qhead_ref[h]
        state = state_ref[h]
        o = o_inner + jnp.dot(q_head.astype(BF16), state.astype(BF16), preferred_element_type=F32)
        state_ref[h] = math.exp(log_g * c) * state + kv
        mu = jnp.mean(o, axis=-1, keepdims=True)
        d = o - mu
        var = jnp.mean(d * d, axis=-1, keepdims=True)
        o = d * lax.rsqrt(var + LN_EPS)
        gate = g_ref[:, sl].astype(F32)
        o_ref[:, sl] = (o * (gate * jax.nn.sigmoid(gate))).astype(o_ref.dtype)


def _retention(proj, batch, seq):
    n = seq // RET_CHUNK
    blk = lambda col: pl.BlockSpec((RET_CHUNK, RET_WIDTH), lambda b, i: (b * n + i, col))
    return pl.pallas_call(
        _retention_kernel,
        grid=(batch, n),
        in_specs=[blk(OFF_RQ // RET_WIDTH), blk(OFF_RK // RET_WIDTH),
                  blk(OFF_RV // RET_WIDTH), blk(OFF_RG // RET_WIDTH)],
        out_specs=pl.BlockSpec((RET_CHUNK, RET_WIDTH), lambda b, i: (b * n + i, 0)),
        out_shape=jax.ShapeDtypeStruct((batch * seq, RET_WIDTH), BF16),
        scratch_shapes=[pltpu.VMEM((RET_HEADS, RET_DIM, RET_DIM), F32),
                        pltpu.VMEM((RET_HEADS, RET_CHUNK, RET_CHUNK), F32),
                        pltpu.VMEM((RET_HEADS, RET_CHUNK, RET_DIM), F32),
                        pltpu.VMEM((RET_HEADS, RET_CHUNK, RET_DIM), F32)],
        compiler_params=_params(("parallel", "arbitrary")),
        name="retention",
    )(proj, proj, proj, proj)


def _gelu_tanh(x):
    return 0.5 * x * (1.0 + jnp.tanh(math.sqrt(2.0 / math.pi) * (x + 0.044715 * (x * x * x))))


def _compress_kernel(zk_ref, zv_ref, pos_ref, w1_ref, w2_ref, kc_ref, vc_ref, zf_ref):
    nsub = zf_ref.shape[0] // CMP_STRIDE
    for kv, (z_ref, o_ref) in enumerate(((zk_ref, kc_ref), (zv_ref, vc_ref))):
        zf_ref[...] = z_ref[...].astype(F32)
        hid_a = jnp.zeros((nsub, LANES), F32)
        hid_b = jnp.zeros((nsub, LANES), F32)
        for i in range(CMP_STRIDE):
            rows = zf_ref[pl.ds(i, nsub, stride=CMP_STRIDE), :]
            xa = (rows + pos_ref[kv, i:i + 1, :]).astype(BF16)
            xb = (rows + pos_ref[kv, CMP_STRIDE + i:CMP_STRIDE + i + 1, :]).astype(BF16)
            wa = w1_ref[kv, i * NSA_DIM:(i + 1) * NSA_DIM, :]
            wb = w1_ref[kv, (CMP_STRIDE + i) * NSA_DIM:(CMP_STRIDE + i + 1) * NSA_DIM, :]
            hid_a = hid_a + jnp.dot(xa, wa, preferred_element_type=F32)
            hid_b = hid_b + jnp.dot(xb, wb, preferred_element_type=F32)
        hid = _gelu_tanh(hid_a + pltpu.roll(hid_b, nsub - 1, 0))
        o_ref[...] = jnp.dot(hid.astype(BF16), w2_ref[kv], preferred_element_type=F32).astype(o_ref.dtype)


def _compress(proj, cmp_pos, cmp_w1, cmp_w2, batch, seq):
    nsub = seq // CMP_STRIDE
    zspec = lambda col0: pl.BlockSpec((seq, NSA_DIM), lambda b, g: (b, col0 + g))
    full = lambda a: pl.BlockSpec(a.shape, lambda b, g: (0,) * a.ndim)
    ospec = pl.BlockSpec((None, None, nsub, NSA_DIM), lambda b, g: (b, g, 0, 0))
    oshape = jax.ShapeDtypeStruct((batch, NSA_KV_GROUPS, nsub, NSA_DIM), BF16)
    return pl.pallas_call(
        _compress_kernel,
        grid=(batch, NSA_KV_GROUPS),
        in_specs=[zspec(OFF_KC // NSA_DIM), zspec(OFF_VC // NSA_DIM),
                  full(cmp_pos), full(cmp_w1), full(cmp_w2)],
        out_specs=[ospec, ospec],
        out_shape=[oshape, oshape],
        scratch_shapes=[pltpu.VMEM((seq, NSA_DIM), F32)],
        compiler_params=_params(("parallel", "parallel")),
        name="compress",
    )(proj, proj, cmp_pos, cmp_w1, cmp_w2)


def _nsa_kernel(q_ref, kc_ref, vc_ref, ks_ref, vs_ref, kw_ref, vw_ref, gate_ref, wo_ref, o_ref, wob_ref,
                q4_ref, vct_ref, vst_ref, vwt_ref, slope_ref, bias_ref, sig_ref, m_ref, l_ref, acc_ref, out_ref):
    tq, tk, rep, groups = NSA_TQ, NSA_TK, NSA_REP, NSA_KV_GROUPS
    cols = rep * tq
    n_tiles = ks_ref.shape[0] // tk
    n_blocks = ks_ref.shape[0] // SLC_LEN
    n_cmp = kc_ref.shape[1]
    per_tile = tk // SLC_LEN
    slc, win = 0, 1
    i = pl.program_id(1)
    t0 = i * tq
    gcols = lambda g: slice(g * NSA_DIM, (g + 1) * NSA_DIM)

    wob_ref[...] = wo_ref[...].astype(BF16)

    @pl.when(i == 0)
    def _():
        for g in range(groups):
            vct_ref[g] = vc_ref[g].astype(F32).T.astype(BF16)

        def transpose_tile(t, carry):
            rows = pl.ds(pl.multiple_of(t * tk, tk), tk)
            for g in range(groups):
                vst_ref[g, t] = vs_ref[rows, gcols(g)].astype(F32).T.astype(BF16)
                vwt_ref[g, t] = vw_ref[rows, gcols(g)].astype(F32).T.astype(BF16)
            return carry

        lax.fori_loop(0, n_tiles, transpose_tile, 0)

    sub = lax.broadcasted_iota(jnp.int32, (tk, cols), 0)
    t_loc = lax.broadcasted_iota(jnp.int32, (tk, cols), 1) & (tq - 1)
    lane_head = lax.broadcasted_iota(jnp.int32, (1, cols), 1) >> (tq.bit_length() - 1)
    causal = sub <= t_loc
    c_end = lax.broadcasted_iota(jnp.int32, (n_cmp, cols), 0) * CMP_STRIDE + (CMP_LEN - 1)
    c_mask = c_end <= t0 + (lax.broadcasted_iota(jnp.int32, (n_cmp, cols), 1) & (tq - 1))
    s_idx = lax.broadcasted_iota(jnp.int32, (n_blocks, n_cmp), 0)
    c_idx = lax.broadcasted_iota(jnp.int32, (n_blocks, n_cmp), 1)
    overlap = jnp.maximum(jnp.minimum(c_idx * CMP_STRIDE + CMP_LEN, s_idx * SLC_LEN + SLC_LEN)
                          - jnp.maximum(c_idx * CMP_STRIDE, s_idx * SLC_LEN), 0)
    overlap = (overlap.astype(F32) * (1.0 / CMP_LEN)).astype(BF16)
    blk = lax.broadcasted_iota(jnp.int32, (n_blocks, tq), 0)
    cur = (t0 + lax.broadcasted_iota(jnp.int32, (n_blocks, tq), 1)) >> (SLC_LEN.bit_length() - 1)
    forced = (blk == 0) | (blk == cur) | (blk == cur - 1)
    future = blk > cur
    tile_start = (lax.broadcasted_iota(jnp.int32, (n_blocks, cols), 0) // per_tile) * tk

    sig_ref[...] = jax.nn.sigmoid(gate_ref[...]).T

    def gate_row(g, branch):
        first = branch * NSA_HEADS + g * rep
        return jnp.concatenate([sig_ref[first + r:first + r + 1, :] for r in range(rep)], axis=1)

    pos = lax.broadcasted_iota(jnp.int32, (tk, NSA_DIM), 0)
    pos_lane = lax.broadcasted_iota(jnp.int32, (tk, NSA_DIM), 1)
    pos_cols = jnp.where((pos_lane & 1) == 0, (pos // SLC_LEN) * SLC_LEN, pos % SLC_LEN)
    pos_cols = jnp.where(pos_lane < 2 * SLOPE_TERMS, pos_cols, 0).astype(F32).astype(BF16)
    row_head = lax.broadcasted_iota(jnp.int32, (cols, NSA_DIM), 0) >> (tq.bit_length() - 1)
    row_lane = lax.broadcasted_iota(jnp.int32, (cols, NSA_DIM), 1)

    def alibi_slope(head):
        return LOG2E * jnp.exp((head + 1).astype(F32) * (-0.5 * math.log(2.0)))

    for g in range(groups):
        q4 = jnp.concatenate([q_ref[:, (g * rep + r) * NSA_DIM:(g * rep + r + 1) * NSA_DIM]
                              for r in range(rep)], axis=0)
        q4 = (q4.astype(F32) * (LOG2E * NSA_DIM ** -0.5)).astype(BF16)
        rest = alibi_slope(g * rep + row_head)
        slope_cols = jnp.zeros((cols, NSA_DIM), F32)
        for n in range(SLOPE_TERMS):
            term = rest.astype(BF16).astype(F32)
            rest = rest - term
            slope_cols = jnp.where((row_lane >> 1) == n, term, slope_cols)
        q4_ref[g] = jnp.concatenate([q4, slope_cols.astype(BF16)], axis=1)
        slope = alibi_slope(g * rep + lane_head)
        slope_ref[g] = slope

        st = lax.dot_general(kc_ref[g], q4, _NT, preferred_element_type=F32)
        st = jnp.where(c_mask, st + slope * c_end.astype(F32), MASKED)
        e = jnp.exp2(st - jnp.max(st, axis=0, keepdims=True))
        p_cmp = jnp.where(c_mask, e * (1.0 / jnp.sum(e, axis=0, keepdims=True)), 0.0)
        o_cmp = jnp.dot(vct_ref[g], p_cmp.astype(BF16), preferred_element_type=F32)
        out_ref[g] = gate_row(g, 0) * o_cmp

        p_sum = p_cmp[:, 0:tq]
        for r in range(1, rep):
            p_sum = p_sum + p_cmp[:, r * tq:(r + 1) * tq]
        p_hi = p_sum.astype(BF16)
        p_lo = (p_sum - p_hi.astype(F32)).astype(BF16)
        imp = (jnp.dot(overlap, p_hi, preferred_element_type=F32)
               + jnp.dot(overlap, p_lo, preferred_element_type=F32))
        imp = jnp.where(forced, FORCE, jnp.where(future, -FORCE, imp))
        rank = jnp.zeros((n_blocks, tq), F32)
        for j in range(n_blocks):
            row = imp[j:j + 1, :]
            before = (row > imp) | ((row == imp) & (blk > j))
            rank = rank + jnp.where(before, 1.0, 0.0)
        sel_bias = jnp.where(rank < SLC_TOPK, 0.0, MASKED)
        bias_ref[g] = jnp.concatenate([sel_bias] * rep, axis=1) + slope * tile_start.astype(F32)

    def scores(g, k):
        k = jnp.concatenate([k, pos_cols], axis=1)
        return lax.dot_general(k, q4_ref[g], _NT, preferred_element_type=F32)

    def update(branch, g, st, vt, biases, keep, first):
        part = tk // len(biases)
        st = jnp.concatenate([st[n * part:(n + 1) * part] + b for n, b in enumerate(biases)], axis=0)
        if keep is not None:
            st = jnp.where(keep, st, MASKED)
        m_tile = jnp.max(st, axis=0, keepdims=True)
        if first:
            p = jnp.exp2(st - m_tile)
            m_ref[branch, g] = m_tile
            l_ref[branch, g] = jnp.sum(p, axis=0, keepdims=True)
            acc_ref[branch, g] = jnp.dot(vt, p.astype(BF16), preferred_element_type=F32)
        else:
            m_old = m_ref[branch, g]
            m_new = jnp.maximum(m_old, m_tile)
            a = jnp.exp2(m_old - m_new)
            p = jnp.exp2(st - m_new)
            m_ref[branch, g] = m_new
            l_ref[branch, g] = a * l_ref[branch, g] + jnp.sum(p, axis=0, keepdims=True)
            acc_ref[branch, g] = a * acc_ref[branch, g] + jnp.dot(vt, p.astype(BF16), preferred_element_type=F32)

    def slc_chains(kt):
        rows = pl.ds(pl.multiple_of(kt * tk, tk), tk)
        return [(slc, g, ks_ref[rows, gcols(g)], vst_ref[g, kt],
                 [bias_ref[g, pl.ds(per_tile * kt + n, 1), :] for n in range(per_tile)])
                for g in range(groups)]

    def win_chains(kt):
        rows = pl.ds(pl.multiple_of(kt * tk, tk), tk)
        start = (kt * tk).astype(F32)
        return [(win, g, kw_ref[rows, gcols(g)], vwt_ref[g, kt], [slope_ref[g] * start])
                for g in range(groups)]

    def run(chains, keep, first):
        sts = [scores(g, k) for (_, g, k, _, _) in chains]
        for st, (branch, g, _, vt, biases) in zip(sts, chains):
            update(branch, g, st, vt, biases, keep, first)

    run(slc_chains(i) + win_chains(i), causal, True)

    def slc_body(kt, carry):
        run(slc_chains(kt), None, False)
        return carry

    lax.fori_loop(0, i, slc_body, 0)

    n_back = WIN_LEN // tk

    @pl.when(i >= n_back)
    def _():
        run(win_chains(i - n_back), sub > t_loc, False)

    def win_body(kt, carry):
        run(win_chains(kt), None, False)
        return carry

    lax.fori_loop(jnp.maximum(i - (n_back - 1), 0), i, win_body, 0)

    for g in range(groups):
        o = out_ref[g]
        for branch, state in ((1, slc), (2, win)):
            o = o + (gate_row(g, branch) * (1.0 / l_ref[state, g])) * acc_ref[state, g]
        for r in range(rep):
            h = g * rep + r
            o_ref[:, h * NSA_DIM:(h + 1) * NSA_DIM] = o[:, r * tq:(r + 1) * tq].T.astype(o_ref.dtype)


def _nsa(proj, gates, kc, vc, w_out, layer, batch, seq):
    nq = seq // NSA_TQ
    wo_in, wo_out, wo_shape = _cast_slab_specs(w_out, layer, batch * nq, lambda b, i: b * nq + i)
    groups, cols = NSA_KV_GROUPS, NSA_REP * NSA_TQ
    n_tiles = seq // NSA_TK
    kvspec = lambda off: pl.BlockSpec((seq, NSA_KV_WIDTH), lambda b, i: (b, off // NSA_KV_WIDTH))
    cspec = lambda a: pl.BlockSpec((None,) + a.shape[1:], lambda b, i: (b, 0, 0, 0))
    return pl.pallas_call(
        _nsa_kernel,
        grid=(batch, nq),
        in_specs=[pl.BlockSpec((NSA_TQ, NSA_WIDTH), lambda b, i: (b * nq + i, OFF_NQ // NSA_WIDTH)),
                  cspec(kc), cspec(vc),
                  kvspec(OFF_KS), kvspec(OFF_VS), kvspec(OFF_KW), kvspec(OFF_VW),
                  pl.BlockSpec((NSA_TQ, LANES), lambda b, i: (b * nq + i, 0)), wo_in],
        out_specs=[pl.BlockSpec((NSA_TQ, NSA_WIDTH), lambda b, i: (b * nq + i, 0)), wo_out],
        out_shape=[jax.ShapeDtypeStruct((batch * seq, NSA_WIDTH), BF16), wo_shape],
        scratch_shapes=[pltpu.VMEM((groups, cols, 2 * NSA_DIM), BF16),
                        pltpu.VMEM((groups, NSA_DIM, seq // CMP_STRIDE), BF16),
                        pltpu.VMEM((groups, n_tiles, NSA_DIM, NSA_TK), BF16),
                        pltpu.VMEM((groups, n_tiles, NSA_DIM, NSA_TK), BF16),
                        pltpu.VMEM((groups, 1, cols), F32),
                        pltpu.VMEM((groups, seq // SLC_LEN, cols), F32),
                        pltpu.VMEM((LANES, NSA_TQ), F32),
                        pltpu.VMEM((2, groups, 1, cols), F32),
                        pltpu.VMEM((2, groups, 1, cols), F32),
                        pltpu.VMEM((2, groups, NSA_DIM, cols), F32),
                        pltpu.VMEM((groups, NSA_DIM, cols), F32)],
        compiler_params=_params(("parallel", "arbitrary")),
        name="nsa",
    )(proj, kc, vc, proj, proj, proj, proj, gates, w_out)


def _outproj_kernel(ret_ref, nsa_ref, wr_ref, wn_ref, x_ref, g_ref, b_ref, o_ref, mu_ref, rstd_ref):
    j = pl.program_id(1)
    nj = pl.num_programs(1)
    tn = x_ref.shape[1]
    y = (jnp.dot(ret_ref[...], wr_ref[...], preferred_element_type=F32)
         + jnp.dot(nsa_ref[...], wn_ref[...], preferred_element_type=F32))
    y = DN_ALPHA * x_ref[...] + y
    for jj in range(o_ref.shape[1] // tn):
        @pl.when(j == jj)
        def _(jj=jj):
            o_ref[:, jj * tn:(jj + 1) * tn] = y

    @pl.when(j == nj - 1)
    def _():
        _layer_norm_rows(o_ref, lambda rows, cols: o_ref[rows, cols], g_ref, b_ref, mu_ref, rstd_ref)


def _outproj(ret, nsa, w_out, x, g, b, tm, tn):
    m, d = x.shape
    vec = pl.BlockSpec((1, d), lambda i, j: (0, 0))
    return pl.pallas_call(
        _outproj_kernel,
        grid=(m // tm, d // tn),
        in_specs=[pl.BlockSpec((tm, RET_WIDTH), lambda i, j: (i, 0)),
                  pl.BlockSpec((tm, NSA_WIDTH), lambda i, j: (i, 0)),
                  pl.BlockSpec((RET_WIDTH, tn), lambda i, j: (0, j)),
                  pl.BlockSpec((NSA_WIDTH, tn), lambda i, j: (RET_WIDTH // NSA_WIDTH, j)),
                  pl.BlockSpec((tm, tn), lambda i, j: (i, j)),
                  vec, vec],
        out_specs=pl.BlockSpec((tm, d), lambda i, j: (i, 0)),
        out_shape=jax.ShapeDtypeStruct((m, d), F32),
        scratch_shapes=_row_stats(tm),
        compiler_params=_params(("parallel", "arbitrary")),
        name="outproj_ln",
    )(ret, nsa, w_out, w_out, x, g, b)


def _xattn_kernel(h_ref, kv_ref, wq_ref, wo_ref, g_ref, b_ref, w2_ref, o_ref, w2b_ref, mu_ref, rstd_ref):
    w2b_ref[...] = w2_ref[...].astype(BF16)

    q = jnp.dot(h_ref[...].astype(BF16), wq_ref[...], preferred_element_type=F32).astype(BF16)
    outs = []
    for hd in range(XA_HEADS):
        sl = slice(hd * XA_DIM, (hd + 1) * XA_DIM)
        k = kv_ref[:, sl]
        v = kv_ref[:, XA_WIDTH + hd * XA_DIM:XA_WIDTH + (hd + 1) * XA_DIM]
        s = lax.dot_general(q[:, sl], k, _NT, preferred_element_type=F32) * (XA_DIM ** -0.5)
        e = jnp.exp(s - jnp.max(s, axis=-1, keepdims=True))
        p = e / jnp.sum(e, axis=-1, keepdims=True)
        outs.append(jnp.dot(p.astype(BF16), v, preferred_element_type=F32))
    o = jnp.concatenate(outs, axis=-1).astype(BF16)
    o_ref[...] = jnp.dot(o, wo_ref[...], preferred_element_type=F32)
    _layer_norm_rows(o_ref, lambda rows, cols: DN_ALPHA * h_ref[rows, cols] + o_ref[rows, cols], g_ref, b_ref,
                     mu_ref, rstd_ref)


def _xattn(h, kv, wq, wo, g, b, w_ff2, layer, seq, mem_len, tm):
    m, d = h.shape
    per_batch = seq // tm
    steps = m // tm
    full = lambda a: pl.BlockSpec(a.shape, lambda i: (0, 0), pipeline_mode=pl.Buffered(1))
    w2_in, w2_out, w2_shape = _cast_slab_specs(w_ff2, layer, steps, lambda i: i)
    return pl.pallas_call(
        _xattn_kernel,
        grid=(steps,),
        in_specs=[pl.BlockSpec((tm, d), lambda i: (i, 0)),
                  pl.BlockSpec((mem_len, 2 * XA_WIDTH), lambda i: (i // per_batch, 0)),
                  full(wq), full(wo), full(g), full(b), w2_in],
        out_specs=[pl.BlockSpec((tm, d), lambda i: (i, 0)), w2_out],
        out_shape=[jax.ShapeDtypeStruct((m, d), F32), w2_shape],
        scratch_shapes=_row_stats(tm),
        compiler_params=_params(("parallel",)),
        name="xattn_ln",
    )(h, kv, wq, wo, g, b, w_ff2)


def _ffn_kernel(h_ref, w1_ref, w2_ref, g_ref, b_ref, o_ref, hb_ref, mu_ref, rstd_ref):
    f = pl.program_id(1)
    nf = pl.num_programs(1)

    @pl.when(f == 0)
    def _():
        hb_ref[...] = h_ref[...].astype(BF16)
        o_ref[...] = jnp.zeros_like(o_ref)

    u = jnp.maximum(jnp.dot(hb_ref[...], w1_ref[...], preferred_element_type=F32), 0.0)
    u = (u * u).astype(BF16)
    for c in range(o_ref.shape[1] // FFN_TN):
        cols = slice(c * FFN_TN, (c + 1) * FFN_TN)
        o_ref[:, cols] += jnp.dot(u, w2_ref[:, cols], preferred_element_type=F32)

    @pl.when(f == nf - 1)
    def _():
        _layer_norm_rows(o_ref, lambda rows, cols: DN_ALPHA * h_ref[rows, cols] + o_ref[rows, cols], g_ref, b_ref,
                     mu_ref, rstd_ref)


def _ffn(h, w1, w2, g, b, tm, tf):
    m, d = h.shape
    d_ff = w1.shape[1]
    vec = pl.BlockSpec((1, d), lambda i, f: (0, 0))
    return pl.pallas_call(
        _ffn_kernel,
        grid=(m // tm, d_ff // tf),
        in_specs=[pl.BlockSpec((tm, d), lambda i, f: (i, 0)),
                  pl.BlockSpec((d, tf), lambda i, f: (0, f)),
                  pl.BlockSpec((tf, d), lambda i, f: (f, 0)),
                  vec, vec],
        out_specs=pl.BlockSpec((tm, d), lambda i, f: (i, 0)),
        out_shape=jax.ShapeDtypeStruct((m, d), F32),
        scratch_shapes=[pltpu.VMEM((tm, d), BF16)] + _row_stats(tm),
        compiler_params=_params(("parallel", "arbitrary")),
        name="ffn_ln",
    )(h, w1, w2, g, b)


def kernel(x, mem, w_in, w_out, cmp_pos, cmp_w1, cmp_w2, xa_wq, xa_wkv, xa_wo,
           w_ff1, w_ff2, ln_g, ln_b):
    batch, seq, d = x.shape
    mem_len = mem.shape[1]
    h = x.reshape(batch * seq, d)
    mem2 = mem.reshape(batch * mem_len, d).astype(BF16)
    for l in range(DEPTH):
        w_in_t = jnp.swapaxes(w_in[l], 0, 1).astype(BF16)
        wg_t = jnp.pad(w_in_t[OFF_GATES:], ((0, LANES - GATE_WIDTH), (0, 0)))
        hb, gates = _cast_gates(h, wg_t, 512)
        proj, w_ff1_b = _inproj(hb, w_in_t, OFF_GATES, w_ff1, l, 1024, 1024)
        ret = _retention(proj, batch, seq)
        kc, vc = _compress(proj, cmp_pos[l], cmp_w1[l].astype(BF16), cmp_w2[l].astype(BF16),
                           batch, seq)
        sparse, w_out_b = _nsa(proj, gates, kc, vc, w_out, l, batch, seq)
        vecs = lambda a, k: a[l, k].reshape(1, d)
        h = _outproj(ret, sparse, w_out_b, h, vecs(ln_g, 0), vecs(ln_b, 0), 512, 1024)
        kv = _matmul(mem2, xa_wkv[l].astype(BF16), 2 * XA_WIDTH, 512, 1024, BF16)
        h, w_ff2_b = _xattn(h, kv, xa_wq[l].astype(BF16), xa_wo[l].astype(BF16),
                            vecs(ln_g, 1), vecs(ln_b, 1), w_ff2, l, seq, mem_len, 256)
        h = _ffn(h, w_ff1_b, w_ff2_b, vecs(ln_g, 2), vecs(ln_b, 2), 512, 512)
    return h.reshape(batch, seq, d)
```

```python
import functools
import math

import jax
import jax.numpy as jnp
from jax import lax
from jax.experimental import pallas as pl
from jax.experimental.pallas import tpu as pltpu

F32 = jnp.float32
BF16 = jnp.bfloat16

D_MODEL = 4096
RET_HEADS = 8
RET_DIM = 256
RET_WIDTH = RET_HEADS * RET_DIM
RET_CHUNK = 128
NSA_HEADS = 16
NSA_KV_GROUPS = 4
NSA_REP = NSA_HEADS // NSA_KV_GROUPS
NSA_DIM = 128
NSA_WIDTH = NSA_HEADS * NSA_DIM
NSA_KV_WIDTH = NSA_KV_GROUPS * NSA_DIM
CMP_LEN = 32
CMP_STRIDE = 16
CMP_SUB = CMP_LEN // CMP_STRIDE
SLC_LEN = 64
SLC_TOPK = 16
WIN_LEN = 512
MIX_WIDTH = RET_WIDTH + NSA_WIDTH
GATE_WIDTH = 3 * NSA_HEADS
OFF_RQ, OFF_RK, OFF_RV, OFF_RG = 0, RET_WIDTH, 2 * RET_WIDTH, 3 * RET_WIDTH
OFF_NQ = 4 * RET_WIDTH
OFF_KC = OFF_NQ + NSA_WIDTH
OFF_VC = OFF_KC + NSA_KV_WIDTH
OFF_KS = OFF_VC + NSA_KV_WIDTH
OFF_VS = OFF_KS + NSA_KV_WIDTH
OFF_KW = OFF_VS + NSA_KV_WIDTH
OFF_VW = OFF_KW + NSA_KV_WIDTH
OFF_GATES = OFF_VW + NSA_KV_WIDTH
XA_HEADS = 4
XA_DIM = 128
XA_WIDTH = XA_HEADS * XA_DIM
LN_EPS = 1e-5
DEPTH = 1
DN_ALPHA = (2.0 * DEPTH) ** 0.25
NEG_INF = -1e30
MASKED = 2.0 * NEG_INF
FORCE = 1e9
LOG2E = math.log2(math.e)
SLOPE_TERMS = 3

LANES = 128
NSA_TQ = 256
NSA_TK = 256
LN_ROWS = 8
LN_STATS_UNROLL = 16
LN_APPLY_UNROLL = 4
INPROJ_SIDE_STEPS = 8
FFN_TN = 1024
FFN_W2_SPLIT = 1024
VMEM_LIMIT = 56 * 1024 * 1024

_NT = (((1,), (1,)), ((), ()))
_TN = (((0,), (0,)), ((), ()))


def _params(sem):
    return pltpu.CompilerParams(dimension_semantics=sem, vmem_limit_bytes=VMEM_LIMIT)


def _cast_slab_specs(w, layer, steps, step_index, cols=None, col_block=0):
    rows, cols = w.shape[1] // steps, cols or w.shape[2]
    in_spec = pl.BlockSpec((None, rows, cols), lambda *ids: (layer, step_index(*ids), col_block))
    out_spec = pl.BlockSpec((rows, cols), lambda *ids: (step_index(*ids), 0))
    return in_spec, out_spec, jax.ShapeDtypeStruct((w.shape[1], cols), BF16)


def _row_stats(tm):
    return [pltpu.VMEM((tm, LANES), F32), pltpu.VMEM((tm, LANES), F32)]


def _layer_norm_rows(o_ref, pre_norm, g_ref, b_ref, mu_ref, rstd_ref):
    tm, d = o_ref.shape
    n = tm // LN_ROWS

    def stats(c, carry):
        rows = pl.ds(pl.multiple_of(c * LN_ROWS, LN_ROWS), LN_ROWS)
        y = pre_norm(rows, slice(None))
        mu = jnp.mean(y, axis=-1, keepdims=True)
        dev = y - mu
        rstd = lax.rsqrt(jnp.mean(dev * dev, axis=-1, keepdims=True) + LN_EPS)
        mu_ref[rows, :] = jnp.broadcast_to(mu, (LN_ROWS, LANES))
        rstd_ref[rows, :] = jnp.broadcast_to(rstd, (LN_ROWS, LANES))
        return carry

    def apply(c, carry):
        rows = pl.ds(pl.multiple_of(c * LN_ROWS, LN_ROWS), LN_ROWS)
        mu, rstd = mu_ref[rows, :], rstd_ref[rows, :]
        for j in range(d // LANES):
            cols = slice(j * LANES, (j + 1) * LANES)
            o_ref[rows, cols] = (pre_norm(rows, cols) - mu) * rstd * g_ref[:, cols] + b_ref[:, cols]
        return carry

    lax.fori_loop(0, n, stats, 0, unroll=min(LN_STATS_UNROLL, n))
    lax.fori_loop(0, n, apply, 0, unroll=LN_APPLY_UNROLL)


def _mm_kernel(x_ref, w_ref, o_ref):
    o_ref[...] = jnp.dot(x_ref[...], w_ref[...], preferred_element_type=F32).astype(o_ref.dtype)


def _matmul(x, w, n_out, tm, tn, out_dtype):
    m, k = x.shape
    return pl.pallas_call(
        _mm_kernel,
        grid=(m // tm, n_out // tn),
        in_specs=[pl.BlockSpec((tm, k), lambda i, j: (i, 0)),
                  pl.BlockSpec((k, tn), lambda i, j: (0, j))],
        out_specs=pl.BlockSpec((tm, tn), lambda i, j: (i, j)),
        out_shape=jax.ShapeDtypeStruct((m, n_out), out_dtype),
        compiler_params=_params(("parallel", "parallel")),
        name="matmul",
    )(x, w)


def _cast_gates_kernel(x_ref, wgt_ref, xb_ref, g_ref):
    xb = x_ref[...].astype(BF16)
    xb_ref[...] = xb
    g_ref[...] = lax.dot_general(xb, wgt_ref[...], _NT, preferred_element_type=F32)


def _cast_gates(x, wg_t, tm):
    m, k = x.shape
    return pl.pallas_call(
        _cast_gates_kernel,
        grid=(m // tm,),
        in_specs=[pl.BlockSpec((tm, k), lambda i: (i, 0)),
                  pl.BlockSpec((LANES, k), lambda i: (0, 0), pipeline_mode=pl.Buffered(1))],
        out_specs=[pl.BlockSpec((tm, k), lambda i: (i, 0)), pl.BlockSpec((tm, LANES), lambda i: (i, 0))],
        out_shape=[jax.ShapeDtypeStruct((m, k), BF16), jax.ShapeDtypeStruct((m, LANES), F32)],
        compiler_params=_params(("parallel",)),
        name="cast_gates",
    )(x, wg_t)


def _inproj_kernel(x_ref, w_ref, w1_ref, o_ref, w1b_ref):
    @pl.when(pl.program_id(1) < INPROJ_SIDE_STEPS)
    def _():
        w1b_ref[...] = w1_ref[...].astype(BF16)

    o_ref[...] = lax.dot_general(x_ref[...], w_ref[...], _NT, preferred_element_type=F32).astype(o_ref.dtype)


def _inproj(x, w_t, n_out, w_ff1, layer, tm, tn):
    m, k = x.shape
    side = INPROJ_SIDE_STEPS
    slab = lambda i, j: i * side + jnp.minimum(j, side - 1)
    w1_in, w1_out, w1_shape = _cast_slab_specs(w_ff1, layer, (m // tm) * side, slab)
    return pl.pallas_call(
        _inproj_kernel,
        grid=(m // tm, n_out // tn),
        in_specs=[pl.BlockSpec((tm, k), lambda i, j: (i, 0)),
                  pl.BlockSpec((tn, k), lambda i, j: (j, 0)), w1_in],
        out_specs=[pl.BlockSpec((tm, tn), lambda i, j: (i, j)), w1_out],
        out_shape=[jax.ShapeDtypeStruct((m, n_out), BF16), w1_shape],
        compiler_params=_params(("parallel", "arbitrary")),
        name="inproj",
    )(x, w_t, w_ff1)


def _retention_kernel(q_ref, k_ref, v_ref, g_ref, o_ref, state_ref, decay_ref, ktail_ref, qhead_ref):
    c = RET_CHUNK

    @pl.when(pl.program_id(1) == 0)
    def _():
        state_ref[...] = jnp.zeros_like(state_ref)
        diff = (lax.broadcasted_iota(jnp.int32, (c, c), 0)
                - lax.broadcasted_iota(jnp.int32, (c, c), 1)).astype(F32)
        pos = lax.broadcasted_iota(jnp.int32, (c, RET_DIM), 0).astype(F32)
        for h in range(RET_HEADS):
            log_g = math.log1p(-(2.0 ** (-5.0 - h)))
            decay_ref[h] = jnp.where(diff >= 0, jnp.exp(log_g * jnp.maximum(diff, 0.0)), 0.0)
            ktail_ref[h] = jnp.exp(log_g * (c - 1.0 - pos))
            qhead_ref[h] = jnp.exp(log_g * (pos + 1.0))

    for h in range(RET_HEADS):
        log_g = math.log1p(-(2.0 ** (-5.0 - h)))
        sl = slice(h * RET_DIM, (h + 1) * RET_DIM)
        q = q_ref[:, sl].astype(F32)
        k = k_ref[:, sl].astype(F32) * (RET_DIM ** -0.5)
        v = v_ref[:, sl]
        scores = lax.dot_general(q.astype(BF16), k.astype(BF16), _NT,
                                 preferred_element_type=F32) * decay_ref[h]
        o_inner = jnp.dot(scores.astype(BF16), v, preferred_element_type=F32)
        k_tail = k * ktail_ref[h]
        kv = lax.dot_general(k_tail.astype(BF16), v, _TN, preferred_element_type=F32)
        q_head = q * qhead_ref[h]
        state = state_ref[h]
        o = o_inner + jnp.dot(q_head.astype(BF16), state.astype(BF16), preferred_element_type=F32)
        state_ref[h] = math.exp(log_g * c) * state + kv
        mu = jnp.mean(o, axis=-1, keepdims=True)
        d = o - mu
        var = jnp.mean(d * d, axis=-1, keepdims=True)
        o = d * lax.rsqrt(var + LN_EPS)
        gate = g_ref[:, sl].astype(F32)
        o_ref[:, sl] = (o * (gate * jax.nn.sigmoid(gate))).astype(o_ref.dtype)


def _retention(proj, batch, seq):
    n = seq // RET_CHUNK
    blk = lambda col: pl.BlockSpec((RET_CHUNK, RET_WIDTH), lambda b, i: (b * n + i, col))
    return pl.pallas_call(
        _retention_kernel,
        grid=(batch, n),
        in_specs=[blk(OFF_RQ // RET_WIDTH), blk(OFF_RK // RET_WIDTH),
                  blk(OFF_RV // RET_WIDTH), blk(OFF_RG // RET_WIDTH)],
        out_specs=pl.BlockSpec((RET_CHUNK, RET_WIDTH), lambda b, i: (b * n + i, 0)),
        out_shape=jax.ShapeDtypeStruct((batch * seq, RET_WIDTH), BF16),
        scratch_shapes=[pltpu.VMEM((RET_HEADS, RET_DIM, RET_DIM), F32),
                        pltpu.VMEM((RET_HEADS, RET_CHUNK, RET_CHUNK), F32),
                        pltpu.VMEM((RET_HEADS, RET_CHUNK, RET_DIM), F32),
                        pltpu.VMEM((RET_HEADS, RET_CHUNK, RET_DIM), F32)],
        compiler_params=_params(("parallel", "arbitrary")),
        name="retention",
    )(proj, proj, proj, proj)


def _gelu_tanh(x):
    return 0.5 * x * (1.0 + jnp.tanh(math.sqrt(2.0 / math.pi) * (x + 0.044715 * (x * x * x))))


def _compress_kernel(zk_ref, zv_ref, pos_ref, w1_ref, w2_ref, kc_ref, vc_ref, zf_ref):
    nsub = zf_ref.shape[0] // CMP_STRIDE
    for kv, (z_ref, o_ref) in enumerate(((zk_ref, kc_ref), (zv_ref, vc_ref))):
        zf_ref[...] = z_ref[...].astype(F32)
        hid_a = jnp.zeros((nsub, LANES), F32)
        hid_b = jnp.zeros((nsub, LANES), F32)
        for i in range(CMP_STRIDE):
            rows = zf_ref[pl.ds(i, nsub, stride=CMP_STRIDE), :]
            xa = (rows + pos_ref[kv, i:i + 1, :]).astype(BF16)
            xb = (rows + pos_ref[kv, CMP_STRIDE + i:CMP_STRIDE + i + 1, :]).astype(BF16)
            wa = w1_ref[kv, i * NSA_DIM:(i + 1) * NSA_DIM, :]
            wb = w1_ref[kv, (CMP_STRIDE + i) * NSA_DIM:(CMP_STRIDE + i + 1) * NSA_DIM, :]
            hid_a = hid_a + jnp.dot(xa, wa, preferred_element_type=F32)
            hid_b = hid_b + jnp.dot(xb, wb, preferred_element_type=F32)
        hid = _gelu_tanh(hid_a + pltpu.roll(hid_b, nsub - 1, 0))
        o_ref[...] = jnp.dot(hid.astype(BF16), w2_ref[kv], preferred_element_type=F32).astype(o_ref.dtype)


def _compress(proj, cmp_pos, cmp_w1, cmp_w2, batch, seq):
    nsub = seq // CMP_STRIDE
    zspec = lambda col0: pl.BlockSpec((seq, NSA_DIM), lambda b, g: (b, col0 + g))
    full = lambda a: pl.BlockSpec(a.shape, lambda b, g: (0,) * a.ndim)
    ospec = pl.BlockSpec((None, None, nsub, NSA_DIM), lambda b, g: (b, g, 0, 0))
    oshape = jax.ShapeDtypeStruct((batch, NSA_KV_GROUPS, nsub, NSA_DIM), BF16)
    return pl.pallas_call(
        _compress_kernel,
        grid=(batch, NSA_KV_GROUPS),
        in_specs=[zspec(OFF_KC // NSA_DIM), zspec(OFF_VC // NSA_DIM),
                  full(cmp_pos), full(cmp_w1), full(cmp_w2)],
        out_specs=[ospec, ospec],
        out_shape=[oshape, oshape],
        scratch_shapes=[pltpu.VMEM((seq, NSA_DIM), F32)],
        compiler_params=_params(("parallel", "parallel")),
        name="compress",
    )(proj, proj, cmp_pos, cmp_w1, cmp_w2)


def _nsa_kernel(q_ref, kc_ref, vc_ref, ks_ref, vs_ref, kw_ref, vw_ref, gate_ref, wo_ref, w2_ref,
                o_ref, wob_ref, w2b_ref,
                q4_ref, vct_ref, vst_ref, vwt_ref, slope_ref, bias_ref, sig_ref, m_ref, l_ref, acc_ref, out_ref):
    tq, tk, rep, groups = NSA_TQ, NSA_TK, NSA_REP, NSA_KV_GROUPS
    cols = rep * tq
    n_tiles = ks_ref.shape[0] // tk
    n_blocks = ks_ref.shape[0] // SLC_LEN
    n_cmp = kc_ref.shape[1]
    per_tile = tk // SLC_LEN
    slc, win = 0, 1
    i = pl.program_id(1)
    t0 = i * tq
    gcols = lambda g: slice(g * NSA_DIM, (g + 1) * NSA_DIM)

    wob_ref[...] = wo_ref[...].astype(BF16)
    w2b_ref[...] = w2_ref[...].astype(BF16)

    @pl.when(i == 0)
    def _():
        for g in range(groups):
            vct_ref[g] = vc_ref[g].astype(F32).T.astype(BF16)

        def transpose_tile(t, carry):
            rows = pl.ds(pl.multiple_of(t * tk, tk), tk)
            for g in range(groups):
                vst_ref[g, t] = vs_ref[rows, gcols(g)].astype(F32).T.astype(BF16)
                vwt_ref[g, t] = vw_ref[rows, gcols(g)].astype(F32).T.astype(BF16)
            return carry

        lax.fori_loop(0, n_tiles, transpose_tile, 0)

    sub = lax.broadcasted_iota(jnp.int32, (tk, cols), 0)
    t_loc = lax.broadcasted_iota(jnp.int32, (tk, cols), 1) & (tq - 1)
    lane_head = lax.broadcasted_iota(jnp.int32, (1, cols), 1) >> (tq.bit_length() - 1)
    causal = sub <= t_loc
    c_end = lax.broadcasted_iota(jnp.int32, (n_cmp, cols), 0) * CMP_STRIDE + (CMP_LEN - 1)
    c_mask = c_end <= t0 + (lax.broadcasted_iota(jnp.int32, (n_cmp, cols), 1) & (tq - 1))
    s_idx = lax.broadcasted_iota(jnp.int32, (n_blocks, n_cmp), 0)
    c_idx = lax.broadcasted_iota(jnp.int32, (n_blocks, n_cmp), 1)
    overlap = jnp.maximum(jnp.minimum(c_idx * CMP_STRIDE + CMP_LEN, s_idx * SLC_LEN + SLC_LEN)
                          - jnp.maximum(c_idx * CMP_STRIDE, s_idx * SLC_LEN), 0)
    overlap = (overlap.astype(F32) * (1.0 / CMP_LEN)).astype(BF16)
    blk = lax.broadcasted_iota(jnp.int32, (n_blocks, tq), 0)
    cur = (t0 + lax.broadcasted_iota(jnp.int32, (n_blocks, tq), 1)) >> (SLC_LEN.bit_length() - 1)
    forced = (blk == 0) | (blk == cur) | (blk == cur - 1)
    future = blk > cur
    tile_start = (lax.broadcasted_iota(jnp.int32, (n_blocks, cols), 0) // per_tile) * tk

    sig_ref[...] = jax.nn.sigmoid(gate_ref[...]).T

    def gate_row(g, branch):
        first = branch * NSA_HEADS + g * rep
        return jnp.concatenate([sig_ref[first + r:first + r + 1, :] for r in range(rep)], axis=1)

    pos = lax.broadcasted_iota(jnp.int32, (tk, NSA_DIM), 0)
    pos_lane = lax.broadcasted_iota(jnp.int32, (tk, NSA_DIM), 1)
    pos_cols = jnp.where((pos_lane & 1) == 0, (pos // SLC_LEN) * SLC_LEN, pos % SLC_LEN)
    pos_cols = jnp.where(pos_lane < 2 * SLOPE_TERMS, pos_cols, 0).astype(F32).astype(BF16)
    row_head = lax.broadcasted_iota(jnp.int32, (cols, NSA_DIM), 0) >> (tq.bit_length() - 1)
    row_lane = lax.broadcasted_iota(jnp.int32, (cols, NSA_DIM), 1)

    def alibi_slope(head):
        return LOG2E * jnp.exp((head + 1).astype(F32) * (-0.5 * math.log(2.0)))

    for g in range(groups):
        q4 = jnp.concatenate([q_ref[:, (g * rep + r) * NSA_DIM:(g * rep + r + 1) * NSA_DIM]
                              for r in range(rep)], axis=0)
        q4 = (q4.astype(F32) * (LOG2E * NSA_DIM ** -0.5)).astype(BF16)
        rest = alibi_slope(g * rep + row_head)
        slope_cols = jnp.zeros((cols, NSA_DIM), F32)
        for n in range(SLOPE_TERMS):
            term = rest.astype(BF16).astype(F32)
            rest = rest - term
            slope_cols = jnp.where((row_lane >> 1) == n, term, slope_cols)
        q4_ref[g] = jnp.concatenate([q4, slope_cols.astype(BF16)], axis=1)
        slope = alibi_slope(g * rep + lane_head)
        slope_ref[g] = slope

        st = lax.dot_general(kc_ref[g], q4, _NT, preferred_element_type=F32)
        st = jnp.where(c_mask, st + slope * c_end.astype(F32), MASKED)
        e = jnp.exp2(st - jnp.max(st, axis=0, keepdims=True))
        p_cmp = jnp.where(c_mask, e * (1.0 / jnp.sum(e, axis=0, keepdims=True)), 0.0)
        o_cmp = jnp.dot(vct_ref[g], p_cmp.astype(BF16), preferred_element_type=F32)
        out_ref[g] = gate_row(g, 0) * o_cmp

        p_sum = p_cmp[:, 0:tq]
        for r in range(1, rep):
            p_sum = p_sum + p_cmp[:, r * tq:(r + 1) * tq]
        p_hi = p_sum.astype(BF16)
        p_lo = (p_sum - p_hi.astype(F32)).astype(BF16)
        imp = (jnp.dot(overlap, p_hi, preferred_element_type=F32)
               + jnp.dot(overlap, p_lo, preferred_element_type=F32))
        imp = jnp.where(forced, FORCE, jnp.where(future, -FORCE, imp))
        rank = jnp.zeros((n_blocks, tq), F32)
        for j in range(n_blocks):
            row = imp[j:j + 1, :]
            before = (row > imp) | ((row == imp) & (blk > j))
            rank = rank + jnp.where(before, 1.0, 0.0)
        sel_bias = jnp.where(rank < SLC_TOPK, 0.0, MASKED)
        bias_ref[g] = jnp.concatenate([sel_bias] * rep, axis=1) + slope * tile_start.astype(F32)

    def scores(g, k):
        k = jnp.concatenate([k, pos_cols], axis=1)
        return lax.dot_general(k, q4_ref[g], _NT, preferred_element_type=F32)

    def update(branch, g, st, vt, biases, keep, first, after):
        if after is not None:
            vt = vt + (0.0 * after[0:1, 0:tk]).astype(BF16)
        part = tk // len(biases)
        st = jnp.concatenate([st[n * part:(n + 1) * part] + b for n, b in enumerate(biases)], axis=0)
        if keep is not None:
            st = jnp.where(keep, st, MASKED)
        m_tile = jnp.max(st, axis=0, keepdims=True)
        if first:
            p = jnp.exp2(st - m_tile)
            m_ref[branch, g] = m_tile
            l_ref[branch, g] = jnp.sum(p, axis=0, keepdims=True)
            acc_ref[branch, g] = jnp.dot(vt, p.astype(BF16), preferred_element_type=F32)
        else:
            m_old = m_ref[branch, g]
            m_new = jnp.maximum(m_old, m_tile)
            a = jnp.exp2(m_old - m_new)
            p = jnp.exp2(st - m_new)
            m_ref[branch, g] = m_new
            l_ref[branch, g] = a * l_ref[branch, g] + jnp.sum(p, axis=0, keepdims=True)
            acc_ref[branch, g] = a * acc_ref[branch, g] + jnp.dot(vt, p.astype(BF16), preferred_element_type=F32)

    def slc_chains(kt):
        rows = pl.ds(pl.multiple_of(kt * tk, tk), tk)
        return [(slc, g, ks_ref[rows, gcols(g)], vst_ref[g, kt],
                 [bias_ref[g, pl.ds(per_tile * kt + n, 1), :] for n in range(per_tile)])
                for g in range(groups)]

    def win_chains(kt):
        rows = pl.ds(pl.multiple_of(kt * tk, tk), tk)
        start = (kt * tk).astype(F32)
        return [(win, g, kw_ref[rows, gcols(g)], vwt_ref[g, kt], [slope_ref[g] * start])
                for g in range(groups)]

    def run(chains, keep, first):
        sts = [scores(g, k) for (_, g, k, _, _) in chains]
        for n, (branch, g, _, vt, biases) in enumerate(chains):
            update(branch, g, sts[n], vt, biases, keep, first, sts[n + 1] if n + 1 < len(sts) else None)

    run(slc_chains(i) + win_chains(i), causal, True)

    def slc_body(kt, carry):
        run(slc_chains(kt), None, False)
        return carry

    lax.fori_loop(0, i, slc_body, 0)

    n_back = WIN_LEN // tk

    @pl.when(i >= n_back)
    def _():
        run(win_chains(i - n_back), sub > t_loc, False)

    def win_body(kt, carry):
        run(win_chains(kt), None, False)
        return carry

    lax.fori_loop(jnp.maximum(i - (n_back - 1), 0), i, win_body, 0)

    for g in range(groups):
        o = out_ref[g]
        for branch, state in ((1, slc), (2, win)):
            o = o + (gate_row(g, branch) * (1.0 / l_ref[state, g])) * acc_ref[state, g]
        for r in range(rep):
            h = g * rep + r
            o_ref[:, h * NSA_DIM:(h + 1) * NSA_DIM] = o[:, r * tq:(r + 1) * tq].T.astype(o_ref.dtype)


def _nsa(proj, gates, kc, vc, w_out, w_ff2, layer, batch, seq):
    nq = seq // NSA_TQ
    step = lambda b, i: b * nq + i
    wo_in, wo_out, wo_shape = _cast_slab_specs(w_out, layer, batch * nq, step)
    w2_in, w2_out, w2_shape = _cast_slab_specs(w_ff2, layer, batch * nq, step, FFN_W2_SPLIT,
                                               w_ff2.shape[2] // FFN_W2_SPLIT - 1)
    groups, cols = NSA_KV_GROUPS, NSA_REP * NSA_TQ
    n_tiles = seq // NSA_TK
    kvspec = lambda off: pl.BlockSpec((seq, NSA_KV_WIDTH), lambda b, i: (b, off // NSA_KV_WIDTH))
    cspec = lambda a: pl.BlockSpec((None,) + a.shape[1:], lambda b, i: (b, 0, 0, 0))
    return pl.pallas_call(
        _nsa_kernel,
        grid=(batch, nq),
        in_specs=[pl.BlockSpec((NSA_TQ, NSA_WIDTH), lambda b, i: (b * nq + i, OFF_NQ // NSA_WIDTH)),
                  cspec(kc), cspec(vc),
                  kvspec(OFF_KS), kvspec(OFF_VS), kvspec(OFF_KW), kvspec(OFF_VW),
                  pl.BlockSpec((NSA_TQ, LANES), lambda b, i: (b * nq + i, 0)), wo_in, w2_in],
        out_specs=[pl.BlockSpec((NSA_TQ, NSA_WIDTH), lambda b, i: (b * nq + i, 0)), wo_out, w2_out],
        out_shape=[jax.ShapeDtypeStruct((batch * seq, NSA_WIDTH), BF16), wo_shape, w2_shape],
        scratch_shapes=[pltpu.VMEM((groups, cols, 2 * NSA_DIM), BF16),
                        pltpu.VMEM((groups, NSA_DIM, seq // CMP_STRIDE), BF16),
                        pltpu.VMEM((groups, n_tiles, NSA_DIM, NSA_TK), BF16),
                        pltpu.VMEM((groups, n_tiles, NSA_DIM, NSA_TK), BF16),
                        pltpu.VMEM((groups, 1, cols), F32),
                        pltpu.VMEM((groups, seq // SLC_LEN, cols), F32),
                        pltpu.VMEM((LANES, NSA_TQ), F32),
                        pltpu.VMEM((2, groups, 1, cols), F32),
                        pltpu.VMEM((2, groups, 1, cols), F32),
                        pltpu.VMEM((2, groups, NSA_DIM, cols), F32),
                        pltpu.VMEM((groups, NSA_DIM, cols), F32)],
        compiler_params=_params(("parallel", "arbitrary")),
        name="nsa",
    )(proj, kc, vc, proj, proj, proj, proj, gates, w_out, w_ff2)


def _outproj_kernel(ret_ref, nsa_ref, wr_ref, wn_ref, x_ref, g_ref, b_ref, o_ref, mu_ref, rstd_ref):
    j = pl.program_id(1)
    nj = pl.num_programs(1)
    tn = x_ref.shape[1]
    y = (jnp.dot(ret_ref[...], wr_ref[...], preferred_element_type=F32)
         + jnp.dot(nsa_ref[...], wn_ref[...], preferred_element_type=F32))
    y = DN_ALPHA * x_ref[...] + y
    for jj in range(o_ref.shape[1] // tn):
        @pl.when(j == jj)
        def _(jj=jj):
            o_ref[:, jj * tn:(jj + 1) * tn] = y

    @pl.when(j == nj - 1)
    def _():
        _layer_norm_rows(o_ref, lambda rows, cols: o_ref[rows, cols], g_ref, b_ref, mu_ref, rstd_ref)


def _outproj(ret, nsa, w_out, x, g, b, tm, tn):
    m, d = x.shape
    vec = pl.BlockSpec((1, d), lambda i, j: (0, 0))
    return pl.pallas_call(
        _outproj_kernel,
        grid=(m // tm, d // tn),
        in_specs=[pl.BlockSpec((tm, RET_WIDTH), lambda i, j: (i, 0)),
                  pl.BlockSpec((tm, NSA_WIDTH), lambda i, j: (i, 0)),
                  pl.BlockSpec((RET_WIDTH, tn), lambda i, j: (0, j)),
                  pl.BlockSpec((NSA_WIDTH, tn), lambda i, j: (RET_WIDTH // NSA_WIDTH, j)),
                  pl.BlockSpec((tm, tn), lambda i, j: (i, j)),
                  vec, vec],
        out_specs=pl.BlockSpec((tm, d), lambda i, j: (i, 0)),
        out_shape=jax.ShapeDtypeStruct((m, d), F32),
        scratch_shapes=_row_stats(tm),
        compiler_params=_params(("parallel", "arbitrary")),
        name="outproj_ln",
    )(ret, nsa, w_out, w_out, x, g, b)


def _xattn_kernel(h_ref, kv_ref, wq_ref, wo_ref, g_ref, b_ref, w2_ref, o_ref, w2b_ref, mu_ref, rstd_ref):
    w2b_ref[...] = w2_ref[...].astype(BF16)

    q = jnp.dot(h_ref[...].astype(BF16), wq_ref[...], preferred_element_type=F32).astype(BF16)
    outs = []
    for hd in range(XA_HEADS):
        sl = slice(hd * XA_DIM, (hd + 1) * XA_DIM)
        k = kv_ref[:, sl]
        v = kv_ref[:, XA_WIDTH + hd * XA_DIM:XA_WIDTH + (hd + 1) * XA_DIM]
        s = lax.dot_general(q[:, sl], k, _NT, preferred_element_type=F32) * (XA_DIM ** -0.5)
        e = jnp.exp(s - jnp.max(s, axis=-1, keepdims=True))
        p = e / jnp.sum(e, axis=-1, keepdims=True)
        outs.append(jnp.dot(p.astype(BF16), v, preferred_element_type=F32))
    o = jnp.concatenate(outs, axis=-1).astype(BF16)
    o_ref[...] = jnp.dot(o, wo_ref[...], preferred_element_type=F32)
    _layer_norm_rows(o_ref, lambda rows, cols: DN_ALPHA * h_ref[rows, cols] + o_ref[rows, cols], g_ref, b_ref,
                     mu_ref, rstd_ref)


def _xattn(h, kv, wq, wo, g, b, w_ff2, layer, seq, mem_len, tm):
    m, d = h.shape
    per_batch = seq // tm
    steps = m // tm
    full = lambda a: pl.BlockSpec(a.shape, lambda i: (0, 0), pipeline_mode=pl.Buffered(1))
    w2_in, w2_out, w2_shape = _cast_slab_specs(w_ff2, layer, steps, lambda i: i, d - FFN_W2_SPLIT, 0)
    return pl.pallas_call(
        _xattn_kernel,
        grid=(steps,),
        in_specs=[pl.BlockSpec((tm, d), lambda i: (i, 0)),
                  pl.BlockSpec((mem_len, 2 * XA_WIDTH), lambda i: (i // per_batch, 0)),
                  full(wq), full(wo), full(g), full(b), w2_in],
        out_specs=[pl.BlockSpec((tm, d), lambda i: (i, 0)), w2_out],
        out_shape=[jax.ShapeDtypeStruct((m, d), F32), w2_shape],
        scratch_shapes=_row_stats(tm),
        compiler_params=_params(("parallel",)),
        name="xattn_ln",
    )(h, kv, wq, wo, g, b, w_ff2)


def _ffn_kernel(h_ref, w1_ref, w2a_ref, w2b_ref, g_ref, b_ref, o_ref, hb_ref, mu_ref, rstd_ref):
    f = pl.program_id(1)
    nf = pl.num_programs(1)

    @pl.when(f == 0)
    def _():
        hb_ref[...] = h_ref[...].astype(BF16)
        o_ref[...] = jnp.zeros_like(o_ref)

    u = jnp.maximum(jnp.dot(hb_ref[...], w1_ref[...], preferred_element_type=F32), 0.0)
    u = (u * u).astype(BF16)
    split = w2a_ref.shape[1]
    for c in range(o_ref.shape[1] // FFN_TN):
        cols = slice(c * FFN_TN, (c + 1) * FFN_TN)
        w2_ref, first = (w2a_ref, 0) if c * FFN_TN < split else (w2b_ref, split)
        w2 = w2_ref[:, c * FFN_TN - first:(c + 1) * FFN_TN - first]
        o_ref[:, cols] += jnp.dot(u, w2, preferred_element_type=F32)

    @pl.when(f == nf - 1)
    def _():
        _layer_norm_rows(o_ref, lambda rows, cols: DN_ALPHA * h_ref[rows, cols] + o_ref[rows, cols], g_ref, b_ref,
                     mu_ref, rstd_ref)


def _ffn(h, w1, w2a, w2b, g, b, tm, tf):
    m, d = h.shape
    d_ff = w1.shape[1]
    vec = pl.BlockSpec((1, d), lambda i, f: (0, 0))
    return pl.pallas_call(
        _ffn_kernel,
        grid=(m // tm, d_ff // tf),
        in_specs=[pl.BlockSpec((tm, d), lambda i, f: (i, 0)),
                  pl.BlockSpec((d, tf), lambda i, f: (0, f)),
                  pl.BlockSpec((tf, w2a.shape[1]), lambda i, f: (f, 0)),
                  pl.BlockSpec((tf, w2b.shape[1]), lambda i, f: (f, 0)),
                  vec, vec],
        out_specs=pl.BlockSpec((tm, d), lambda i, f: (i, 0)),
        out_shape=jax.ShapeDtypeStruct((m, d), F32),
        scratch_shapes=[pltpu.VMEM((tm, d), BF16)] + _row_stats(tm),
        compiler_params=_params(("parallel", "arbitrary")),
        name="ffn_ln",
    )(h, w1, w2a, w2b, g, b)


def kernel(x, mem, w_in, w_out, cmp_pos, cmp_w1, cmp_w2, xa_wq, xa_wkv, xa_wo,
           w_ff1, w_ff2, ln_g, ln_b):
    batch, seq, d = x.shape
    mem_len = mem.shape[1]
    h = x.reshape(batch * seq, d)
    mem2 = mem.reshape(batch * mem_len, d).astype(BF16)
    for l in range(DEPTH):
        w_in_t = jnp.swapaxes(w_in[l], 0, 1).astype(BF16)
        wg_t = jnp.pad(w_in_t[OFF_GATES:], ((0, LANES - GATE_WIDTH), (0, 0)))
        hb, gates = _cast_gates(h, wg_t, 512)
        proj, w_ff1_b = _inproj(hb, w_in_t, OFF_GATES, w_ff1, l, 1024, 1024)
        ret = _retention(proj, batch, seq)
        kc, vc = _compress(proj, cmp_pos[l], cmp_w1[l].astype(BF16), cmp_w2[l].astype(BF16),
                           batch, seq)
        sparse, w_out_b, w_ff2_tail = _nsa(proj, gates, kc, vc, w_out, w_ff2, l, batch, seq)
        vecs = lambda a, k: a[l, k].reshape(1, d)
        h = _outproj(ret, sparse, w_out_b, h, vecs(ln_g, 0), vecs(ln_b, 0), 512, 1024)
        kv = _matmul(mem2, xa_wkv[l].astype(BF16), 2 * XA_WIDTH, 512, 1024, BF16)
        h, w_ff2_head = _xattn(h, kv, xa_wq[l].astype(BF16), xa_wo[l].astype(BF16),
                               vecs(ln_g, 1), vecs(ln_b, 1), w_ff2, l, seq, mem_len, 256)
        h = _ffn(h, w_ff1_b, w_ff2_head, w_ff2_tail, vecs(ln_g, 2), vecs(ln_b, 2), 512, 512)
    return h.reshape(batch, seq, d)
```

```python
import math

import jax
import jax.numpy as jnp
from jax import lax
from jax.experimental import pallas as pl
from jax.experimental.pallas import tpu as pltpu

F32 = jnp.float32
BF16 = jnp.bfloat16

RET_HEADS = 8
RET_DIM = 256
RET_WIDTH = RET_HEADS * RET_DIM
RET_CHUNK = 128
NSA_HEADS = 16
NSA_KV_GROUPS = 4
NSA_REP = NSA_HEADS // NSA_KV_GROUPS
NSA_DIM = 128
NSA_WIDTH = NSA_HEADS * NSA_DIM
NSA_KV_WIDTH = NSA_KV_GROUPS * NSA_DIM
CMP_LEN = 32
CMP_STRIDE = 16
SLC_LEN = 64
SLC_TOPK = 16
WIN_LEN = 512
GATE_WIDTH = 3 * NSA_HEADS
OFF_RQ, OFF_RK, OFF_RV, OFF_RG = 0, RET_WIDTH, 2 * RET_WIDTH, 3 * RET_WIDTH
OFF_NQ = 4 * RET_WIDTH
OFF_KC = OFF_NQ + NSA_WIDTH
OFF_VC = OFF_KC + NSA_KV_WIDTH
OFF_KS = OFF_VC + NSA_KV_WIDTH
OFF_VS = OFF_KS + NSA_KV_WIDTH
OFF_KW = OFF_VS + NSA_KV_WIDTH
OFF_VW = OFF_KW + NSA_KV_WIDTH
OFF_GATES = OFF_VW + NSA_KV_WIDTH
XA_HEADS = 4
XA_DIM = 128
XA_WIDTH = XA_HEADS * XA_DIM
LN_EPS = 1e-5
DEPTH = 1
DN_ALPHA = (2.0 * DEPTH) ** 0.25
NEG_INF = -1e30
MASKED = 2.0 * NEG_INF
FORCE = 1e9
LOG2E = math.log2(math.e)
SLOPE_TERMS = 3

LANES = 128
VMEM_LIMIT = 56 * 1024 * 1024
CAST_TM = 512
INPROJ_TM, INPROJ_TN = 1024, 1024
INPROJ_SIDE_STEPS = 8
RET_CHUNKS_PER_STEP = 4
NSA_TQ = 256
NSA_TK = 256
OUTPROJ_TM, OUTPROJ_TN = 512, 1024
KV_TM, KV_TN = 512, 1024
XATTN_TM = 256
FFN_TM, FFN_TF = 512, 512
FFN_TN = 1024
FFN_W2_SPLIT = 1024
LN_ROWS = 8
LN_STATS_UNROLL = 16
LN_APPLY_UNROLL = 4

_NT = (((1,), (1,)), ((), ()))
_TN = (((0,), (0,)), ((), ()))


def _params(sem):
    return pltpu.CompilerParams(dimension_semantics=sem, vmem_limit_bytes=VMEM_LIMIT)


def _cast_slab_specs(w, layer, steps, step_index, cols=None, col_block=0):
    rows, cols = w.shape[1] // steps, cols or w.shape[2]
    in_spec = pl.BlockSpec((None, rows, cols), lambda *ids: (layer, step_index(*ids), col_block))
    out_spec = pl.BlockSpec((rows, cols), lambda *ids: (step_index(*ids), 0))
    return in_spec, out_spec, jax.ShapeDtypeStruct((w.shape[1], cols), BF16)


def _row_stats(tm):
    return [pltpu.VMEM((tm, LANES), F32), pltpu.VMEM((tm, LANES), F32)]


def _layer_norm_rows(o_ref, pre_norm, g_ref, b_ref, mu_ref, rstd_ref):
    tm, d = o_ref.shape
    n = tm // LN_ROWS

    def stats(c, carry):
        rows = pl.ds(pl.multiple_of(c * LN_ROWS, LN_ROWS), LN_ROWS)
        y = pre_norm(rows, slice(None))
        mu = jnp.mean(y, axis=-1, keepdims=True)
        dev = y - mu
        rstd = lax.rsqrt(jnp.mean(dev * dev, axis=-1, keepdims=True) + LN_EPS)
        mu_ref[rows, :] = jnp.broadcast_to(mu, (LN_ROWS, LANES))
        rstd_ref[rows, :] = jnp.broadcast_to(rstd, (LN_ROWS, LANES))
        return carry

    def apply(c, carry):
        rows = pl.ds(pl.multiple_of(c * LN_ROWS, LN_ROWS), LN_ROWS)
        mu, rstd = mu_ref[rows, :], rstd_ref[rows, :]
        for j in range(d // LANES):
            cols = slice(j * LANES, (j + 1) * LANES)
            o_ref[rows, cols] = (pre_norm(rows, cols) - mu) * rstd * g_ref[:, cols] + b_ref[:, cols]
        return carry

    lax.fori_loop(0, n, stats, 0, unroll=min(LN_STATS_UNROLL, n))
    lax.fori_loop(0, n, apply, 0, unroll=LN_APPLY_UNROLL)


def _mm_kernel(x_ref, w_ref, o_ref):
    o_ref[...] = jnp.dot(x_ref[...], w_ref[...], preferred_element_type=F32).astype(o_ref.dtype)


def _matmul(x, w, n_out, tm, tn, out_dtype):
    m, k = x.shape
    return pl.pallas_call(
        _mm_kernel,
        grid=(m // tm, n_out // tn),
        in_specs=[pl.BlockSpec((tm, k), lambda i, j: (i, 0)),
                  pl.BlockSpec((k, tn), lambda i, j: (0, j))],
        out_specs=pl.BlockSpec((tm, tn), lambda i, j: (i, j)),
        out_shape=jax.ShapeDtypeStruct((m, n_out), out_dtype),
        compiler_params=_params(("parallel", "parallel")),
        name="matmul",
    )(x, w)


def _cast_gates_kernel(x_ref, wgt_ref, xb_ref, g_ref):
    xb = x_ref[...].astype(BF16)
    xb_ref[...] = xb
    g_ref[...] = lax.dot_general(xb, wgt_ref[...], _NT, preferred_element_type=F32)


def _cast_gates(x, wg_t, tm):
    m, k = x.shape
    return pl.pallas_call(
        _cast_gates_kernel,
        grid=(m // tm,),
        in_specs=[pl.BlockSpec((tm, k), lambda i: (i, 0)),
                  pl.BlockSpec((LANES, k), lambda i: (0, 0), pipeline_mode=pl.Buffered(1))],
        out_specs=[pl.BlockSpec((tm, k), lambda i: (i, 0)), pl.BlockSpec((tm, LANES), lambda i: (i, 0))],
        out_shape=[jax.ShapeDtypeStruct((m, k), BF16), jax.ShapeDtypeStruct((m, LANES), F32)],
        compiler_params=_params(("parallel",)),
        name="cast_gates",
    )(x, wg_t)


def _inproj_kernel(x_ref, w_ref, w1_ref, o_ref, w1b_ref):
    @pl.when(pl.program_id(1) < INPROJ_SIDE_STEPS)
    def _():
        w1b_ref[...] = w1_ref[...].astype(BF16)

    o_ref[...] = lax.dot_general(x_ref[...], w_ref[...], _NT, preferred_element_type=F32).astype(o_ref.dtype)


def _inproj(x, w_t, n_out, w_ff1, layer, tm, tn):
    m, k = x.shape
    side = INPROJ_SIDE_STEPS
    slab = lambda i, j: i * side + jnp.minimum(j, side - 1)
    w1_in, w1_out, w1_shape = _cast_slab_specs(w_ff1, layer, (m // tm) * side, slab)
    return pl.pallas_call(
        _inproj_kernel,
        grid=(m // tm, n_out // tn),
        in_specs=[pl.BlockSpec((tm, k), lambda i, j: (i, 0)),
                  pl.BlockSpec((tn, k), lambda i, j: (j, 0)), w1_in],
        out_specs=[pl.BlockSpec((tm, tn), lambda i, j: (i, j)), w1_out],
        out_shape=[jax.ShapeDtypeStruct((m, n_out), BF16), w1_shape],
        compiler_params=_params(("parallel", "arbitrary")),
        name="inproj",
    )(x, w_t, w_ff1)


def _retention_kernel(q_ref, k_ref, v_ref, g_ref, o_ref, state_ref, decay_ref, ktail_ref, qhead_ref):
    c = RET_CHUNK

    @pl.when(pl.program_id(1) == 0)
    def _():
        state_ref[...] = jnp.zeros_like(state_ref)
        diff = (lax.broadcasted_iota(jnp.int32, (c, c), 0)
                - lax.broadcasted_iota(jnp.int32, (c, c), 1)).astype(F32)
        pos = lax.broadcasted_iota(jnp.int32, (c, RET_DIM), 0).astype(F32)
        for h in range(RET_HEADS):
            log_g = math.log1p(-(2.0 ** (-5.0 - h)))
            decay_ref[h] = jnp.where(diff >= 0, jnp.exp(log_g * jnp.maximum(diff, 0.0)), 0.0)
            ktail_ref[h] = jnp.exp(log_g * (c - 1.0 - pos))
            qhead_ref[h] = jnp.exp(log_g * (pos + 1.0))

    for chunk, h in [(ch, hd) for ch in range(q_ref.shape[0] // c) for hd in range(RET_HEADS)]:
        log_g = math.log1p(-(2.0 ** (-5.0 - h)))
        sl = (slice(chunk * c, (chunk + 1) * c), slice(h * RET_DIM, (h + 1) * RET_DIM))
        q = q_ref[sl].astype(F32)
        k = k_ref[sl].astype(F32) * (RET_DIM ** -0.5)
        v = v_ref[sl]
        scores = lax.dot_general(q.astype(BF16), k.astype(BF16), _NT,
                                 preferred_element_type=F32) * decay_ref[h]
        o_inner = jnp.dot(scores.astype(BF16), v, preferred_element_type=F32)
        k_tail = k * ktail_ref[h]
        kv = lax.dot_general(k_tail.astype(BF16), v, _TN, preferred_element_type=F32)
        q_head = q * qhead_ref[h]
        state = state_ref[h]
        o = o_inner + jnp.dot(q_head.astype(BF16), state.astype(BF16), preferred_element_type=F32)
        state_ref[h] = math.exp(log_g * c) * state + kv
        mu = jnp.mean(o, axis=-1, keepdims=True)
        d = o - mu
        var = jnp.mean(d * d, axis=-1, keepdims=True)
        o = d * lax.rsqrt(var + LN_EPS)
        gate = g_ref[sl].astype(F32)
        o_ref[sl] = (o * (gate * jax.nn.sigmoid(gate))).astype(o_ref.dtype)


def _retention(proj, batch, seq):
    rows = RET_CHUNKS_PER_STEP * RET_CHUNK
    n = seq // rows
    blk = lambda col: pl.BlockSpec((rows, RET_WIDTH), lambda b, i: (b * n + i, col))
    return pl.pallas_call(
        _retention_kernel,
        grid=(batch, n),
        in_specs=[blk(OFF_RQ // RET_WIDTH), blk(OFF_RK // RET_WIDTH),
                  blk(OFF_RV // RET_WIDTH), blk(OFF_RG // RET_WIDTH)],
        out_specs=pl.BlockSpec((rows, RET_WIDTH), lambda b, i: (b * n + i, 0)),
        out_shape=jax.ShapeDtypeStruct((batch * seq, RET_WIDTH), BF16),
        scratch_shapes=[pltpu.VMEM((RET_HEADS, RET_DIM, RET_DIM), F32),
                        pltpu.VMEM((RET_HEADS, RET_CHUNK, RET_CHUNK), F32),
                        pltpu.VMEM((RET_HEADS, RET_CHUNK, RET_DIM), F32),
                        pltpu.VMEM((RET_HEADS, RET_CHUNK, RET_DIM), F32)],
        compiler_params=_params(("parallel", "arbitrary")),
        name="retention",
    )(proj, proj, proj, proj)


def _gelu_tanh(x):
    return 0.5 * x * (1.0 + jnp.tanh(math.sqrt(2.0 / math.pi) * (x + 0.044715 * (x * x * x))))


def _compress_kernel(zk_ref, zv_ref, pos_ref, w1_ref, w2_ref, kc_ref, vc_ref, zf_ref):
    nsub = zf_ref.shape[0] // CMP_STRIDE
    for kv, (z_ref, o_ref) in enumerate(((zk_ref, kc_ref), (zv_ref, vc_ref))):
        zf_ref[...] = z_ref[...].astype(F32)
        hid_a = jnp.zeros((nsub, LANES), F32)
        hid_b = jnp.zeros((nsub, LANES), F32)
        for i in range(CMP_STRIDE):
            rows = zf_ref[pl.ds(i, nsub, stride=CMP_STRIDE), :]
            xa = (rows + pos_ref[kv, i:i + 1, :]).astype(BF16)
            xb = (rows + pos_ref[kv, CMP_STRIDE + i:CMP_STRIDE + i + 1, :]).astype(BF16)
            wa = w1_ref[kv, i * NSA_DIM:(i + 1) * NSA_DIM, :]
            wb = w1_ref[kv, (CMP_STRIDE + i) * NSA_DIM:(CMP_STRIDE + i + 1) * NSA_DIM, :]
            hid_a = hid_a + jnp.dot(xa, wa, preferred_element_type=F32)
            hid_b = hid_b + jnp.dot(xb, wb, preferred_element_type=F32)
        hid = _gelu_tanh(hid_a + pltpu.roll(hid_b, nsub - 1, 0))
        o_ref[...] = jnp.dot(hid.astype(BF16), w2_ref[kv], preferred_element_type=F32).astype(o_ref.dtype)


def _compress(proj, cmp_pos, cmp_w1, cmp_w2, batch, seq):
    nsub = seq // CMP_STRIDE
    zspec = lambda col0: pl.BlockSpec((seq, NSA_DIM), lambda b, g: (b, col0 + g))
    full = lambda a: pl.BlockSpec(a.shape, lambda b, g: (0,) * a.ndim)
    ospec = pl.BlockSpec((None, None, nsub, NSA_DIM), lambda b, g: (b, g, 0, 0))
    oshape = jax.ShapeDtypeStruct((batch, NSA_KV_GROUPS, nsub, NSA_DIM), BF16)
    return pl.pallas_call(
        _compress_kernel,
        grid=(batch, NSA_KV_GROUPS),
        in_specs=[zspec(OFF_KC // NSA_DIM), zspec(OFF_VC // NSA_DIM),
                  full(cmp_pos), full(cmp_w1), full(cmp_w2)],
        out_specs=[ospec, ospec],
        out_shape=[oshape, oshape],
        scratch_shapes=[pltpu.VMEM((seq, NSA_DIM), F32)],
        compiler_params=_params(("parallel", "parallel")),
        name="compress",
    )(proj, proj, cmp_pos, cmp_w1, cmp_w2)


def _nsa_kernel(q_ref, kc_ref, vc_ref, ks_ref, vs_ref, kw_ref, vw_ref, gate_ref, wo_ref, w2_ref,
                o_ref, wob_ref, w2b_ref,
                q4_ref, vct_ref, vst_ref, vwt_ref, sig_ref, m_ref, l_ref, acc_ref, out_ref):
    tq, tk, rep, groups = NSA_TQ, NSA_TK, NSA_REP, NSA_KV_GROUPS
    cols = rep * tq
    n_tiles = ks_ref.shape[0] // tk
    n_blocks = ks_ref.shape[0] // SLC_LEN
    n_cmp = kc_ref.shape[1]
    slc, win = 0, 1
    i = pl.program_id(1)
    t0 = i * tq
    gcols = lambda g: slice(g * NSA_DIM, (g + 1) * NSA_DIM)

    wob_ref[...] = wo_ref[...].astype(BF16)
    w2b_ref[...] = w2_ref[...].astype(BF16)

    @pl.when(i == 0)
    def _():
        for g in range(groups):
            vct_ref[g] = vc_ref[g].astype(F32).T.astype(BF16)

        def transpose_tile(t, carry):
            rows = pl.ds(pl.multiple_of(t * tk, tk), tk)
            for g in range(groups):
                vst_ref[g, t] = vs_ref[rows, gcols(g)].astype(F32).T.astype(BF16)
                vwt_ref[g, t] = vw_ref[rows, gcols(g)].astype(F32).T.astype(BF16)
            return carry

        lax.fori_loop(0, n_tiles, transpose_tile, 0)

    sub = lax.broadcasted_iota(jnp.int32, (tk, cols), 0)
    t_loc = lax.broadcasted_iota(jnp.int32, (tk, cols), 1) & (tq - 1)
    lane_head = lax.broadcasted_iota(jnp.int32, (1, cols), 1) >> (tq.bit_length() - 1)
    causal = sub <= t_loc
    c_end = lax.broadcasted_iota(jnp.int32, (n_cmp, cols), 0) * CMP_STRIDE + (CMP_LEN - 1)
    c_mask = c_end <= t0 + (lax.broadcasted_iota(jnp.int32, (n_cmp, cols), 1) & (tq - 1))
    s_idx = lax.broadcasted_iota(jnp.int32, (n_blocks, n_cmp), 0)
    c_idx = lax.broadcasted_iota(jnp.int32, (n_blocks, n_cmp), 1)
    overlap = jnp.maximum(jnp.minimum(c_idx * CMP_STRIDE + CMP_LEN, s_idx * SLC_LEN + SLC_LEN)
                          - jnp.maximum(c_idx * CMP_STRIDE, s_idx * SLC_LEN), 0)
    overlap = (overlap.astype(F32) * (1.0 / CMP_LEN)).astype(BF16)
    blk = lax.broadcasted_iota(jnp.int32, (n_blocks, tq), 0)
    cur = (t0 + lax.broadcasted_iota(jnp.int32, (n_blocks, tq), 1)) >> (SLC_LEN.bit_length() - 1)
    forced = (blk == 0) | (blk == cur) | (blk == cur - 1)
    future = blk > cur

    sig_ref[...] = jax.nn.sigmoid(gate_ref[...]).T

    def gate_row(g, branch):
        first = branch * NSA_HEADS + g * rep
        return jnp.concatenate([sig_ref[first + r:first + r + 1, :] for r in range(rep)], axis=1)

    pos_lane, start_lane = n_blocks, n_blocks + 2 * SLOPE_TERMS
    pos = lax.broadcasted_iota(jnp.int32, (tk, NSA_DIM), 0)
    key_lane = lax.broadcasted_iota(jnp.int32, (tk, NSA_DIM), 1)
    in_pos = (key_lane >= pos_lane) & (key_lane < start_lane)
    in_start = (key_lane >= start_lane) & (key_lane < start_lane + SLOPE_TERMS)
    pos_terms = jnp.where(((key_lane - pos_lane) & 1) == 0, (pos // SLC_LEN) * SLC_LEN, pos % SLC_LEN)
    pos_terms = jnp.where(in_pos, pos_terms, 0).astype(F32)

    def key_cols(kt, selected):
        start = kt * tk
        extra = pos_terms + jnp.where(in_start, start.astype(F32), 0.0)
        if selected:
            extra = extra + jnp.where(key_lane == ((start + pos) >> (SLC_LEN.bit_length() - 1)), 1.0, 0.0)
        return extra.astype(BF16)

    row_head = lax.broadcasted_iota(jnp.int32, (cols, NSA_DIM), 0) >> (tq.bit_length() - 1)
    row_lane = lax.broadcasted_iota(jnp.int32, (cols, NSA_DIM), 1)
    slope_term = jnp.where(row_lane < start_lane, (row_lane - pos_lane) >> 1, row_lane - start_lane)

    def alibi_slope(head):
        return LOG2E * jnp.exp((head + 1).astype(F32) * (-0.5 * math.log(2.0)))

    for g in range(groups):
        q4 = jnp.concatenate([q_ref[:, (g * rep + r) * NSA_DIM:(g * rep + r + 1) * NSA_DIM]
                              for r in range(rep)], axis=0)
        q4 = (q4.astype(F32) * (LOG2E * NSA_DIM ** -0.5)).astype(BF16)
        slope = alibi_slope(g * rep + lane_head)

        st = lax.dot_general(kc_ref[g], q4, _NT, preferred_element_type=F32)
        st = jnp.where(c_mask, st + slope * c_end.astype(F32), MASKED)
        e = jnp.exp2(st - jnp.max(st, axis=0, keepdims=True))
        p_cmp = jnp.where(c_mask, e * (1.0 / jnp.sum(e, axis=0, keepdims=True)), 0.0)
        o_cmp = jnp.dot(vct_ref[g], p_cmp.astype(BF16), preferred_element_type=F32)
        out_ref[g] = gate_row(g, 0) * o_cmp

        p_sum = p_cmp[:, 0:tq]
        for r in range(1, rep):
            p_sum = p_sum + p_cmp[:, r * tq:(r + 1) * tq]
        p_hi = p_sum.astype(BF16)
        p_lo = (p_sum - p_hi.astype(F32)).astype(BF16)
        imp = (jnp.dot(overlap, p_hi, preferred_element_type=F32)
               + jnp.dot(overlap, p_lo, preferred_element_type=F32))
        imp = jnp.where(forced, FORCE, jnp.where(future, -FORCE, imp))
        rank = jnp.zeros((n_blocks, tq), F32)
        for j in range(n_blocks):
            row = imp[j:j + 1, :]
            before = (row > imp) | ((row == imp) & (blk > j))
            rank = rank + jnp.where(before, 1.0, 0.0)
        sel_bias = jnp.where(rank < SLC_TOPK, 0.0, MASKED)

        sel_t = jnp.concatenate([sel_bias, jnp.zeros((NSA_DIM - n_blocks, tq), F32)], axis=0).T
        extra = jnp.concatenate([sel_t] * rep, axis=0)
        rest = alibi_slope(g * rep + row_head)
        for n in range(SLOPE_TERMS):
            term = rest.astype(BF16).astype(F32)
            rest = rest - term
            extra = jnp.where((row_lane >= pos_lane) & (slope_term == n), term, extra)
        q4_ref[g] = jnp.concatenate([q4, extra.astype(BF16)], axis=1)

    def scores(g, k, extra):
        k = jnp.concatenate([k, extra], axis=1)
        return lax.dot_general(k, q4_ref[g], _NT, preferred_element_type=F32)

    def update(branch, g, st, vt, keep, first, after):
        if after is not None:
            vt = vt + (0.0 * after[0:1, 0:tk]).astype(BF16)
        if keep is not None:
            st = jnp.where(keep, st, MASKED)
        m_tile = jnp.max(st, axis=0, keepdims=True)
        if first:
            p = jnp.exp2(st - m_tile)
            m_ref[branch, g] = m_tile
            l_ref[branch, g] = jnp.sum(p, axis=0, keepdims=True)
            acc_ref[branch, g] = jnp.dot(vt, p.astype(BF16), preferred_element_type=F32)
        else:
            m_old = m_ref[branch, g]
            m_new = jnp.maximum(m_old, m_tile)
            a = jnp.exp2(m_old - m_new)
            p = jnp.exp2(st - m_new)
            m_ref[branch, g] = m_new
            l_ref[branch, g] = a * l_ref[branch, g] + jnp.sum(p, axis=0, keepdims=True)
            acc_ref[branch, g] = a * acc_ref[branch, g] + jnp.dot(vt, p.astype(BF16), preferred_element_type=F32)

    def slc_chains(kt):
        rows = pl.ds(pl.multiple_of(kt * tk, tk), tk)
        extra = key_cols(kt, True)
        return [(slc, g, ks_ref[rows, gcols(g)], extra, vst_ref[g, kt]) for g in range(groups)]

    def win_chains(kt):
        rows = pl.ds(pl.multiple_of(kt * tk, tk), tk)
        extra = key_cols(kt, False)
        return [(win, g, kw_ref[rows, gcols(g)], extra, vwt_ref[g, kt]) for g in range(groups)]

    def run(chains, keep, first):
        sts = [scores(g, k, extra) for (_, g, k, extra, _) in chains]
        for n, (branch, g, _, _, vt) in enumerate(chains):
            update(branch, g, sts[n], vt, keep, first, sts[n + 1] if n + 1 < len(sts) else None)

    run(slc_chains(i) + win_chains(i), causal, True)

    def slc_body(kt, carry):
        run(slc_chains(kt), None, False)
        return carry

    lax.fori_loop(0, i, slc_body, 0)

    n_back = WIN_LEN // tk

    @pl.when(i >= n_back)
    def _():
        run(win_chains(i - n_back), sub > t_loc, False)

    def win_body(kt, carry):
        run(win_chains(kt), None, False)
        return carry

    lax.fori_loop(jnp.maximum(i - (n_back - 1), 0), i, win_body, 0)

    for g in range(groups):
        o = out_ref[g]
        for branch, state in ((1, slc), (2, win)):
            o = o + (gate_row(g, branch) * (1.0 / l_ref[state, g])) * acc_ref[state, g]
        for r in range(rep):
            h = g * rep + r
            o_ref[:, h * NSA_DIM:(h + 1) * NSA_DIM] = o[:, r * tq:(r + 1) * tq].T.astype(o_ref.dtype)


def _nsa(proj, gates, kc, vc, w_out, w_ff2, layer, batch, seq):
    nq = seq // NSA_TQ
    step = lambda b, i: b * nq + i
    wo_in, wo_out, wo_shape = _cast_slab_specs(w_out, layer, batch * nq, step)
    w2_in, w2_out, w2_shape = _cast_slab_specs(w_ff2, layer, batch * nq, step, FFN_W2_SPLIT,
                                               w_ff2.shape[2] // FFN_W2_SPLIT - 1)
    groups, cols = NSA_KV_GROUPS, NSA_REP * NSA_TQ
    n_tiles = seq // NSA_TK
    kvspec = lambda off: pl.BlockSpec((seq, NSA_KV_WIDTH), lambda b, i: (b, off // NSA_KV_WIDTH))
    cspec = lambda a: pl.BlockSpec((None,) + a.shape[1:], lambda b, i: (b, 0, 0, 0))
    return pl.pallas_call(
        _nsa_kernel,
        grid=(batch, nq),
        in_specs=[pl.BlockSpec((NSA_TQ, NSA_WIDTH), lambda b, i: (b * nq + i, OFF_NQ // NSA_WIDTH)),
                  cspec(kc), cspec(vc),
                  kvspec(OFF_KS), kvspec(OFF_VS), kvspec(OFF_KW), kvspec(OFF_VW),
                  pl.BlockSpec((NSA_TQ, LANES), lambda b, i: (b * nq + i, 0)), wo_in, w2_in],
        out_specs=[pl.BlockSpec((NSA_TQ, NSA_WIDTH), lambda b, i: (b * nq + i, 0)), wo_out, w2_out],
        out_shape=[jax.ShapeDtypeStruct((batch * seq, NSA_WIDTH), BF16), wo_shape, w2_shape],
        scratch_shapes=[pltpu.VMEM((groups, cols, 2 * NSA_DIM), BF16),
                        pltpu.VMEM((groups, NSA_DIM, seq // CMP_STRIDE), BF16),
                        pltpu.VMEM((groups, n_tiles, NSA_DIM, NSA_TK), BF16),
                        pltpu.VMEM((groups, n_tiles, NSA_DIM, NSA_TK), BF16),
                        pltpu.VMEM((LANES, NSA_TQ), F32),
                        pltpu.VMEM((2, groups, 1, cols), F32),
                        pltpu.VMEM((2, groups, 1, cols), F32),
                        pltpu.VMEM((2, groups, NSA_DIM, cols), F32),
                        pltpu.VMEM((groups, NSA_DIM, cols), F32)],
        compiler_params=_params(("parallel", "arbitrary")),
        name="nsa",
    )(proj, kc, vc, proj, proj, proj, proj, gates, w_out, w_ff2)


def _outproj_kernel(ret_ref, nsa_ref, wr_ref, wn_ref, x_ref, g_ref, b_ref, o_ref, mu_ref, rstd_ref):
    j = pl.program_id(1)
    nj = pl.num_programs(1)
    tn = x_ref.shape[1]
    y = (jnp.dot(ret_ref[...], wr_ref[...], preferred_element_type=F32)
         + jnp.dot(nsa_ref[...], wn_ref[...], preferred_element_type=F32))
    y = DN_ALPHA * x_ref[...] + y
    o_ref[:, pl.ds(pl.multiple_of(j * tn, tn), tn)] = y

    @pl.when(j == nj - 1)
    def _():
        _layer_norm_rows(o_ref, lambda rows, cols: o_ref[rows, cols], g_ref, b_ref, mu_ref, rstd_ref)


def _outproj(ret, nsa, w_out, x, g, b, tm, tn):
    m, d = x.shape
    vec = pl.BlockSpec((1, d), lambda i, j: (0, 0))
    return pl.pallas_call(
        _outproj_kernel,
        grid=(m // tm, d // tn),
        in_specs=[pl.BlockSpec((tm, RET_WIDTH), lambda i, j: (i, 0)),
                  pl.BlockSpec((tm, NSA_WIDTH), lambda i, j: (i, 0)),
                  pl.BlockSpec((RET_WIDTH, tn), lambda i, j: (0, j)),
                  pl.BlockSpec((NSA_WIDTH, tn), lambda i, j: (RET_WIDTH // NSA_WIDTH, j)),
                  pl.BlockSpec((tm, tn), lambda i, j: (i, j)),
                  vec, vec],
        out_specs=pl.BlockSpec((tm, d), lambda i, j: (i, 0)),
        out_shape=jax.ShapeDtypeStruct((m, d), F32),
        scratch_shapes=_row_stats(tm),
        compiler_params=_params(("parallel", "arbitrary")),
        name="outproj_ln",
    )(ret, nsa, w_out, w_out, x, g, b)


def _xattn_kernel(h_ref, kv_ref, wq_ref, wo_ref, g_ref, b_ref, w2_ref, o_ref, w2b_ref, mu_ref, rstd_ref):
    w2b_ref[...] = w2_ref[...].astype(BF16)

    q = jnp.dot(h_ref[...].astype(BF16), wq_ref[...], preferred_element_type=F32).astype(BF16)
    outs = []
    for hd in range(XA_HEADS):
        sl = slice(hd * XA_DIM, (hd + 1) * XA_DIM)
        k = kv_ref[:, sl]
        v = kv_ref[:, XA_WIDTH + hd * XA_DIM:XA_WIDTH + (hd + 1) * XA_DIM]
        s = lax.dot_general(q[:, sl], k, _NT, preferred_element_type=F32) * (XA_DIM ** -0.5)
        e = jnp.exp(s - jnp.max(s, axis=-1, keepdims=True))
        p = e / jnp.sum(e, axis=-1, keepdims=True)
        outs.append(jnp.dot(p.astype(BF16), v, preferred_element_type=F32))
    o = jnp.concatenate(outs, axis=-1).astype(BF16)
    o_ref[...] = jnp.dot(o, wo_ref[...], preferred_element_type=F32)
    _layer_norm_rows(o_ref, lambda rows, cols: DN_ALPHA * h_ref[rows, cols] + o_ref[rows, cols], g_ref, b_ref,
                     mu_ref, rstd_ref)


def _xattn(h, kv, wq, wo, g, b, w_ff2, layer, seq, mem_len, tm):
    m, d = h.shape
    per_batch = seq // tm
    steps = m // tm
    full = lambda a: pl.BlockSpec(a.shape, lambda i: (0, 0), pipeline_mode=pl.Buffered(1))
    w2_in, w2_out, w2_shape = _cast_slab_specs(w_ff2, layer, steps, lambda i: i, d - FFN_W2_SPLIT, 0)
    return pl.pallas_call(
        _xattn_kernel,
        grid=(steps,),
        in_specs=[pl.BlockSpec((tm, d), lambda i: (i, 0)),
                  pl.BlockSpec((mem_len, 2 * XA_WIDTH), lambda i: (i // per_batch, 0)),
                  full(wq), full(wo), full(g), full(b), w2_in],
        out_specs=[pl.BlockSpec((tm, d), lambda i: (i, 0)), w2_out],
        out_shape=[jax.ShapeDtypeStruct((m, d), F32), w2_shape],
        scratch_shapes=_row_stats(tm),
        compiler_params=_params(("parallel",)),
        name="xattn_ln",
    )(h, kv, wq, wo, g, b, w_ff2)


def _ffn_kernel(h_ref, w1_ref, w2a_ref, w2b_ref, g_ref, b_ref, o_ref, hb_ref, mu_ref, rstd_ref):
    f = pl.program_id(1)
    nf = pl.num_programs(1)

    @pl.when(f == 0)
    def _():
        hb_ref[...] = h_ref[...].astype(BF16)
        o_ref[...] = jnp.zeros_like(o_ref)

    u = jnp.maximum(jnp.dot(hb_ref[...], w1_ref[...], preferred_element_type=F32), 0.0)
    u = (u * u).astype(BF16)
    split = w2a_ref.shape[1]
    for c in range(o_ref.shape[1] // FFN_TN):
        cols = slice(c * FFN_TN, (c + 1) * FFN_TN)
        w2_ref, first = (w2a_ref, 0) if c * FFN_TN < split else (w2b_ref, split)
        w2 = w2_ref[:, c * FFN_TN - first:(c + 1) * FFN_TN - first]
        o_ref[:, cols] += jnp.dot(u, w2, preferred_element_type=F32)

    @pl.when(f == nf - 1)
    def _():
        _layer_norm_rows(o_ref, lambda rows, cols: DN_ALPHA * h_ref[rows, cols] + o_ref[rows, cols], g_ref, b_ref,
                         mu_ref, rstd_ref)


def _ffn(h, w1, w2a, w2b, g, b, tm, tf):
    m, d = h.shape
    d_ff = w1.shape[1]
    vec = pl.BlockSpec((1, d), lambda i, f: (0, 0))
    return pl.pallas_call(
        _ffn_kernel,
        grid=(m // tm, d_ff // tf),
        in_specs=[pl.BlockSpec((tm, d), lambda i, f: (i, 0)),
                  pl.BlockSpec((d, tf), lambda i, f: (0, f)),
                  pl.BlockSpec((tf, w2a.shape[1]), lambda i, f: (f, 0)),
                  pl.BlockSpec((tf, w2b.shape[1]), lambda i, f: (f, 0)),
                  vec, vec],
        out_specs=pl.BlockSpec((tm, d), lambda i, f: (i, 0)),
        out_shape=jax.ShapeDtypeStruct((m, d), F32),
        scratch_shapes=[pltpu.VMEM((tm, d), BF16)] + _row_stats(tm),
        compiler_params=_params(("parallel", "arbitrary")),
        name="ffn_ln",
    )(h, w1, w2a, w2b, g, b)


def kernel(x, mem, w_in, w_out, cmp_pos, cmp_w1, cmp_w2, xa_wq, xa_wkv, xa_wo,
           w_ff1, w_ff2, ln_g, ln_b):
    batch, seq, d = x.shape
    mem_len = mem.shape[1]
    h = x.reshape(batch * seq, d)
    mem2 = mem.reshape(batch * mem_len, d).astype(BF16)
    for l in range(DEPTH):
        w_in_t = jnp.swapaxes(w_in[l], 0, 1).astype(BF16)
        wg_t = jnp.pad(w_in_t[OFF_GATES:], ((0, LANES - GATE_WIDTH), (0, 0)))
        hb, gates = _cast_gates(h, wg_t, CAST_TM)
        proj, w_ff1_b = _inproj(hb, w_in_t, OFF_GATES, w_ff1, l, INPROJ_TM, INPROJ_TN)
        ret = _retention(proj, batch, seq)
        kc, vc = _compress(proj, cmp_pos[l], cmp_w1[l].astype(BF16), cmp_w2[l].astype(BF16),
                           batch, seq)
        sparse, w_out_b, w_ff2_tail = _nsa(proj, gates, kc, vc, w_out, w_ff2, l, batch, seq)
        vecs = lambda a, k: a[l, k].reshape(1, d)
        h = _outproj(ret, sparse, w_out_b, h, vecs(ln_g, 0), vecs(ln_b, 0), OUTPROJ_TM, OUTPROJ_TN)
        kv = _matmul(mem2, xa_wkv[l].astype(BF16), 2 * XA_WIDTH, KV_TM, KV_TN, BF16)
        h, w_ff2_head = _xattn(h, kv, xa_wq[l].astype(BF16), xa_wo[l].astype(BF16),
                               vecs(ln_g, 1), vecs(ln_b, 1), w_ff2, l, seq, mem_len, XATTN_TM)
        h = _ffn(h, w_ff1_b, w_ff2_head, w_ff2_tail, vecs(ln_g, 2), vecs(ln_b, 2), FFN_TM, FFN_TF)
    return h.reshape(batch, seq, d)
```

```python
import math

import jax
import jax.numpy as jnp
from jax import lax
from jax.experimental import pallas as pl
from jax.experimental.pallas import tpu as pltpu

F32 = jnp.float32
BF16 = jnp.bfloat16

RET_HEADS = 8
RET_DIM = 256
RET_WIDTH = RET_HEADS * RET_DIM
RET_CHUNK = 128
NSA_HEADS = 16
NSA_KV_GROUPS = 4
NSA_REP = NSA_HEADS // NSA_KV_GROUPS
NSA_DIM = 128
NSA_WIDTH = NSA_HEADS * NSA_DIM
NSA_KV_WIDTH = NSA_KV_GROUPS * NSA_DIM
CMP_LEN = 32
CMP_STRIDE = 16
SLC_LEN = 64
SLC_TOPK = 16
WIN_LEN = 512
GATE_WIDTH = 3 * NSA_HEADS
OFF_RQ, OFF_RK, OFF_RV, OFF_RG = 0, RET_WIDTH, 2 * RET_WIDTH, 3 * RET_WIDTH
OFF_NQ = 4 * RET_WIDTH
OFF_KC = OFF_NQ + NSA_WIDTH
OFF_VC = OFF_KC + NSA_KV_WIDTH
OFF_KS = OFF_VC + NSA_KV_WIDTH
OFF_VS = OFF_KS + NSA_KV_WIDTH
OFF_KW = OFF_VS + NSA_KV_WIDTH
OFF_VW = OFF_KW + NSA_KV_WIDTH
OFF_GATES = OFF_VW + NSA_KV_WIDTH
XA_HEADS = 4
XA_DIM = 128
XA_WIDTH = XA_HEADS * XA_DIM
LN_EPS = 1e-5
DEPTH = 1
DN_ALPHA = (2.0 * DEPTH) ** 0.25
NEG_INF = -1e30
MASKED = 2.0 * NEG_INF
FORCE = 1e9
LOG2E = math.log2(math.e)
SLOPE_TERMS = 3

LANES = 128
VMEM_LIMIT = 56 * 1024 * 1024
CAST_TM = 512
INPROJ_TM, INPROJ_TN = 1024, 1024
INPROJ_SIDE_STEPS = 8
RET_CHUNKS_PER_STEP = 4
NSA_TQ = 256
NSA_TK = 256
OUTPROJ_TM, OUTPROJ_TN = 512, 1024
KV_TM, KV_TN = 512, 1024
XATTN_TM = 256
FFN_TM, FFN_TF = 512, 512
FFN_TN = 1024
FFN_W2_SPLIT = 1024
LN_ROWS = 8
LN_STATS_UNROLL = 16
LN_APPLY_UNROLL = 4

_NT = (((1,), (1,)), ((), ()))
_TN = (((0,), (0,)), ((), ()))


def _params(sem):
    return pltpu.CompilerParams(dimension_semantics=sem, vmem_limit_bytes=VMEM_LIMIT)


def _cast_slab_specs(w, layer, steps, step_index, cols=None, col_block=0):
    rows, cols = w.shape[1] // steps, cols or w.shape[2]
    in_spec = pl.BlockSpec((None, rows, cols), lambda *ids: (layer, step_index(*ids), col_block))
    out_spec = pl.BlockSpec((rows, cols), lambda *ids: (step_index(*ids), 0))
    return in_spec, out_spec, jax.ShapeDtypeStruct((w.shape[1], cols), BF16)


def _row_stats(tm):
    return [pltpu.VMEM((tm, LANES), F32), pltpu.VMEM((tm, LANES), F32)]


def _layer_norm_rows(o_ref, g_ref, b_ref, mu_ref, rstd_ref):
    tm, d = o_ref.shape
    n = tm // LN_ROWS

    def stats(c, carry):
        rows = pl.ds(pl.multiple_of(c * LN_ROWS, LN_ROWS), LN_ROWS)
        y = o_ref[rows, :]
        mu = jnp.mean(y, axis=-1, keepdims=True)
        dev = y - mu
        rstd = lax.rsqrt(jnp.mean(dev * dev, axis=-1, keepdims=True) + LN_EPS)
        mu_ref[rows, :] = jnp.broadcast_to(mu, (LN_ROWS, LANES))
        rstd_ref[rows, :] = jnp.broadcast_to(rstd, (LN_ROWS, LANES))
        return carry

    def apply(c, carry):
        rows = pl.ds(pl.multiple_of(c * LN_ROWS, LN_ROWS), LN_ROWS)
        mu, rstd = mu_ref[rows, :], rstd_ref[rows, :]
        for j in range(d // LANES):
            cols = slice(j * LANES, (j + 1) * LANES)
            o_ref[rows, cols] = (o_ref[rows, cols] - mu) * rstd * g_ref[:, cols] + b_ref[:, cols]
        return carry

    lax.fori_loop(0, n, stats, 0, unroll=min(LN_STATS_UNROLL, n))
    lax.fori_loop(0, n, apply, 0, unroll=LN_APPLY_UNROLL)


def _mm_kernel(x_ref, w_ref, o_ref):
    o_ref[...] = jnp.dot(x_ref[...], w_ref[...], preferred_element_type=F32).astype(o_ref.dtype)


def _matmul(x, w, n_out, tm, tn, out_dtype):
    m, k = x.shape
    return pl.pallas_call(
        _mm_kernel,
        grid=(m // tm, n_out // tn),
        in_specs=[pl.BlockSpec((tm, k), lambda i, j: (i, 0)),
                  pl.BlockSpec((k, tn), lambda i, j: (0, j))],
        out_specs=pl.BlockSpec((tm, tn), lambda i, j: (i, j)),
        out_shape=jax.ShapeDtypeStruct((m, n_out), out_dtype),
        compiler_params=_params(("parallel", "parallel")),
        name="matmul",
    )(x, w)


def _cast_gates_kernel(x_ref, wgt_ref, xb_ref, g_ref):
    xb = x_ref[...].astype(BF16)
    xb_ref[...] = xb
    g_ref[...] = lax.dot_general(xb, wgt_ref[...], _NT, preferred_element_type=F32)


def _cast_gates(x, wg_t, tm):
    m, k = x.shape
    return pl.pallas_call(
        _cast_gates_kernel,
        grid=(m // tm,),
        in_specs=[pl.BlockSpec((tm, k), lambda i: (i, 0)),
                  pl.BlockSpec((LANES, k), lambda i: (0, 0), pipeline_mode=pl.Buffered(1))],
        out_specs=[pl.BlockSpec((tm, k), lambda i: (i, 0)), pl.BlockSpec((tm, LANES), lambda i: (i, 0))],
        out_shape=[jax.ShapeDtypeStruct((m, k), BF16), jax.ShapeDtypeStruct((m, LANES), F32)],
        compiler_params=_params(("parallel",)),
        name="cast_gates",
    )(x, wg_t)


def _inproj_kernel(x_ref, w_ref, w1_ref, o_ref, w1b_ref):
    @pl.when(pl.program_id(1) < INPROJ_SIDE_STEPS)
    def _():
        w1b_ref[...] = w1_ref[...].astype(BF16)

    o_ref[...] = lax.dot_general(x_ref[...], w_ref[...], _NT, preferred_element_type=F32).astype(o_ref.dtype)


def _inproj(x, w_t, n_out, w_ff1, layer, tm, tn):
    m, k = x.shape
    side = INPROJ_SIDE_STEPS
    slab = lambda i, j: i * side + jnp.minimum(j, side - 1)
    w1_in, w1_out, w1_shape = _cast_slab_specs(w_ff1, layer, (m // tm) * side, slab)
    return pl.pallas_call(
        _inproj_kernel,
        grid=(m // tm, n_out // tn),
        in_specs=[pl.BlockSpec((tm, k), lambda i, j: (i, 0)),
                  pl.BlockSpec((tn, k), lambda i, j: (j, 0)), w1_in],
        out_specs=[pl.BlockSpec((tm, tn), lambda i, j: (i, j)), w1_out],
        out_shape=[jax.ShapeDtypeStruct((m, n_out), BF16), w1_shape],
        compiler_params=_params(("parallel", "arbitrary")),
        name="inproj",
    )(x, w_t, w_ff1)


def _retention_kernel(q_ref, k_ref, v_ref, g_ref, o_ref, state_ref, decay_ref, ktail_ref, qhead_ref):
    c = RET_CHUNK

    @pl.when(pl.program_id(1) == 0)
    def _():
        state_ref[...] = jnp.zeros_like(state_ref)
        diff = (lax.broadcasted_iota(jnp.int32, (c, c), 0)
                - lax.broadcasted_iota(jnp.int32, (c, c), 1)).astype(F32)
        pos = lax.broadcasted_iota(jnp.int32, (c, RET_DIM), 0).astype(F32)
        for h in range(RET_HEADS):
            log_g = math.log1p(-(2.0 ** (-5.0 - h)))
            decay_ref[h] = jnp.where(diff >= 0, jnp.exp(log_g * jnp.maximum(diff, 0.0)), 0.0)
            ktail_ref[h] = jnp.exp(log_g * (c - 1.0 - pos))
            qhead_ref[h] = jnp.exp(log_g * (pos + 1.0))

    for chunk, h in [(ch, hd) for ch in range(q_ref.shape[0] // c) for hd in range(RET_HEADS)]:
        log_g = math.log1p(-(2.0 ** (-5.0 - h)))
        sl = (slice(chunk * c, (chunk + 1) * c), slice(h * RET_DIM, (h + 1) * RET_DIM))
        q = q_ref[sl].astype(F32)
        k = k_ref[sl].astype(F32) * (RET_DIM ** -0.5)
        v = v_ref[sl]
        scores = lax.dot_general(q.astype(BF16), k.astype(BF16), _NT,
                                 preferred_element_type=F32) * decay_ref[h]
        o_inner = jnp.dot(scores.astype(BF16), v, preferred_element_type=F32)
        k_tail = k * ktail_ref[h]
        kv = lax.dot_general(k_tail.astype(BF16), v, _TN, preferred_element_type=F32)
        q_head = q * qhead_ref[h]
        state = state_ref[h]
        o = o_inner + jnp.dot(q_head.astype(BF16), state.astype(BF16), preferred_element_type=F32)
        state_ref[h] = math.exp(log_g * c) * state + kv
        mu = jnp.mean(o, axis=-1, keepdims=True)
        d = o - mu
        var = jnp.mean(d * d, axis=-1, keepdims=True)
        o = d * lax.rsqrt(var + LN_EPS)
        gate = g_ref[sl].astype(F32)
        o_ref[sl] = (o * (gate * jax.nn.sigmoid(gate))).astype(o_ref.dtype)


def _retention(proj, batch, seq):
    rows = RET_CHUNKS_PER_STEP * RET_CHUNK
    n = seq // rows
    blk = lambda col: pl.BlockSpec((rows, RET_WIDTH), lambda b, i: (b * n + i, col))
    return pl.pallas_call(
        _retention_kernel,
        grid=(batch, n),
        in_specs=[blk(OFF_RQ // RET_WIDTH), blk(OFF_RK // RET_WIDTH),
                  blk(OFF_RV // RET_WIDTH), blk(OFF_RG // RET_WIDTH)],
        out_specs=pl.BlockSpec((rows, RET_WIDTH), lambda b, i: (b * n + i, 0)),
        out_shape=jax.ShapeDtypeStruct((batch * seq, RET_WIDTH), BF16),
        scratch_shapes=[pltpu.VMEM((RET_HEADS, RET_DIM, RET_DIM), F32),
                        pltpu.VMEM((RET_HEADS, RET_CHUNK, RET_CHUNK), F32),
                        pltpu.VMEM((RET_HEADS, RET_CHUNK, RET_DIM), F32),
                        pltpu.VMEM((RET_HEADS, RET_CHUNK, RET_DIM), F32)],
        compiler_params=_params(("parallel", "arbitrary")),
        name="retention",
    )(proj, proj, proj, proj)


def _gelu_tanh(x):
    return 0.5 * x * (1.0 + jnp.tanh(math.sqrt(2.0 / math.pi) * (x + 0.044715 * (x * x * x))))


def _compress_kernel(zk_ref, zv_ref, pos_ref, w1_ref, w2_ref, kc_ref, vc_ref, zf_ref):
    nsub = zf_ref.shape[0] // CMP_STRIDE
    for kv, (z_ref, o_ref) in enumerate(((zk_ref, kc_ref), (zv_ref, vc_ref))):
        zf_ref[...] = z_ref[...].astype(F32)
        hid_a = jnp.zeros((nsub, LANES), F32)
        hid_b = jnp.zeros((nsub, LANES), F32)
        for i in range(CMP_STRIDE):
            rows = zf_ref[pl.ds(i, nsub, stride=CMP_STRIDE), :]
            xa = (rows + pos_ref[kv, i:i + 1, :]).astype(BF16)
            xb = (rows + pos_ref[kv, CMP_STRIDE + i:CMP_STRIDE + i + 1, :]).astype(BF16)
            wa = w1_ref[kv, i * NSA_DIM:(i + 1) * NSA_DIM, :]
            wb = w1_ref[kv, (CMP_STRIDE + i) * NSA_DIM:(CMP_STRIDE + i + 1) * NSA_DIM, :]
            hid_a = hid_a + jnp.dot(xa, wa, preferred_element_type=F32)
            hid_b = hid_b + jnp.dot(xb, wb, preferred_element_type=F32)
        hid = _gelu_tanh(hid_a + pltpu.roll(hid_b, nsub - 1, 0))
        o_ref[...] = jnp.dot(hid.astype(BF16), w2_ref[kv], preferred_element_type=F32).astype(o_ref.dtype)


def _compress(proj, cmp_pos, cmp_w1, cmp_w2, batch, seq):
    nsub = seq // CMP_STRIDE
    zspec = lambda col0: pl.BlockSpec((seq, NSA_DIM), lambda b, g: (b, col0 + g))
    full = lambda a: pl.BlockSpec(a.shape, lambda b, g: (0,) * a.ndim)
    ospec = pl.BlockSpec((None, None, nsub, NSA_DIM), lambda b, g: (b, g, 0, 0))
    oshape = jax.ShapeDtypeStruct((batch, NSA_KV_GROUPS, nsub, NSA_DIM), BF16)
    return pl.pallas_call(
        _compress_kernel,
        grid=(batch, NSA_KV_GROUPS),
        in_specs=[zspec(OFF_KC // NSA_DIM), zspec(OFF_VC // NSA_DIM),
                  full(cmp_pos), full(cmp_w1), full(cmp_w2)],
        out_specs=[ospec, ospec],
        out_shape=[oshape, oshape],
        scratch_shapes=[pltpu.VMEM((seq, NSA_DIM), F32)],
        compiler_params=_params(("parallel", "parallel")),
        name="compress",
    )(proj, proj, cmp_pos, cmp_w1, cmp_w2)


def _nsa_kernel(q_ref, kc_ref, vc_ref, ks_ref, vs_ref, kw_ref, vw_ref, gate_ref, wo_ref, w2_ref,
                o_ref, wob_ref, w2b_ref,
                q4_ref, vct_ref, vst_ref, vwt_ref, sig_ref, m_ref, l_ref, acc_ref, out_ref):
    tq, tk, rep, groups = NSA_TQ, NSA_TK, NSA_REP, NSA_KV_GROUPS
    cols = rep * tq
    n_tiles = ks_ref.shape[0] // tk
    n_blocks = ks_ref.shape[0] // SLC_LEN
    n_cmp = kc_ref.shape[1]
    slc, win = 0, 1
    i = pl.program_id(1)
    t0 = i * tq
    gcols = lambda g: slice(g * NSA_DIM, (g + 1) * NSA_DIM)

    wob_ref[...] = wo_ref[...].astype(BF16)
    w2b_ref[...] = w2_ref[...].astype(BF16)

    @pl.when(i == 0)
    def _():
        for g in range(groups):
            vct_ref[g] = vc_ref[g].astype(F32).T.astype(BF16)

        def transpose_tile(t, carry):
            rows = pl.ds(pl.multiple_of(t * tk, tk), tk)
            for g in range(groups):
                vst_ref[g, t] = vs_ref[rows, gcols(g)].astype(F32).T.astype(BF16)
                vwt_ref[g, t] = vw_ref[rows, gcols(g)].astype(F32).T.astype(BF16)
            return carry

        lax.fori_loop(0, n_tiles, transpose_tile, 0)

    sub = lax.broadcasted_iota(jnp.int32, (tk, cols), 0)
    t_loc = lax.broadcasted_iota(jnp.int32, (tk, cols), 1) & (tq - 1)
    lane_head = lax.broadcasted_iota(jnp.int32, (1, cols), 1) >> (tq.bit_length() - 1)
    causal = sub <= t_loc
    c_end = lax.broadcasted_iota(jnp.int32, (n_cmp, cols), 0) * CMP_STRIDE + (CMP_LEN - 1)
    c_mask = c_end <= t0 + (lax.broadcasted_iota(jnp.int32, (n_cmp, cols), 1) & (tq - 1))
    s_idx = lax.broadcasted_iota(jnp.int32, (n_blocks, n_cmp), 0)
    c_idx = lax.broadcasted_iota(jnp.int32, (n_blocks, n_cmp), 1)
    overlap = jnp.maximum(jnp.minimum(c_idx * CMP_STRIDE + CMP_LEN, s_idx * SLC_LEN + SLC_LEN)
                          - jnp.maximum(c_idx * CMP_STRIDE, s_idx * SLC_LEN), 0)
    overlap = (overlap.astype(F32) * (1.0 / CMP_LEN)).astype(BF16)
    blk = lax.broadcasted_iota(jnp.int32, (n_blocks, tq), 0)
    cur = (t0 + lax.broadcasted_iota(jnp.int32, (n_blocks, tq), 1)) >> (SLC_LEN.bit_length() - 1)
    forced = (blk == 0) | (blk == cur) | (blk == cur - 1)
    future = blk > cur

    sig_ref[...] = jax.nn.sigmoid(gate_ref[...]).T

    def gate_row(g, branch):
        first = branch * NSA_HEADS + g * rep
        return jnp.concatenate([sig_ref[first + r:first + r + 1, :] for r in range(rep)], axis=1)

    pos_lane, start_lane = n_blocks, n_blocks + 2 * SLOPE_TERMS
    pos = lax.broadcasted_iota(jnp.int32, (tk, NSA_DIM), 0)
    key_lane = lax.broadcasted_iota(jnp.int32, (tk, NSA_DIM), 1)
    in_pos = (key_lane >= pos_lane) & (key_lane < start_lane)
    in_start = (key_lane >= start_lane) & (key_lane < start_lane + SLOPE_TERMS)
    pos_terms = jnp.where(((key_lane - pos_lane) & 1) == 0, (pos // SLC_LEN) * SLC_LEN, pos % SLC_LEN)
    pos_terms = jnp.where(in_pos, pos_terms, 0).astype(F32)

    def key_cols(kt, selected):
        start = kt * tk
        extra = pos_terms + jnp.where(in_start, start.astype(F32), 0.0)
        if selected:
            extra = extra + jnp.where(key_lane == ((start + pos) >> (SLC_LEN.bit_length() - 1)), 1.0, 0.0)
        return extra.astype(BF16)

    row_head = lax.broadcasted_iota(jnp.int32, (cols, NSA_DIM), 0) >> (tq.bit_length() - 1)
    row_lane = lax.broadcasted_iota(jnp.int32, (cols, NSA_DIM), 1)
    slope_term = jnp.where(row_lane < start_lane, (row_lane - pos_lane) >> 1, row_lane - start_lane)

    def alibi_slope(head):
        return LOG2E * jnp.exp((head + 1).astype(F32) * (-0.5 * math.log(2.0)))

    for g in range(groups):
        q4 = jnp.concatenate([q_ref[:, (g * rep + r) * NSA_DIM:(g * rep + r + 1) * NSA_DIM]
                              for r in range(rep)], axis=0)
        q4 = (q4.astype(F32) * (LOG2E * NSA_DIM ** -0.5)).astype(BF16)
        slope = alibi_slope(g * rep + lane_head)

        st = lax.dot_general(kc_ref[g], q4, _NT, preferred_element_type=F32)
        st = jnp.where(c_mask, st + slope * c_end.astype(F32), MASKED)
        e = jnp.exp2(st - jnp.max(st, axis=0, keepdims=True))
        p_cmp = jnp.where(c_mask, e * (1.0 / jnp.sum(e, axis=0, keepdims=True)), 0.0)
        o_cmp = jnp.dot(vct_ref[g], p_cmp.astype(BF16), preferred_element_type=F32)
        out_ref[g] = gate_row(g, 0) * o_cmp

        p_sum = p_cmp[:, 0:tq]
        for r in range(1, rep):
            p_sum = p_sum + p_cmp[:, r * tq:(r + 1) * tq]
        p_hi = p_sum.astype(BF16)
        p_lo = (p_sum - p_hi.astype(F32)).astype(BF16)
        imp = (jnp.dot(overlap, p_hi, preferred_element_type=F32)
               + jnp.dot(overlap, p_lo, preferred_element_type=F32))
        imp = jnp.where(forced, FORCE, jnp.where(future, -FORCE, imp))
        rank = jnp.zeros((n_blocks, tq), F32)
        for j in range(n_blocks):
            row = imp[j:j + 1, :]
            before = (row > imp) | ((row == imp) & (blk > j))
            rank = rank + jnp.where(before, 1.0, 0.0)
        sel_bias = jnp.where(rank < SLC_TOPK, 0.0, MASKED)

        sel_t = jnp.concatenate([sel_bias, jnp.zeros((NSA_DIM - n_blocks, tq), F32)], axis=0).T
        extra = jnp.concatenate([sel_t] * rep, axis=0)
        rest = alibi_slope(g * rep + row_head)
        for n in range(SLOPE_TERMS):
            term = rest.astype(BF16).astype(F32)
            rest = rest - term
            extra = jnp.where((row_lane >= pos_lane) & (slope_term == n), term, extra)
        q4_ref[g] = jnp.concatenate([q4, extra.astype(BF16)], axis=1)

    def scores(g, k, extra):
        k = jnp.concatenate([k, extra], axis=1)
        return lax.dot_general(k, q4_ref[g], _NT, preferred_element_type=F32)

    def update(branch, g, st, vt, keep, first, after):
        if after is not None:
            vt = vt + (0.0 * after[0:1, 0:tk]).astype(BF16)
        if keep is not None:
            st = jnp.where(keep, st, MASKED)
        m_tile = jnp.max(st, axis=0, keepdims=True)
        if first:
            p = jnp.exp2(st - m_tile)
            m_ref[branch, g] = m_tile
            l_ref[branch, g] = jnp.sum(p, axis=0, keepdims=True)
            acc_ref[branch, g] = jnp.dot(vt, p.astype(BF16), preferred_element_type=F32)
        else:
            m_old = m_ref[branch, g]
            m_new = jnp.maximum(m_old, m_tile)
            a = jnp.exp2(m_old - m_new)
            p = jnp.exp2(st - m_new)
            m_ref[branch, g] = m_new
            l_ref[branch, g] = a * l_ref[branch, g] + jnp.sum(p, axis=0, keepdims=True)
            acc_ref[branch, g] = a * acc_ref[branch, g] + jnp.dot(vt, p.astype(BF16), preferred_element_type=F32)

    def slc_chains(kt):
        rows = pl.ds(pl.multiple_of(kt * tk, tk), tk)
        extra = key_cols(kt, True)
        return [(slc, g, ks_ref[rows, gcols(g)], extra, vst_ref[g, kt]) for g in range(groups)]

    def win_chains(kt):
        rows = pl.ds(pl.multiple_of(kt * tk, tk), tk)
        extra = key_cols(kt, False)
        return [(win, g, kw_ref[rows, gcols(g)], extra, vwt_ref[g, kt]) for g in range(groups)]

    def run(chains, keep, first):
        sts = [scores(g, k, extra) for (_, g, k, extra, _) in chains]
        for n, (branch, g, _, _, vt) in enumerate(chains):
            update(branch, g, sts[n], vt, keep, first, sts[n + 1] if n + 1 < len(sts) else None)

    run(slc_chains(i) + win_chains(i), causal, True)

    def slc_body(kt, carry):
        run(slc_chains(kt), None, False)
        return carry

    lax.fori_loop(0, i, slc_body, 0)

    n_back = WIN_LEN // tk

    @pl.when(i >= n_back)
    def _():
        run(win_chains(i - n_back), sub > t_loc, False)

    def win_body(kt, carry):
        run(win_chains(kt), None, False)
        return carry

    lax.fori_loop(jnp.maximum(i - (n_back - 1), 0), i, win_body, 0)

    for g in range(groups):
        o = out_ref[g]
        for branch, state in ((1, slc), (2, win)):
            o = o + (gate_row(g, branch) * (1.0 / l_ref[state, g])) * acc_ref[state, g]
        for r in range(rep):
            h = g * rep + r
            o_ref[:, h * NSA_DIM:(h + 1) * NSA_DIM] = o[:, r * tq:(r + 1) * tq].T.astype(o_ref.dtype)


def _nsa(proj, gates, kc, vc, w_out, w_ff2, layer, batch, seq):
    nq = seq // NSA_TQ
    step = lambda b, i: b * nq + i
    wo_in, wo_out, wo_shape = _cast_slab_specs(w_out, layer, batch * nq, step)
    w2_in, w2_out, w2_shape = _cast_slab_specs(w_ff2, layer, batch * nq, step, FFN_W2_SPLIT,
                                               w_ff2.shape[2] // FFN_W2_SPLIT - 1)
    groups, cols = NSA_KV_GROUPS, NSA_REP * NSA_TQ
    n_tiles = seq // NSA_TK
    kvspec = lambda off: pl.BlockSpec((seq, NSA_KV_WIDTH), lambda b, i: (b, off // NSA_KV_WIDTH))
    cspec = lambda a: pl.BlockSpec((None,) + a.shape[1:], lambda b, i: (b, 0, 0, 0))
    return pl.pallas_call(
        _nsa_kernel,
        grid=(batch, nq),
        in_specs=[pl.BlockSpec((NSA_TQ, NSA_WIDTH), lambda b, i: (b * nq + i, OFF_NQ // NSA_WIDTH)),
                  cspec(kc), cspec(vc),
                  kvspec(OFF_KS), kvspec(OFF_VS), kvspec(OFF_KW), kvspec(OFF_VW),
                  pl.BlockSpec((NSA_TQ, LANES), lambda b, i: (b * nq + i, 0)), wo_in, w2_in],
        out_specs=[pl.BlockSpec((NSA_TQ, NSA_WIDTH), lambda b, i: (b * nq + i, 0)), wo_out, w2_out],
        out_shape=[jax.ShapeDtypeStruct((batch * seq, NSA_WIDTH), BF16), wo_shape, w2_shape],
        scratch_shapes=[pltpu.VMEM((groups, cols, 2 * NSA_DIM), BF16),
                        pltpu.VMEM((groups, NSA_DIM, seq // CMP_STRIDE), BF16),
                        pltpu.VMEM((groups, n_tiles, NSA_DIM, NSA_TK), BF16),
                        pltpu.VMEM((groups, n_tiles, NSA_DIM, NSA_TK), BF16),
                        pltpu.VMEM((LANES, NSA_TQ), F32),
                        pltpu.VMEM((2, groups, 1, cols), F32),
                        pltpu.VMEM((2, groups, 1, cols), F32),
                        pltpu.VMEM((2, groups, NSA_DIM, cols), F32),
                        pltpu.VMEM((groups, NSA_DIM, cols), F32)],
        compiler_params=_params(("parallel", "arbitrary")),
        name="nsa",
    )(proj, kc, vc, proj, proj, proj, proj, gates, w_out, w_ff2)


def _outproj_kernel(ret_ref, nsa_ref, wr_ref, wn_ref, x_ref, g_ref, b_ref, o_ref, mu_ref, rstd_ref):
    j = pl.program_id(1)
    nj = pl.num_programs(1)
    tn = x_ref.shape[1]
    y = (jnp.dot(ret_ref[...], wr_ref[...], preferred_element_type=F32)
         + jnp.dot(nsa_ref[...], wn_ref[...], preferred_element_type=F32))
    y = DN_ALPHA * x_ref[...] + y
    o_ref[:, pl.ds(pl.multiple_of(j * tn, tn), tn)] = y

    @pl.when(j == nj - 1)
    def _():
        _layer_norm_rows(o_ref, g_ref, b_ref, mu_ref, rstd_ref)


def _outproj(ret, nsa, w_out, x, g, b, tm, tn):
    m, d = x.shape
    vec = pl.BlockSpec((1, d), lambda i, j: (0, 0))
    return pl.pallas_call(
        _outproj_kernel,
        grid=(m // tm, d // tn),
        in_specs=[pl.BlockSpec((tm, RET_WIDTH), lambda i, j: (i, 0)),
                  pl.BlockSpec((tm, NSA_WIDTH), lambda i, j: (i, 0)),
                  pl.BlockSpec((RET_WIDTH, tn), lambda i, j: (0, j)),
                  pl.BlockSpec((NSA_WIDTH, tn), lambda i, j: (RET_WIDTH // NSA_WIDTH, j)),
                  pl.BlockSpec((tm, tn), lambda i, j: (i, j)),
                  vec, vec],
        out_specs=pl.BlockSpec((tm, d), lambda i, j: (i, 0)),
        out_shape=jax.ShapeDtypeStruct((m, d), F32),
        scratch_shapes=_row_stats(tm),
        compiler_params=_params(("parallel", "arbitrary")),
        name="outproj_ln",
    )(ret, nsa, w_out, w_out, x, g, b)


def _xattn_kernel(h_ref, kv_ref, wq_ref, wo_ref, g_ref, b_ref, w2_ref, o_ref, w2b_ref, mu_ref, rstd_ref):
    w2b_ref[...] = w2_ref[...].astype(BF16)

    q = jnp.dot(h_ref[...].astype(BF16), wq_ref[...], preferred_element_type=F32).astype(BF16)
    outs = []
    for hd in range(XA_HEADS):
        sl = slice(hd * XA_DIM, (hd + 1) * XA_DIM)
        k = kv_ref[:, sl]
        v = kv_ref[:, XA_WIDTH + hd * XA_DIM:XA_WIDTH + (hd + 1) * XA_DIM]
        s = lax.dot_general(q[:, sl], k, _NT, preferred_element_type=F32) * (XA_DIM ** -0.5)
        e = jnp.exp(s - jnp.max(s, axis=-1, keepdims=True))
        p = e / jnp.sum(e, axis=-1, keepdims=True)
        outs.append(jnp.dot(p.astype(BF16), v, preferred_element_type=F32))
    o = jnp.concatenate(outs, axis=-1).astype(BF16)
    o_ref[...] = DN_ALPHA * h_ref[...] + jnp.dot(o, wo_ref[...], preferred_element_type=F32)
    _layer_norm_rows(o_ref, g_ref, b_ref, mu_ref, rstd_ref)


def _xattn(h, kv, wq, wo, g, b, w_ff2, layer, seq, mem_len, tm):
    m, d = h.shape
    per_batch = seq // tm
    steps = m // tm
    full = lambda a: pl.BlockSpec(a.shape, lambda i: (0, 0), pipeline_mode=pl.Buffered(1))
    w2_in, w2_out, w2_shape = _cast_slab_specs(w_ff2, layer, steps, lambda i: i, d - FFN_W2_SPLIT, 0)
    return pl.pallas_call(
        _xattn_kernel,
        grid=(steps,),
        in_specs=[pl.BlockSpec((tm, d), lambda i: (i, 0)),
                  pl.BlockSpec((mem_len, 2 * XA_WIDTH), lambda i: (i // per_batch, 0)),
                  full(wq), full(wo), full(g), full(b), w2_in],
        out_specs=[pl.BlockSpec((tm, d), lambda i: (i, 0)), w2_out],
        out_shape=[jax.ShapeDtypeStruct((m, d), F32), w2_shape],
        scratch_shapes=_row_stats(tm),
        compiler_params=_params(("parallel",)),
        name="xattn_ln",
    )(h, kv, wq, wo, g, b, w_ff2)


def _ffn_kernel(h_ref, w1_ref, w2a_ref, w2b_ref, g_ref, b_ref, o_ref, hb_ref, mu_ref, rstd_ref):
    f = pl.program_id(1)
    nf = pl.num_programs(1)

    @pl.when(f == 0)
    def _():
        hb_ref[...] = h_ref[...].astype(BF16)
        o_ref[...] = DN_ALPHA * h_ref[...]

    u = jnp.maximum(jnp.dot(hb_ref[...], w1_ref[...], preferred_element_type=F32), 0.0)
    u = (u * u).astype(BF16)
    split = w2a_ref.shape[1]
    for c in range(o_ref.shape[1] // FFN_TN):
        cols = slice(c * FFN_TN, (c + 1) * FFN_TN)
        w2_ref, first = (w2a_ref, 0) if c * FFN_TN < split else (w2b_ref, split)
        w2 = w2_ref[:, c * FFN_TN - first:(c + 1) * FFN_TN - first]
        o_ref[:, cols] += jnp.dot(u, w2, preferred_element_type=F32)

    @pl.when(f == nf - 1)
    def _():
        _layer_norm_rows(o_ref, g_ref, b_ref, mu_ref, rstd_ref)


def _ffn(h, w1, w2a, w2b, g, b, tm, tf):
    m, d = h.shape
    d_ff = w1.shape[1]
    vec = pl.BlockSpec((1, d), lambda i, f: (0, 0))
    return pl.pallas_call(
        _ffn_kernel,
        grid=(m // tm, d_ff // tf),
        in_specs=[pl.BlockSpec((tm, d), lambda i, f: (i, 0)),
                  pl.BlockSpec((d, tf), lambda i, f: (0, f)),
                  pl.BlockSpec((tf, w2a.shape[1]), lambda i, f: (f, 0)),
                  pl.BlockSpec((tf, w2b.shape[1]), lambda i, f: (f, 0)),
                  vec, vec],
        out_specs=pl.BlockSpec((tm, d), lambda i, f: (i, 0)),
        out_shape=jax.ShapeDtypeStruct((m, d), F32),
        scratch_shapes=[pltpu.VMEM((tm, d), BF16)] + _row_stats(tm),
        compiler_params=_params(("parallel", "arbitrary")),
        name="ffn_ln",
    )(h, w1, w2a, w2b, g, b)


def kernel(x, mem, w_in, w_out, cmp_pos, cmp_w1, cmp_w2, xa_wq, xa_wkv, xa_wo,
           w_ff1, w_ff2, ln_g, ln_b):
    batch, seq, d = x.shape
    mem_len = mem.shape[1]
    h = x.reshape(batch * seq, d)
    mem2 = mem.reshape(batch * mem_len, d).astype(BF16)
    for l in range(DEPTH):
        w_in_t = jnp.swapaxes(w_in[l], 0, 1).astype(BF16)
        wg_t = jnp.pad(w_in_t[OFF_GATES:], ((0, LANES - GATE_WIDTH), (0, 0)))
        hb, gates = _cast_gates(h, wg_t, CAST_TM)
        proj, w_ff1_b = _inproj(hb, w_in_t, OFF_GATES, w_ff1, l, INPROJ_TM, INPROJ_TN)
        ret = _retention(proj, batch, seq)
        kc, vc = _compress(proj, cmp_pos[l], cmp_w1[l].astype(BF16), cmp_w2[l].astype(BF16),
                           batch, seq)
        sparse, w_out_b, w_ff2_tail = _nsa(proj, gates, kc, vc, w_out, w_ff2, l, batch, seq)
        vecs = lambda a, k: a[l, k].reshape(1, d)
        h = _outproj(ret, sparse, w_out_b, h, vecs(ln_g, 0), vecs(ln_b, 0), OUTPROJ_TM, OUTPROJ_TN)
        kv = _matmul(mem2, xa_wkv[l].astype(BF16), 2 * XA_WIDTH, KV_TM, KV_TN, BF16)
        h, w_ff2_head = _xattn(h, kv, xa_wq[l].astype(BF16), xa_wo[l].astype(BF16),
                               vecs(ln_g, 1), vecs(ln_b, 1), w_ff2, l, seq, mem_len, XATTN_TM)
        h = _ffn(h, w_ff1_b, w_ff2_head, w_ff2_tail, vecs(ln_g, 2), vecs(ln_b, 2), FFN_TM, FFN_TF)
    return h.reshape(batch, seq, d)
```

```python
import math

import jax
import jax.numpy as jnp
from jax import lax
from jax.experimental import pallas as pl
from jax.experimental.pallas import tpu as pltpu

F32 = jnp.float32
BF16 = jnp.bfloat16

RET_HEADS = 8
RET_DIM = 256
RET_WIDTH = RET_HEADS * RET_DIM
RET_CHUNK = 128
NSA_HEADS = 16
NSA_KV_GROUPS = 4
NSA_REP = NSA_HEADS // NSA_KV_GROUPS
NSA_DIM = 128
NSA_WIDTH = NSA_HEADS * NSA_DIM
NSA_KV_WIDTH = NSA_KV_GROUPS * NSA_DIM
CMP_LEN = 32
CMP_STRIDE = 16
SLC_LEN = 64
SLC_TOPK = 16
WIN_LEN = 512
GATE_WIDTH = 3 * NSA_HEADS
OFF_RQ, OFF_RK, OFF_RV, OFF_RG = 0, RET_WIDTH, 2 * RET_WIDTH, 3 * RET_WIDTH
OFF_NQ = 4 * RET_WIDTH
OFF_KC = OFF_NQ + NSA_WIDTH
OFF_VC = OFF_KC + NSA_KV_WIDTH
OFF_KS = OFF_VC + NSA_KV_WIDTH
OFF_VS = OFF_KS + NSA_KV_WIDTH
OFF_KW = OFF_VS + NSA_KV_WIDTH
OFF_VW = OFF_KW + NSA_KV_WIDTH
OFF_GATES = OFF_VW + NSA_KV_WIDTH
XA_HEADS = 4
XA_DIM = 128
XA_WIDTH = XA_HEADS * XA_DIM
LN_EPS = 1e-5
DEPTH = 1
DN_ALPHA = (2.0 * DEPTH) ** 0.25
NEG_INF = -1e30
MASKED = 2.0 * NEG_INF
FORCE = 1e9
LOG2E = math.log2(math.e)
SLOPE_TERMS = 3

LANES = 128
VMEM_LIMIT = 56 * 1024 * 1024
CAST_TM = 512
INPROJ_TM, INPROJ_TN = 1024, 1024
INPROJ_SIDE_STEPS = 8
RET_CHUNKS_PER_STEP = 4
NSA_TQ = 256
NSA_TK = 256
OUTPROJ_TM, OUTPROJ_TN = 512, 1024
KV_TM, KV_TN = 512, 1024
XATTN_TM = 256
FFN_TM, FFN_TF = 512, 512
FFN_TN = 1024
FFN_W2_SPLIT = 1024
LN_ROWS = 8
LN_STATS_UNROLL = 16
LN_APPLY_UNROLL = 4

_NT = (((1,), (1,)), ((), ()))
_TN = (((0,), (0,)), ((), ()))


def _params(sem):
    return pltpu.CompilerParams(dimension_semantics=sem, vmem_limit_bytes=VMEM_LIMIT)


def _cast_slab_specs(w, layer, steps, step_index, cols=None, col_block=0):
    rows, cols = w.shape[1] // steps, cols or w.shape[2]
    in_spec = pl.BlockSpec((None, rows, cols), lambda *ids: (layer, step_index(*ids), col_block))
    out_spec = pl.BlockSpec((rows, cols), lambda *ids: (step_index(*ids), 0))
    return in_spec, out_spec, jax.ShapeDtypeStruct((w.shape[1], cols), BF16)


def _row_stats(tm):
    return [pltpu.VMEM((tm, LANES), F32), pltpu.VMEM((tm, LANES), F32)]


def _layer_norm_rows(o_ref, g_ref, b_ref, mu_ref, rstd_ref):
    tm, d = o_ref.shape
    n = tm // LN_ROWS

    def stats(c, carry):
        rows = pl.ds(pl.multiple_of(c * LN_ROWS, LN_ROWS), LN_ROWS)
        y = o_ref[rows, :]
        mu = jnp.mean(y, axis=-1, keepdims=True)
        dev = y - mu
        rstd = lax.rsqrt(jnp.mean(dev * dev, axis=-1, keepdims=True) + LN_EPS)
        mu_ref[rows, :] = jnp.broadcast_to(mu, (LN_ROWS, LANES))
        rstd_ref[rows, :] = jnp.broadcast_to(rstd, (LN_ROWS, LANES))
        return carry

    def apply(c, carry):
        rows = pl.ds(pl.multiple_of(c * LN_ROWS, LN_ROWS), LN_ROWS)
        mu, rstd = mu_ref[rows, :], rstd_ref[rows, :]
        for j in range(d // LANES):
            cols = slice(j * LANES, (j + 1) * LANES)
            o_ref[rows, cols] = (o_ref[rows, cols] - mu) * rstd * g_ref[:, cols] + b_ref[:, cols]
        return carry

    lax.fori_loop(0, n, stats, 0, unroll=min(LN_STATS_UNROLL, n))
    lax.fori_loop(0, n, apply, 0, unroll=LN_APPLY_UNROLL)


def _mm_kernel(x_ref, w_ref, o_ref):
    o_ref[...] = jnp.dot(x_ref[...], w_ref[...], preferred_element_type=F32).astype(o_ref.dtype)


def _matmul(x, w, n_out, tm, tn, out_dtype):
    m, k = x.shape
    return pl.pallas_call(
        _mm_kernel,
        grid=(m // tm, n_out // tn),
        in_specs=[pl.BlockSpec((tm, k), lambda i, j: (i, 0)),
                  pl.BlockSpec((k, tn), lambda i, j: (0, j))],
        out_specs=pl.BlockSpec((tm, tn), lambda i, j: (i, j)),
        out_shape=jax.ShapeDtypeStruct((m, n_out), out_dtype),
        compiler_params=_params(("parallel", "parallel")),
        name="matmul",
    )(x, w)


def _cast_gates_kernel(x_ref, wgt_ref, xb_ref, g_ref):
    xb = x_ref[...].astype(BF16)
    xb_ref[...] = xb
    g_ref[...] = lax.dot_general(xb, wgt_ref[...], _NT, preferred_element_type=F32)


def _cast_gates(x, wg_t, tm):
    m, k = x.shape
    return pl.pallas_call(
        _cast_gates_kernel,
        grid=(m // tm,),
        in_specs=[pl.BlockSpec((tm, k), lambda i: (i, 0)),
                  pl.BlockSpec((LANES, k), lambda i: (0, 0), pipeline_mode=pl.Buffered(1))],
        out_specs=[pl.BlockSpec((tm, k), lambda i: (i, 0)), pl.BlockSpec((tm, LANES), lambda i: (i, 0))],
        out_shape=[jax.ShapeDtypeStruct((m, k), BF16), jax.ShapeDtypeStruct((m, LANES), F32)],
        compiler_params=_params(("parallel",)),
        name="cast_gates",
    )(x, wg_t)


def _inproj_kernel(x_ref, w_ref, w1_ref, o_ref, w1b_ref):
    @pl.when(pl.program_id(1) < INPROJ_SIDE_STEPS)
    def _():
        w1b_ref[...] = w1_ref[...].astype(BF16)

    o_ref[...] = lax.dot_general(x_ref[...], w_ref[...], _NT, preferred_element_type=F32).astype(o_ref.dtype)


def _inproj(x, w_t, n_out, w_ff1, layer, tm, tn):
    m, k = x.shape
    side = INPROJ_SIDE_STEPS
    slab = lambda i, j: i * side + jnp.minimum(j, side - 1)
    w1_in, w1_out, w1_shape = _cast_slab_specs(w_ff1, layer, (m // tm) * side, slab)
    return pl.pallas_call(
        _inproj_kernel,
        grid=(m // tm, n_out // tn),
        in_specs=[pl.BlockSpec((tm, k), lambda i, j: (i, 0)),
                  pl.BlockSpec((tn, k), lambda i, j: (j, 0)), w1_in],
        out_specs=[pl.BlockSpec((tm, tn), lambda i, j: (i, j)), w1_out],
        out_shape=[jax.ShapeDtypeStruct((m, n_out), BF16), w1_shape],
        compiler_params=_params(("parallel", "arbitrary")),
        name="inproj",
    )(x, w_t, w_ff1)


def _retention_kernel(q_ref, k_ref, v_ref, g_ref, o_ref, state_ref, decay_ref, ktail_ref, qhead_ref):
    c = RET_CHUNK

    @pl.when(pl.program_id(1) == 0)
    def _():
        state_ref[...] = jnp.zeros_like(state_ref)
        diff = (lax.broadcasted_iota(jnp.int32, (c, c), 0)
                - lax.broadcasted_iota(jnp.int32, (c, c), 1)).astype(F32)
        pos = lax.broadcasted_iota(jnp.int32, (c, RET_DIM), 0).astype(F32)
        for h in range(RET_HEADS):
            log_g = math.log1p(-(2.0 ** (-5.0 - h)))
            decay_ref[h] = jnp.where(diff >= 0, jnp.exp(log_g * jnp.maximum(diff, 0.0)), 0.0)
            ktail_ref[h] = jnp.exp(log_g * (c - 1.0 - pos))
            qhead_ref[h] = jnp.exp(log_g * (pos + 1.0))

    for chunk, h in [(ch, hd) for ch in range(q_ref.shape[0] // c) for hd in range(RET_HEADS)]:
        log_g = math.log1p(-(2.0 ** (-5.0 - h)))
        sl = (slice(chunk * c, (chunk + 1) * c), slice(h * RET_DIM, (h + 1) * RET_DIM))
        q = q_ref[sl].astype(F32)
        k = k_ref[sl].astype(F32) * (RET_DIM ** -0.5)
        v = v_ref[sl]
        scores = lax.dot_general(q.astype(BF16), k.astype(BF16), _NT,
                                 preferred_element_type=F32) * decay_ref[h]
        o_inner = jnp.dot(scores.astype(BF16), v, preferred_element_type=F32)
        k_tail = k * ktail_ref[h]
        kv = lax.dot_general(k_tail.astype(BF16), v, _TN, preferred_element_type=F32)
        q_head = q * qhead_ref[h]
        state = state_ref[h]
        o = o_inner + jnp.dot(q_head.astype(BF16), state.astype(BF16), preferred_element_type=F32)
        state_ref[h] = math.exp(log_g * c) * state + kv
        mu = jnp.mean(o, axis=-1, keepdims=True)
        d = o - mu
        var = jnp.mean(d * d, axis=-1, keepdims=True)
        o = d * lax.rsqrt(var + LN_EPS)
        gate = g_ref[sl].astype(F32)
        o_ref[sl] = (o * (gate * jax.nn.sigmoid(gate))).astype(o_ref.dtype)


def _retention(proj, batch, seq):
    rows = RET_CHUNKS_PER_STEP * RET_CHUNK
    n = seq // rows
    blk = lambda col: pl.BlockSpec((rows, RET_WIDTH), lambda b, i: (b * n + i, col))
    return pl.pallas_call(
        _retention_kernel,
        grid=(batch, n),
        in_specs=[blk(OFF_RQ // RET_WIDTH), blk(OFF_RK // RET_WIDTH),
                  blk(OFF_RV // RET_WIDTH), blk(OFF_RG // RET_WIDTH)],
        out_specs=pl.BlockSpec((rows, RET_WIDTH), lambda b, i: (b * n + i, 0)),
        out_shape=jax.ShapeDtypeStruct((batch * seq, RET_WIDTH), BF16),
        scratch_shapes=[pltpu.VMEM((RET_HEADS, RET_DIM, RET_DIM), F32),
                        pltpu.VMEM((RET_HEADS, RET_CHUNK, RET_CHUNK), F32),
                        pltpu.VMEM((RET_HEADS, RET_CHUNK, RET_DIM), F32),
                        pltpu.VMEM((RET_HEADS, RET_CHUNK, RET_DIM), F32)],
        compiler_params=_params(("parallel", "arbitrary")),
        name="retention",
    )(proj, proj, proj, proj)


def _gelu_tanh(x):
    return 0.5 * x * (1.0 + jnp.tanh(math.sqrt(2.0 / math.pi) * (x + 0.044715 * (x * x * x))))


def _compress_kernel(zk_ref, zv_ref, pos_ref, w1_ref, w2_ref, kc_ref, vc_ref, zf_ref):
    nsub = zf_ref.shape[0] // CMP_STRIDE
    for kv, (z_ref, o_ref) in enumerate(((zk_ref, kc_ref), (zv_ref, vc_ref))):
        zf_ref[...] = z_ref[...].astype(F32)
        hid_a = jnp.zeros((nsub, LANES), F32)
        hid_b = jnp.zeros((nsub, LANES), F32)
        for i in range(CMP_STRIDE):
            rows = zf_ref[pl.ds(i, nsub, stride=CMP_STRIDE), :]
            xa = (rows + pos_ref[kv, i:i + 1, :]).astype(BF16)
            xb = (rows + pos_ref[kv, CMP_STRIDE + i:CMP_STRIDE + i + 1, :]).astype(BF16)
            wa = w1_ref[kv, i * NSA_DIM:(i + 1) * NSA_DIM, :]
            wb = w1_ref[kv, (CMP_STRIDE + i) * NSA_DIM:(CMP_STRIDE + i + 1) * NSA_DIM, :]
            hid_a = hid_a + jnp.dot(xa, wa, preferred_element_type=F32)
            hid_b = hid_b + jnp.dot(xb, wb, preferred_element_type=F32)
        hid = _gelu_tanh(hid_a + pltpu.roll(hid_b, nsub - 1, 0))
        o_ref[...] = jnp.dot(hid.astype(BF16), w2_ref[kv], preferred_element_type=F32).astype(o_ref.dtype)


def _compress(proj, cmp_pos, cmp_w1, cmp_w2, batch, seq):
    nsub = seq // CMP_STRIDE
    zspec = lambda col0: pl.BlockSpec((seq, NSA_DIM), lambda b, g: (b, col0 + g))
    full = lambda a: pl.BlockSpec(a.shape, lambda b, g: (0,) * a.ndim)
    ospec = pl.BlockSpec((None, None, nsub, NSA_DIM), lambda b, g: (b, g, 0, 0))
    oshape = jax.ShapeDtypeStruct((batch, NSA_KV_GROUPS, nsub, NSA_DIM), BF16)
    return pl.pallas_call(
        _compress_kernel,
        grid=(batch, NSA_KV_GROUPS),
        in_specs=[zspec(OFF_KC // NSA_DIM), zspec(OFF_VC // NSA_DIM),
                  full(cmp_pos), full(cmp_w1), full(cmp_w2)],
        out_specs=[ospec, ospec],
        out_shape=[oshape, oshape],
        scratch_shapes=[pltpu.VMEM((seq, NSA_DIM), F32)],
        compiler_params=_params(("parallel", "parallel")),
        name="compress",
    )(proj, proj, cmp_pos, cmp_w1, cmp_w2)


def _nsa_kernel(q_ref, kc_ref, vc_ref, ks_ref, vs_ref, kw_ref, vw_ref, gate_ref, wo_ref, w2_ref,
                o_ref, wob_ref, w2b_ref,
                q4_ref, slope_ref, vct_ref, vst_ref, vwt_ref, sig_ref, m_ref, l_ref, acc_ref, out_ref):
    tq, tk, rep, groups = NSA_TQ, NSA_TK, NSA_REP, NSA_KV_GROUPS
    cols = rep * tq
    n_tiles = ks_ref.shape[0] // tk
    n_blocks = ks_ref.shape[0] // SLC_LEN
    n_cmp = kc_ref.shape[1]
    slc, win = 0, 1
    i = pl.program_id(1)
    t0 = i * tq
    gcols = lambda g: slice(g * NSA_DIM, (g + 1) * NSA_DIM)

    wob_ref[...] = wo_ref[...].astype(BF16)
    w2b_ref[...] = w2_ref[...].astype(BF16)

    pos_lane, start_lane = n_blocks, n_blocks + 2 * SLOPE_TERMS

    def alibi_slope(head):
        return LOG2E * jnp.exp((head + 1).astype(F32) * (-0.5 * math.log(2.0)))

    @pl.when(i == 0)
    def _():
        for g in range(groups):
            vct_ref[g] = vc_ref[g].astype(F32).T.astype(BF16)
        row_head = lax.broadcasted_iota(jnp.int32, (cols, NSA_DIM), 0) >> (tq.bit_length() - 1)
        row_lane = lax.broadcasted_iota(jnp.int32, (cols, NSA_DIM), 1)
        slope_term = jnp.where(row_lane < start_lane, (row_lane - pos_lane) >> 1, row_lane - start_lane)
        for g in range(groups):
            rest = alibi_slope(g * rep + row_head)
            terms = jnp.zeros((cols, NSA_DIM), F32)
            for n in range(SLOPE_TERMS):
                term = rest.astype(BF16).astype(F32)
                rest = rest - term
                terms = jnp.where((row_lane >= pos_lane) & (slope_term == n), term, terms)
            slope_ref[g] = terms.astype(BF16)

        def transpose_tile(t, carry):
            rows = pl.ds(pl.multiple_of(t * tk, tk), tk)
            for g in range(groups):
                vst_ref[g, t] = vs_ref[rows, gcols(g)].astype(F32).T.astype(BF16)
                vwt_ref[g, t] = vw_ref[rows, gcols(g)].astype(F32).T.astype(BF16)
            return carry

        lax.fori_loop(0, n_tiles, transpose_tile, 0)

    sub = lax.broadcasted_iota(jnp.int32, (tk, cols), 0)
    t_loc = lax.broadcasted_iota(jnp.int32, (tk, cols), 1) & (tq - 1)
    lane_head = lax.broadcasted_iota(jnp.int32, (1, cols), 1) >> (tq.bit_length() - 1)
    causal = sub <= t_loc
    c_end = lax.broadcasted_iota(jnp.int32, (n_cmp, cols), 0) * CMP_STRIDE + (CMP_LEN - 1)
    c_mask = c_end <= t0 + (lax.broadcasted_iota(jnp.int32, (n_cmp, cols), 1) & (tq - 1))
    s_idx = lax.broadcasted_iota(jnp.int32, (n_blocks, n_cmp), 0)
    c_idx = lax.broadcasted_iota(jnp.int32, (n_blocks, n_cmp), 1)
    overlap = jnp.maximum(jnp.minimum(c_idx * CMP_STRIDE + CMP_LEN, s_idx * SLC_LEN + SLC_LEN)
                          - jnp.maximum(c_idx * CMP_STRIDE, s_idx * SLC_LEN), 0)
    overlap = (overlap.astype(F32) * (1.0 / CMP_LEN)).astype(BF16)
    blk = lax.broadcasted_iota(jnp.int32, (n_blocks, tq), 0)
    cur = (t0 + lax.broadcasted_iota(jnp.int32, (n_blocks, tq), 1)) >> (SLC_LEN.bit_length() - 1)
    forced = (blk == 0) | (blk == cur) | (blk == cur - 1)
    future = blk > cur

    sig_ref[...] = jax.nn.sigmoid(gate_ref[...]).T

    def gate_row(g, branch):
        first = branch * NSA_HEADS + g * rep
        return jnp.concatenate([sig_ref[first + r:first + r + 1, :] for r in range(rep)], axis=1)

    pos = lax.broadcasted_iota(jnp.int32, (tk, NSA_DIM), 0)
    key_lane = lax.broadcasted_iota(jnp.int32, (tk, NSA_DIM), 1)
    in_pos = (key_lane >= pos_lane) & (key_lane < start_lane)
    in_start = (key_lane >= start_lane) & (key_lane < start_lane + SLOPE_TERMS)
    pos_terms = jnp.where(((key_lane - pos_lane) & 1) == 0, (pos // SLC_LEN) * SLC_LEN, pos % SLC_LEN)
    pos_terms = jnp.where(in_pos, pos_terms, 0).astype(F32)

    def key_cols(kt, selected):
        start = kt * tk
        extra = pos_terms + jnp.where(in_start, start.astype(F32), 0.0)
        if selected:
            extra = extra + jnp.where(key_lane == ((start + pos) >> (SLC_LEN.bit_length() - 1)), 1.0, 0.0)
        return extra.astype(BF16)

    bias_lane = lax.broadcasted_iota(jnp.int32, (cols, NSA_DIM), 1) < n_blocks

    for g in range(groups):
        q4 = jnp.concatenate([q_ref[:, (g * rep + r) * NSA_DIM:(g * rep + r + 1) * NSA_DIM]
                              for r in range(rep)], axis=0)
        q4 = (q4.astype(F32) * (LOG2E * NSA_DIM ** -0.5)).astype(BF16)
        slope = alibi_slope(g * rep + lane_head)

        st = lax.dot_general(kc_ref[g], q4, _NT, preferred_element_type=F32)
        st = jnp.where(c_mask, st + slope * c_end.astype(F32), MASKED)
        e = jnp.exp2(st - jnp.max(st, axis=0, keepdims=True))
        p_cmp = jnp.where(c_mask, e * (1.0 / jnp.sum(e, axis=0, keepdims=True)), 0.0)
        o_cmp = jnp.dot(vct_ref[g], p_cmp.astype(BF16), preferred_element_type=F32)
        out_ref[g] = gate_row(g, 0) * o_cmp

        p_sum = p_cmp[:, 0:tq]
        for r in range(1, rep):
            p_sum = p_sum + p_cmp[:, r * tq:(r + 1) * tq]
        p_hi = p_sum.astype(BF16)
        p_lo = (p_sum - p_hi.astype(F32)).astype(BF16)
        imp = (jnp.dot(overlap, p_hi, preferred_element_type=F32)
               + jnp.dot(overlap, p_lo, preferred_element_type=F32))
        imp = jnp.where(forced, FORCE, jnp.where(future, -FORCE, imp))
        rank = jnp.zeros((n_blocks, tq), F32)
        for j in range(n_blocks):
            row = imp[j:j + 1, :]
            before = (row > imp) | ((row == imp) & (blk > j))
            rank = rank + jnp.where(before, 1.0, 0.0)
        sel_bias = jnp.where(rank < SLC_TOPK, 0.0, MASKED)

        sel_t = jnp.concatenate([sel_bias, jnp.zeros((NSA_DIM - n_blocks, tq), F32)], axis=0).T
        extra = jnp.where(bias_lane, jnp.concatenate([sel_t] * rep, axis=0), slope_ref[g].astype(F32))
        q4_ref[g] = jnp.concatenate([q4, extra.astype(BF16)], axis=1)

    def scores(g, k, extra):
        k = jnp.concatenate([k, extra], axis=1)
        return lax.dot_general(k, q4_ref[g], _NT, preferred_element_type=F32)

    def update(branch, g, st, vt, keep, first, after):
        if after is not None:
            vt = vt + (0.0 * after[0:1, 0:tk]).astype(BF16)
        if keep is not None:
            st = jnp.where(keep, st, MASKED)
        m_tile = jnp.max(st, axis=0, keepdims=True)
        if first:
            p = jnp.exp2(st - m_tile)
            m_ref[branch, g] = m_tile
            l_ref[branch, g] = jnp.sum(p, axis=0, keepdims=True)
            acc_ref[branch, g] = jnp.dot(vt, p.astype(BF16), preferred_element_type=F32)
        else:
            m_old = m_ref[branch, g]
            m_new = jnp.maximum(m_old, m_tile)
            a = jnp.exp2(m_old - m_new)
            p = jnp.exp2(st - m_new)
            m_ref[branch, g] = m_new
            l_ref[branch, g] = a * l_ref[branch, g] + jnp.sum(p, axis=0, keepdims=True)
            acc_ref[branch, g] = a * acc_ref[branch, g] + jnp.dot(vt, p.astype(BF16), preferred_element_type=F32)

    def slc_chains(kt):
        rows = pl.ds(pl.multiple_of(kt * tk, tk), tk)
        extra = key_cols(kt, True)
        return [(slc, g, ks_ref[rows, gcols(g)], extra, vst_ref[g, kt]) for g in range(groups)]

    def win_chains(kt):
        rows = pl.ds(pl.multiple_of(kt * tk, tk), tk)
        extra = key_cols(kt, False)
        return [(win, g, kw_ref[rows, gcols(g)], extra, vwt_ref[g, kt]) for g in range(groups)]

    def run(chains, keep, first):
        sts = [scores(g, k, extra) for (_, g, k, extra, _) in chains]
        for n, (branch, g, _, _, vt) in enumerate(chains):
            update(branch, g, sts[n], vt, keep, first, sts[n + 1] if n + 1 < len(sts) else None)

    run(slc_chains(i) + win_chains(i), causal, True)

    def slc_body(kt, carry):
        run(slc_chains(kt), None, False)
        return carry

    lax.fori_loop(0, i, slc_body, 0)

    n_back = WIN_LEN // tk

    @pl.when(i >= n_back)
    def _():
        run(win_chains(i - n_back), sub > t_loc, False)

    def win_body(kt, carry):
        run(win_chains(kt), None, False)
        return carry

    lax.fori_loop(jnp.maximum(i - (n_back - 1), 0), i, win_body, 0)

    for g in range(groups):
        o = out_ref[g]
        for branch, state in ((1, slc), (2, win)):
            o = o + (gate_row(g, branch) * (1.0 / l_ref[state, g])) * acc_ref[state, g]
        for r in range(rep):
            h = g * rep + r
            o_ref[:, h * NSA_DIM:(h + 1) * NSA_DIM] = o[:, r * tq:(r + 1) * tq].T.astype(o_ref.dtype)


def _nsa(proj, gates, kc, vc, w_out, w_ff2, layer, batch, seq):
    nq = seq // NSA_TQ
    step = lambda b, i: b * nq + i
    wo_in, wo_out, wo_shape = _cast_slab_specs(w_out, layer, batch * nq, step)
    w2_in, w2_out, w2_shape = _cast_slab_specs(w_ff2, layer, batch * nq, step, FFN_W2_SPLIT,
                                               w_ff2.shape[2] // FFN_W2_SPLIT - 1)
    groups, cols = NSA_KV_GROUPS, NSA_REP * NSA_TQ
    n_tiles = seq // NSA_TK
    kvspec = lambda off: pl.BlockSpec((seq, NSA_KV_WIDTH), lambda b, i: (b, off // NSA_KV_WIDTH))
    cspec = lambda a: pl.BlockSpec((None,) + a.shape[1:], lambda b, i: (b, 0, 0, 0))
    return pl.pallas_call(
        _nsa_kernel,
        grid=(batch, nq),
        in_specs=[pl.BlockSpec((NSA_TQ, NSA_WIDTH), lambda b, i: (b * nq + i, OFF_NQ // NSA_WIDTH)),
                  cspec(kc), cspec(vc),
                  kvspec(OFF_KS), kvspec(OFF_VS), kvspec(OFF_KW), kvspec(OFF_VW),
                  pl.BlockSpec((NSA_TQ, LANES), lambda b, i: (b * nq + i, 0)), wo_in, w2_in],
        out_specs=[pl.BlockSpec((NSA_TQ, NSA_WIDTH), lambda b, i: (b * nq + i, 0)), wo_out, w2_out],
        out_shape=[jax.ShapeDtypeStruct((batch * seq, NSA_WIDTH), BF16), wo_shape, w2_shape],
        scratch_shapes=[pltpu.VMEM((groups, cols, 2 * NSA_DIM), BF16),
                        pltpu.VMEM((groups, cols, NSA_DIM), BF16),
                        pltpu.VMEM((groups, NSA_DIM, seq // CMP_STRIDE), BF16),
                        pltpu.VMEM((groups, n_tiles, NSA_DIM, NSA_TK), BF16),
                        pltpu.VMEM((groups, n_tiles, NSA_DIM, NSA_TK), BF16),
                        pltpu.VMEM((LANES, NSA_TQ), F32),
                        pltpu.VMEM((2, groups, 1, cols), F32),
                        pltpu.VMEM((2, groups, 1, cols), F32),
                        pltpu.VMEM((2, groups, NSA_DIM, cols), F32),
                        pltpu.VMEM((groups, NSA_DIM, cols), F32)],
        compiler_params=_params(("parallel", "arbitrary")),
        name="nsa",
    )(proj, kc, vc, proj, proj, proj, proj, gates, w_out, w_ff2)


def _outproj_kernel(ret_ref, nsa_ref, wr_ref, wn_ref, x_ref, g_ref, b_ref, o_ref, mu_ref, rstd_ref):
    j = pl.program_id(1)
    nj = pl.num_programs(1)
    tn = x_ref.shape[1]
    y = (jnp.dot(ret_ref[...], wr_ref[...], preferred_element_type=F32)
         + jnp.dot(nsa_ref[...], wn_ref[...], preferred_element_type=F32))
    y = DN_ALPHA * x_ref[...] + y
    o_ref[:, pl.ds(pl.multiple_of(j * tn, tn), tn)] = y

    @pl.when(j == nj - 1)
    def _():
        _layer_norm_rows(o_ref, g_ref, b_ref, mu_ref, rstd_ref)


def _outproj(ret, nsa, w_out, x, g, b, tm, tn):
    m, d = x.shape
    vec = pl.BlockSpec((1, d), lambda i, j: (0, 0))
    return pl.pallas_call(
        _outproj_kernel,
        grid=(m // tm, d // tn),
        in_specs=[pl.BlockSpec((tm, RET_WIDTH), lambda i, j: (i, 0)),
                  pl.BlockSpec((tm, NSA_WIDTH), lambda i, j: (i, 0)),
                  pl.BlockSpec((RET_WIDTH, tn), lambda i, j: (0, j)),
                  pl.BlockSpec((NSA_WIDTH, tn), lambda i, j: (RET_WIDTH // NSA_WIDTH, j)),
                  pl.BlockSpec((tm, tn), lambda i, j: (i, j)),
                  vec, vec],
        out_specs=pl.BlockSpec((tm, d), lambda i, j: (i, 0)),
        out_shape=jax.ShapeDtypeStruct((m, d), F32),
        scratch_shapes=_row_stats(tm),
        compiler_params=_params(("parallel", "arbitrary")),
        name="outproj_ln",
    )(ret, nsa, w_out, w_out, x, g, b)


def _xattn_kernel(h_ref, kv_ref, wq_ref, wo_ref, g_ref, b_ref, w2_ref, o_ref, w2b_ref, mu_ref, rstd_ref):
    w2b_ref[...] = w2_ref[...].astype(BF16)

    q = jnp.dot(h_ref[...].astype(BF16), wq_ref[...], preferred_element_type=F32).astype(BF16)
    outs = []
    for hd in range(XA_HEADS):
        sl = slice(hd * XA_DIM, (hd + 1) * XA_DIM)
        k = kv_ref[:, sl]
        v = kv_ref[:, XA_WIDTH + hd * XA_DIM:XA_WIDTH + (hd + 1) * XA_DIM]
        s = lax.dot_general(q[:, sl], k, _NT, preferred_element_type=F32) * (XA_DIM ** -0.5)
        e = jnp.exp(s - jnp.max(s, axis=-1, keepdims=True))
        p = e / jnp.sum(e, axis=-1, keepdims=True)
        outs.append(jnp.dot(p.astype(BF16), v, preferred_element_type=F32))
    o = jnp.concatenate(outs, axis=-1).astype(BF16)
    o_ref[...] = DN_ALPHA * h_ref[...] + jnp.dot(o, wo_ref[...], preferred_element_type=F32)
    _layer_norm_rows(o_ref, g_ref, b_ref, mu_ref, rstd_ref)


def _xattn(h, kv, wq, wo, g, b, w_ff2, layer, seq, mem_len, tm):
    m, d = h.shape
    per_batch = seq // tm
    steps = m // tm
    full = lambda a: pl.BlockSpec(a.shape, lambda i: (0, 0), pipeline_mode=pl.Buffered(1))
    w2_in, w2_out, w2_shape = _cast_slab_specs(w_ff2, layer, steps, lambda i: i, d - FFN_W2_SPLIT, 0)
    return pl.pallas_call(
        _xattn_kernel,
        grid=(steps,),
        in_specs=[pl.BlockSpec((tm, d), lambda i: (i, 0)),
                  pl.BlockSpec((mem_len, 2 * XA_WIDTH), lambda i: (i // per_batch, 0)),
                  full(wq), full(wo), full(g), full(b), w2_in],
        out_specs=[pl.BlockSpec((tm, d), lambda i: (i, 0)), w2_out],
        out_shape=[jax.ShapeDtypeStruct((m, d), F32), w2_shape],
        scratch_shapes=_row_stats(tm),
        compiler_params=_params(("parallel",)),
        name="xattn_ln",
    )(h, kv, wq, wo, g, b, w_ff2)


def _ffn_kernel(h_ref, w1_ref, w2a_ref, w2b_ref, g_ref, b_ref, o_ref, hb_ref, mu_ref, rstd_ref):
    f = pl.program_id(1)
    nf = pl.num_programs(1)

    @pl.when(f == 0)
    def _():
        hb_ref[...] = h_ref[...].astype(BF16)
        o_ref[...] = DN_ALPHA * h_ref[...]

    u = jnp.maximum(jnp.dot(hb_ref[...], w1_ref[...], preferred_element_type=F32), 0.0)
    u = (u * u).astype(BF16)
    split = w2a_ref.shape[1]
    for c in range(o_ref.shape[1] // FFN_TN):
        cols = slice(c * FFN_TN, (c + 1) * FFN_TN)
        w2_ref, first = (w2a_ref, 0) if c * FFN_TN < split else (w2b_ref, split)
        w2 = w2_ref[:, c * FFN_TN - first:(c + 1) * FFN_TN - first]
        o_ref[:, cols] += jnp.dot(u, w2, preferred_element_type=F32)

    @pl.when(f == nf - 1)
    def _():
        _layer_norm_rows(o_ref, g_ref, b_ref, mu_ref, rstd_ref)


def _ffn(h, w1, w2a, w2b, g, b, tm, tf):
    m, d = h.shape
    d_ff = w1.shape[1]
    vec = pl.BlockSpec((1, d), lambda i, f: (0, 0))
    return pl.pallas_call(
        _ffn_kernel,
        grid=(m // tm, d_ff // tf),
        in_specs=[pl.BlockSpec((tm, d), lambda i, f: (i, 0)),
                  pl.BlockSpec((d, tf), lambda i, f: (0, f)),
                  pl.BlockSpec((tf, w2a.shape[1]), lambda i, f: (f, 0)),
                  pl.BlockSpec((tf, w2b.shape[1]), lambda i, f: (f, 0)),
                  vec, vec],
        out_specs=pl.BlockSpec((tm, d), lambda i, f: (i, 0)),
        out_shape=jax.ShapeDtypeStruct((m, d), F32),
        scratch_shapes=[pltpu.VMEM((tm, d), BF16)] + _row_stats(tm),
        compiler_params=_params(("parallel", "arbitrary")),
        name="ffn_ln",
    )(h, w1, w2a, w2b, g, b)


def kernel(x, mem, w_in, w_out, cmp_pos, cmp_w1, cmp_w2, xa_wq, xa_wkv, xa_wo,
           w_ff1, w_ff2, ln_g, ln_b):
    batch, seq, d = x.shape
    mem_len = mem.shape[1]
    h = x.reshape(batch * seq, d)
    mem2 = mem.reshape(batch * mem_len, d).astype(BF16)
    for l in range(DEPTH):
        w_in_t = jnp.swapaxes(w_in[l], 0, 1).astype(BF16)
        wg_t = jnp.pad(w_in_t[OFF_GATES:], ((0, LANES - GATE_WIDTH), (0, 0)))
        hb, gates = _cast_gates(h, wg_t, CAST_TM)
        proj, w_ff1_b = _inproj(hb, w_in_t, OFF_GATES, w_ff1, l, INPROJ_TM, INPROJ_TN)
        ret = _retention(proj, batch, seq)
        kc, vc = _compress(proj, cmp_pos[l], cmp_w1[l].astype(BF16), cmp_w2[l].astype(BF16),
                           batch, seq)
        sparse, w_out_b, w_ff2_tail = _nsa(proj, gates, kc, vc, w_out, w_ff2, l, batch, seq)
        vecs = lambda a, k: a[l, k].reshape(1, d)
        h = _outproj(ret, sparse, w_out_b, h, vecs(ln_g, 0), vecs(ln_b, 0), OUTPROJ_TM, OUTPROJ_TN)
        kv = _matmul(mem2, xa_wkv[l].astype(BF16), 2 * XA_WIDTH, KV_TM, KV_TN, BF16)
        h, w_ff2_head = _xattn(h, kv, xa_wq[l].astype(BF16), xa_wo[l].astype(BF16),
                               vecs(ln_g, 1), vecs(ln_b, 1), w_ff2, l, seq, mem_len, XATTN_TM)
        h = _ffn(h, w_ff1_b, w_ff2_head, w_ff2_tail, vecs(ln_g, 2), vecs(ln_b, 2), FFN_TM, FFN_TF)
    return h.reshape(batch, seq, d)
```

```python
import math

import jax
import jax.numpy as jnp
from jax import lax
from jax.experimental import pallas as pl
from jax.experimental.pallas import tpu as pltpu

F32 = jnp.float32
BF16 = jnp.bfloat16

RET_HEADS = 8
RET_DIM = 256
RET_WIDTH = RET_HEADS * RET_DIM
RET_CHUNK = 128
NSA_HEADS = 16
NSA_KV_GROUPS = 4
NSA_REP = NSA_HEADS // NSA_KV_GROUPS
NSA_DIM = 128
NSA_WIDTH = NSA_HEADS * NSA_DIM
NSA_KV_WIDTH = NSA_KV_GROUPS * NSA_DIM
CMP_LEN = 32
CMP_STRIDE = 16
SLC_LEN = 64
SLC_TOPK = 16
WIN_LEN = 512
GATE_WIDTH = 3 * NSA_HEADS
OFF_RQ, OFF_RK, OFF_RV, OFF_RG = 0, RET_WIDTH, 2 * RET_WIDTH, 3 * RET_WIDTH
OFF_NQ = 4 * RET_WIDTH
OFF_KC = OFF_NQ + NSA_WIDTH
OFF_VC = OFF_KC + NSA_KV_WIDTH
OFF_KS = OFF_VC + NSA_KV_WIDTH
OFF_VS = OFF_KS + NSA_KV_WIDTH
OFF_KW = OFF_VS + NSA_KV_WIDTH
OFF_VW = OFF_KW + NSA_KV_WIDTH
OFF_GATES = OFF_VW + NSA_KV_WIDTH
XA_HEADS = 4
XA_DIM = 128
XA_WIDTH = XA_HEADS * XA_DIM
LN_EPS = 1e-5
DEPTH = 1
DN_ALPHA = (2.0 * DEPTH) ** 0.25
NEG_INF = -1e30
MASKED = 2.0 * NEG_INF
FORCE = 1e9
LOG2E = math.log2(math.e)
SLOPE_TERMS = 3

LANES = 128
VMEM_LIMIT = 56 * 1024 * 1024
CAST_TM = 512
INPROJ_TM, INPROJ_TN = 1024, 1024
INPROJ_SIDE_STEPS = 8
RET_CHUNKS_PER_STEP = 4
NSA_TQ = 256
NSA_TK = 256
OUTPROJ_TM, OUTPROJ_TN = 512, 1024
KV_TM, KV_TN = 512, 1024
XATTN_TM = 256
FFN_TM, FFN_TF = 512, 512
FFN_TN = 1024
FFN_W2_PARTS = (2048, 1024, 1024)
LN_ROWS = 8
LN_STATS_UNROLL = 16
LN_APPLY_UNROLL = 4

_NT = (((1,), (1,)), ((), ()))
_TN = (((0,), (0,)), ((), ()))


def _params(sem):
    return pltpu.CompilerParams(dimension_semantics=sem, vmem_limit_bytes=VMEM_LIMIT)


def _cast_slab_specs(w, layer, steps, step_index, cols=None, col_block=0):
    rows, cols = w.shape[1] // steps, cols or w.shape[2]
    in_spec = pl.BlockSpec((None, rows, cols), lambda *ids: (layer, step_index(*ids), col_block))
    out_spec = pl.BlockSpec((rows, cols), lambda *ids: (step_index(*ids), 0))
    return in_spec, out_spec, jax.ShapeDtypeStruct((w.shape[1], cols), BF16)


def _row_stats(tm):
    return [pltpu.VMEM((tm, LANES), F32), pltpu.VMEM((tm, LANES), F32)]


def _layer_norm_rows(o_ref, g_ref, b_ref, mu_ref, rstd_ref):
    tm, d = o_ref.shape
    n = tm // LN_ROWS

    def stats(c, carry):
        rows = pl.ds(pl.multiple_of(c * LN_ROWS, LN_ROWS), LN_ROWS)
        y = o_ref[rows, :]
        mu = jnp.mean(y, axis=-1, keepdims=True)
        dev = y - mu
        rstd = lax.rsqrt(jnp.mean(dev * dev, axis=-1, keepdims=True) + LN_EPS)
        mu_ref[rows, :] = jnp.broadcast_to(mu, (LN_ROWS, LANES))
        rstd_ref[rows, :] = jnp.broadcast_to(rstd, (LN_ROWS, LANES))
        return carry

    def apply(c, carry):
        rows = pl.ds(pl.multiple_of(c * LN_ROWS, LN_ROWS), LN_ROWS)
        mu, rstd = mu_ref[rows, :], rstd_ref[rows, :]
        for j in range(d // LANES):
            cols = slice(j * LANES, (j + 1) * LANES)
            o_ref[rows, cols] = (o_ref[rows, cols] - mu) * rstd * g_ref[:, cols] + b_ref[:, cols]
        return carry

    lax.fori_loop(0, n, stats, 0, unroll=min(LN_STATS_UNROLL, n))
    lax.fori_loop(0, n, apply, 0, unroll=LN_APPLY_UNROLL)


def _mm_kernel(x_ref, w_ref, o_ref):
    o_ref[...] = jnp.dot(x_ref[...], w_ref[...], preferred_element_type=F32).astype(o_ref.dtype)


def _matmul(x, w, n_out, tm, tn, out_dtype):
    m, k = x.shape
    return pl.pallas_call(
        _mm_kernel,
        grid=(m // tm, n_out // tn),
        in_specs=[pl.BlockSpec((tm, k), lambda i, j: (i, 0)),
                  pl.BlockSpec((k, tn), lambda i, j: (0, j))],
        out_specs=pl.BlockSpec((tm, tn), lambda i, j: (i, j)),
        out_shape=jax.ShapeDtypeStruct((m, n_out), out_dtype),
        compiler_params=_params(("parallel", "parallel")),
        name="matmul",
    )(x, w)


def _cast_gates_kernel(x_ref, wgt_ref, xb_ref, g_ref):
    xb = x_ref[...].astype(BF16)
    xb_ref[...] = xb
    g_ref[...] = lax.dot_general(xb, wgt_ref[...], _NT, preferred_element_type=F32)


def _cast_gates(x, wg_t, tm):
    m, k = x.shape
    return pl.pallas_call(
        _cast_gates_kernel,
        grid=(m // tm,),
        in_specs=[pl.BlockSpec((tm, k), lambda i: (i, 0)),
                  pl.BlockSpec((LANES, k), lambda i: (0, 0), pipeline_mode=pl.Buffered(1))],
        out_specs=[pl.BlockSpec((tm, k), lambda i: (i, 0)), pl.BlockSpec((tm, LANES), lambda i: (i, 0))],
        out_shape=[jax.ShapeDtypeStruct((m, k), BF16), jax.ShapeDtypeStruct((m, LANES), F32)],
        compiler_params=_params(("parallel",)),
        name="cast_gates",
    )(x, wg_t)


def _inproj_kernel(x_ref, w_ref, w1_ref, o_ref, w1b_ref):
    @pl.when(pl.program_id(1) < INPROJ_SIDE_STEPS)
    def _():
        w1b_ref[...] = w1_ref[...].astype(BF16)

    o_ref[...] = lax.dot_general(x_ref[...], w_ref[...], _NT, preferred_element_type=F32).astype(o_ref.dtype)


def _inproj(x, w_t, n_out, w_ff1, layer, tm, tn):
    m, k = x.shape
    side = INPROJ_SIDE_STEPS
    slab = lambda i, j: i * side + jnp.minimum(j, side - 1)
    w1_in, w1_out, w1_shape = _cast_slab_specs(w_ff1, layer, (m // tm) * side, slab)
    return pl.pallas_call(
        _inproj_kernel,
        grid=(m // tm, n_out // tn),
        in_specs=[pl.BlockSpec((tm, k), lambda i, j: (i, 0)),
                  pl.BlockSpec((tn, k), lambda i, j: (j, 0)), w1_in],
        out_specs=[pl.BlockSpec((tm, tn), lambda i, j: (i, j)), w1_out],
        out_shape=[jax.ShapeDtypeStruct((m, n_out), BF16), w1_shape],
        compiler_params=_params(("parallel", "arbitrary")),
        name="inproj",
    )(x, w_t, w_ff1)


def _retention_kernel(q_ref, k_ref, v_ref, g_ref, o_ref, state_ref, decay_ref, ktail_ref, qhead_ref):
    c = RET_CHUNK

    @pl.when(pl.program_id(1) == 0)
    def _():
        state_ref[...] = jnp.zeros_like(state_ref)
        diff = (lax.broadcasted_iota(jnp.int32, (c, c), 0)
                - lax.broadcasted_iota(jnp.int32, (c, c), 1)).astype(F32)
        pos = lax.broadcasted_iota(jnp.int32, (c, RET_DIM), 0).astype(F32)
        for h in range(RET_HEADS):
            log_g = math.log1p(-(2.0 ** (-5.0 - h)))
            decay_ref[h] = jnp.where(diff >= 0, jnp.exp(log_g * jnp.maximum(diff, 0.0)), 0.0)
            ktail_ref[h] = jnp.exp(log_g * (c - 1.0 - pos))
            qhead_ref[h] = jnp.exp(log_g * (pos + 1.0))

    for chunk, h in [(ch, hd) for ch in range(q_ref.shape[0] // c) for hd in range(RET_HEADS)]:
        log_g = math.log1p(-(2.0 ** (-5.0 - h)))
        sl = (slice(chunk * c, (chunk + 1) * c), slice(h * RET_DIM, (h + 1) * RET_DIM))
        q = q_ref[sl].astype(F32)
        k = k_ref[sl].astype(F32) * (RET_DIM ** -0.5)
        v = v_ref[sl]
        scores = lax.dot_general(q.astype(BF16), k.astype(BF16), _NT,
                                 preferred_element_type=F32) * decay_ref[h]
        o_inner = jnp.dot(scores.astype(BF16), v, preferred_element_type=F32)
        k_tail = k * ktail_ref[h]
        kv = lax.dot_general(k_tail.astype(BF16), v, _TN, preferred_element_type=F32)
        q_head = q * qhead_ref[h]
        state = state_ref[h]
        o = o_inner + jnp.dot(q_head.astype(BF16), state.astype(BF16), preferred_element_type=F32)
        state_ref[h] = math.exp(log_g * c) * state + kv
        mu = jnp.mean(o, axis=-1, keepdims=True)
        d = o - mu
        var = jnp.mean(d * d, axis=-1, keepdims=True)
        o = d * lax.rsqrt(var + LN_EPS)
        gate = g_ref[sl].astype(F32)
        o_ref[sl] = (o * (gate * jax.nn.sigmoid(gate))).astype(o_ref.dtype)


def _retention(proj, batch, seq):
    rows = RET_CHUNKS_PER_STEP * RET_CHUNK
    n = seq // rows
    blk = lambda col: pl.BlockSpec((rows, RET_WIDTH), lambda b, i: (b * n + i, col))
    return pl.pallas_call(
        _retention_kernel,
        grid=(batch, n),
        in_specs=[blk(OFF_RQ // RET_WIDTH), blk(OFF_RK // RET_WIDTH),
                  blk(OFF_RV // RET_WIDTH), blk(OFF_RG // RET_WIDTH)],
        out_specs=pl.BlockSpec((rows, RET_WIDTH), lambda b, i: (b * n + i, 0)),
        out_shape=jax.ShapeDtypeStruct((batch * seq, RET_WIDTH), BF16),
        scratch_shapes=[pltpu.VMEM((RET_HEADS, RET_DIM, RET_DIM), F32),
                        pltpu.VMEM((RET_HEADS, RET_CHUNK, RET_CHUNK), F32),
                        pltpu.VMEM((RET_HEADS, RET_CHUNK, RET_DIM), F32),
                        pltpu.VMEM((RET_HEADS, RET_CHUNK, RET_DIM), F32)],
        compiler_params=_params(("parallel", "arbitrary")),
        name="retention",
    )(proj, proj, proj, proj)


def _gelu_tanh(x):
    return 0.5 * x * (1.0 + jnp.tanh(math.sqrt(2.0 / math.pi) * (x + 0.044715 * (x * x * x))))


def _compress_kernel(zk_ref, zv_ref, pos_ref, w1_ref, w2_ref, kc_ref, vc_ref, zf_ref):
    nsub = zf_ref.shape[0] // CMP_STRIDE
    for kv, (z_ref, o_ref) in enumerate(((zk_ref, kc_ref), (zv_ref, vc_ref))):
        zf_ref[...] = z_ref[...].astype(F32)
        hid_a = jnp.zeros((nsub, LANES), F32)
        hid_b = jnp.zeros((nsub, LANES), F32)
        for i in range(CMP_STRIDE):
            rows = zf_ref[pl.ds(i, nsub, stride=CMP_STRIDE), :]
            xa = (rows + pos_ref[kv, i:i + 1, :]).astype(BF16)
            xb = (rows + pos_ref[kv, CMP_STRIDE + i:CMP_STRIDE + i + 1, :]).astype(BF16)
            wa = w1_ref[kv, i * NSA_DIM:(i + 1) * NSA_DIM, :]
            wb = w1_ref[kv, (CMP_STRIDE + i) * NSA_DIM:(CMP_STRIDE + i + 1) * NSA_DIM, :]
            hid_a = hid_a + jnp.dot(xa, wa, preferred_element_type=F32)
            hid_b = hid_b + jnp.dot(xb, wb, preferred_element_type=F32)
        hid = _gelu_tanh(hid_a + pltpu.roll(hid_b, nsub - 1, 0))
        o_ref[...] = jnp.dot(hid.astype(BF16), w2_ref[kv], preferred_element_type=F32).astype(o_ref.dtype)


def _compress(proj, cmp_pos, cmp_w1, cmp_w2, batch, seq):
    nsub = seq // CMP_STRIDE
    zspec = lambda col0: pl.BlockSpec((seq, NSA_DIM), lambda b, g: (b, col0 + g))
    full = lambda a: pl.BlockSpec(a.shape, lambda b, g: (0,) * a.ndim)
    ospec = pl.BlockSpec((None, None, nsub, NSA_DIM), lambda b, g: (b, g, 0, 0))
    oshape = jax.ShapeDtypeStruct((batch, NSA_KV_GROUPS, nsub, NSA_DIM), BF16)
    return pl.pallas_call(
        _compress_kernel,
        grid=(batch, NSA_KV_GROUPS),
        in_specs=[zspec(OFF_KC // NSA_DIM), zspec(OFF_VC // NSA_DIM),
                  full(cmp_pos), full(cmp_w1), full(cmp_w2)],
        out_specs=[ospec, ospec],
        out_shape=[oshape, oshape],
        scratch_shapes=[pltpu.VMEM((seq, NSA_DIM), F32)],
        compiler_params=_params(("parallel", "parallel")),
        name="compress",
    )(proj, proj, cmp_pos, cmp_w1, cmp_w2)


def _nsa_kernel(q_ref, kc_ref, vc_ref, ks_ref, vs_ref, kw_ref, vw_ref, gate_ref, wo_ref, w2_ref,
                o_ref, wob_ref, w2b_ref,
                q4_ref, slope_ref, vct_ref, vst_ref, vwt_ref, sig_ref, m_ref, l_ref, acc_ref, out_ref):
    tq, tk, rep, groups = NSA_TQ, NSA_TK, NSA_REP, NSA_KV_GROUPS
    cols = rep * tq
    n_tiles = ks_ref.shape[0] // tk
    n_blocks = ks_ref.shape[0] // SLC_LEN
    n_cmp = kc_ref.shape[1]
    slc, win = 0, 1
    i = pl.program_id(1)
    t0 = i * tq
    gcols = lambda g: slice(g * NSA_DIM, (g + 1) * NSA_DIM)

    wob_ref[...] = wo_ref[...].astype(BF16)
    w2b_ref[...] = w2_ref[...].astype(BF16)

    pos_lane, start_lane = n_blocks, n_blocks + 2 * SLOPE_TERMS

    def alibi_slope(head):
        return LOG2E * jnp.exp((head + 1).astype(F32) * (-0.5 * math.log(2.0)))

    @pl.when(i == 0)
    def _():
        for g in range(groups):
            vct_ref[g] = vc_ref[g].astype(F32).T.astype(BF16)
        row_head = lax.broadcasted_iota(jnp.int32, (cols, NSA_DIM), 0) >> (tq.bit_length() - 1)
        row_lane = lax.broadcasted_iota(jnp.int32, (cols, NSA_DIM), 1)
        slope_term = jnp.where(row_lane < start_lane, (row_lane - pos_lane) >> 1, row_lane - start_lane)
        for g in range(groups):
            rest = alibi_slope(g * rep + row_head)
            terms = jnp.zeros((cols, NSA_DIM), F32)
            for n in range(SLOPE_TERMS):
                term = rest.astype(BF16).astype(F32)
                rest = rest - term
                terms = jnp.where((row_lane >= pos_lane) & (slope_term == n), term, terms)
            slope_ref[g] = terms.astype(BF16)

        def transpose_tile(t, carry):
            rows = pl.ds(pl.multiple_of(t * tk, tk), tk)
            for g in range(groups):
                vst_ref[g, t] = vs_ref[rows, gcols(g)].astype(F32).T.astype(BF16)
                vwt_ref[g, t] = vw_ref[rows, gcols(g)].astype(F32).T.astype(BF16)
            return carry

        lax.fori_loop(0, n_tiles, transpose_tile, 0)

    sub = lax.broadcasted_iota(jnp.int32, (tk, cols), 0)
    t_loc = lax.broadcasted_iota(jnp.int32, (tk, cols), 1) & (tq - 1)
    lane_head = lax.broadcasted_iota(jnp.int32, (1, cols), 1) >> (tq.bit_length() - 1)
    causal = sub <= t_loc
    c_end = lax.broadcasted_iota(jnp.int32, (n_cmp, cols), 0) * CMP_STRIDE + (CMP_LEN - 1)
    c_mask = c_end <= t0 + (lax.broadcasted_iota(jnp.int32, (n_cmp, cols), 1) & (tq - 1))
    s_idx = lax.broadcasted_iota(jnp.int32, (n_blocks, n_cmp), 0)
    c_idx = lax.broadcasted_iota(jnp.int32, (n_blocks, n_cmp), 1)
    overlap = jnp.maximum(jnp.minimum(c_idx * CMP_STRIDE + CMP_LEN, s_idx * SLC_LEN + SLC_LEN)
                          - jnp.maximum(c_idx * CMP_STRIDE, s_idx * SLC_LEN), 0)
    overlap = (overlap.astype(F32) * (1.0 / CMP_LEN)).astype(BF16)
    blk = lax.broadcasted_iota(jnp.int32, (n_blocks, tq), 0)
    cur = (t0 + lax.broadcasted_iota(jnp.int32, (n_blocks, tq), 1)) >> (SLC_LEN.bit_length() - 1)
    forced = (blk == 0) | (blk == cur) | (blk == cur - 1)
    future = blk > cur

    sig_ref[...] = jax.nn.sigmoid(gate_ref[...]).T

    def gate_row(g, branch):
        first = branch * NSA_HEADS + g * rep
        return jnp.concatenate([sig_ref[first + r:first + r + 1, :] for r in range(rep)], axis=1)

    pos = lax.broadcasted_iota(jnp.int32, (tk, NSA_DIM), 0)
    key_lane = lax.broadcasted_iota(jnp.int32, (tk, NSA_DIM), 1)
    in_pos = (key_lane >= pos_lane) & (key_lane < start_lane)
    in_start = (key_lane >= start_lane) & (key_lane < start_lane + SLOPE_TERMS)
    pos_terms = jnp.where(((key_lane - pos_lane) & 1) == 0, (pos // SLC_LEN) * SLC_LEN, pos % SLC_LEN)
    pos_terms = jnp.where(in_pos, pos_terms, 0).astype(F32)

    def key_cols(kt, selected):
        start = kt * tk
        extra = pos_terms + jnp.where(in_start, start.astype(F32), 0.0)
        if selected:
            extra = extra + jnp.where(key_lane == ((start + pos) >> (SLC_LEN.bit_length() - 1)), 1.0, 0.0)
        return extra.astype(BF16)

    bias_lane = lax.broadcasted_iota(jnp.int32, (cols, NSA_DIM), 1) < n_blocks

    for g in range(groups):
        q4 = jnp.concatenate([q_ref[:, (g * rep + r) * NSA_DIM:(g * rep + r + 1) * NSA_DIM]
                              for r in range(rep)], axis=0)
        q4 = (q4.astype(F32) * (LOG2E * NSA_DIM ** -0.5)).astype(BF16)
        slope = alibi_slope(g * rep + lane_head)

        st = lax.dot_general(kc_ref[g], q4, _NT, preferred_element_type=F32)
        st = jnp.where(c_mask, st + slope * c_end.astype(F32), MASKED)
        e = jnp.exp2(st - jnp.max(st, axis=0, keepdims=True))
        p_cmp = jnp.where(c_mask, e * (1.0 / jnp.sum(e, axis=0, keepdims=True)), 0.0)
        o_cmp = jnp.dot(vct_ref[g], p_cmp.astype(BF16), preferred_element_type=F32)
        out_ref[g] = gate_row(g, 0) * o_cmp

        p_sum = p_cmp[:, 0:tq]
        for r in range(1, rep):
            p_sum = p_sum + p_cmp[:, r * tq:(r + 1) * tq]
        p_hi = p_sum.astype(BF16)
        p_lo = (p_sum - p_hi.astype(F32)).astype(BF16)
        imp = (jnp.dot(overlap, p_hi, preferred_element_type=F32)
               + jnp.dot(overlap, p_lo, preferred_element_type=F32))
        imp = jnp.where(forced, FORCE, jnp.where(future, -FORCE, imp))
        rank = jnp.zeros((n_blocks, tq), F32)
        for j in range(n_blocks):
            row = imp[j:j + 1, :]
            before = (row > imp) | ((row == imp) & (blk > j))
            rank = rank + jnp.where(before, 1.0, 0.0)
        sel_bias = jnp.where(rank < SLC_TOPK, 0.0, MASKED)

        sel_t = jnp.concatenate([sel_bias, jnp.zeros((NSA_DIM - n_blocks, tq), F32)], axis=0).T
        extra = jnp.where(bias_lane, jnp.concatenate([sel_t] * rep, axis=0), slope_ref[g].astype(F32))
        q4_ref[g] = jnp.concatenate([q4, extra.astype(BF16)], axis=1)

    def scores(g, k, extra):
        k = jnp.concatenate([k, extra], axis=1)
        return lax.dot_general(k, q4_ref[g], _NT, preferred_element_type=F32)

    def update(branch, g, st, vt, keep, first, after):
        if after is not None:
            vt = vt + (0.0 * after[0:1, 0:tk]).astype(BF16)
        if keep is not None:
            st = jnp.where(keep, st, MASKED)
        m_tile = jnp.max(st, axis=0, keepdims=True)
        if first:
            p = jnp.exp2(st - m_tile)
            m_ref[branch, g] = m_tile
            l_ref[branch, g] = jnp.sum(p, axis=0, keepdims=True)
            acc_ref[branch, g] = jnp.dot(vt, p.astype(BF16), preferred_element_type=F32)
        else:
            m_old = m_ref[branch, g]
            m_new = jnp.maximum(m_old, m_tile)
            a = jnp.exp2(m_old - m_new)
            p = jnp.exp2(st - m_new)
            m_ref[branch, g] = m_new
            l_ref[branch, g] = a * l_ref[branch, g] + jnp.sum(p, axis=0, keepdims=True)
            acc_ref[branch, g] = a * acc_ref[branch, g] + jnp.dot(vt, p.astype(BF16), preferred_element_type=F32)

    def slc_chains(kt):
        rows = pl.ds(pl.multiple_of(kt * tk, tk), tk)
        extra = key_cols(kt, True)
        return [(slc, g, ks_ref[rows, gcols(g)], extra, vst_ref[g, kt]) for g in range(groups)]

    def win_chains(kt):
        rows = pl.ds(pl.multiple_of(kt * tk, tk), tk)
        extra = key_cols(kt, False)
        return [(win, g, kw_ref[rows, gcols(g)], extra, vwt_ref[g, kt]) for g in range(groups)]

    def run(chains, keep, first):
        sts = [scores(g, k, extra) for (_, g, k, extra, _) in chains]
        for n, (branch, g, _, _, vt) in enumerate(chains):
            update(branch, g, sts[n], vt, keep, first, sts[n + 1] if n + 1 < len(sts) else None)

    run(slc_chains(i) + win_chains(i), causal, True)

    def slc_body(kt, carry):
        run(slc_chains(kt), None, False)
        return carry

    lax.fori_loop(0, i, slc_body, 0)

    n_back = WIN_LEN // tk

    @pl.when(i >= n_back)
    def _():
        run(win_chains(i - n_back), sub > t_loc, False)

    def win_body(kt, carry):
        run(win_chains(kt), None, False)
        return carry

    lax.fori_loop(jnp.maximum(i - (n_back - 1), 0), i, win_body, 0)

    for g in range(groups):
        o = out_ref[g]
        for branch, state in ((1, slc), (2, win)):
            o = o + (gate_row(g, branch) * (1.0 / l_ref[state, g])) * acc_ref[state, g]
        for r in range(rep):
            h = g * rep + r
            o_ref[:, h * NSA_DIM:(h + 1) * NSA_DIM] = o[:, r * tq:(r + 1) * tq].T.astype(o_ref.dtype)


def _nsa(proj, gates, kc, vc, w_out, w_ff2, layer, batch, seq):
    nq = seq // NSA_TQ
    step = lambda b, i: b * nq + i
    wo_in, wo_out, wo_shape = _cast_slab_specs(w_out, layer, batch * nq, step)
    w2_in, w2_out, w2_shape = _cast_slab_specs(w_ff2, layer, batch * nq, step, FFN_W2_PARTS[2],
                                               sum(FFN_W2_PARTS[:2]) // FFN_W2_PARTS[2])
    groups, cols = NSA_KV_GROUPS, NSA_REP * NSA_TQ
    n_tiles = seq // NSA_TK
    kvspec = lambda off: pl.BlockSpec((seq, NSA_KV_WIDTH), lambda b, i: (b, off // NSA_KV_WIDTH))
    cspec = lambda a: pl.BlockSpec((None,) + a.shape[1:], lambda b, i: (b, 0, 0, 0))
    return pl.pallas_call(
        _nsa_kernel,
        grid=(batch, nq),
        in_specs=[pl.BlockSpec((NSA_TQ, NSA_WIDTH), lambda b, i: (b * nq + i, OFF_NQ // NSA_WIDTH)),
                  cspec(kc), cspec(vc),
                  kvspec(OFF_KS), kvspec(OFF_VS), kvspec(OFF_KW), kvspec(OFF_VW),
                  pl.BlockSpec((NSA_TQ, LANES), lambda b, i: (b * nq + i, 0)), wo_in, w2_in],
        out_specs=[pl.BlockSpec((NSA_TQ, NSA_WIDTH), lambda b, i: (b * nq + i, 0)), wo_out, w2_out],
        out_shape=[jax.ShapeDtypeStruct((batch * seq, NSA_WIDTH), BF16), wo_shape, w2_shape],
        scratch_shapes=[pltpu.VMEM((groups, cols, 2 * NSA_DIM), BF16),
                        pltpu.VMEM((groups, cols, NSA_DIM), BF16),
                        pltpu.VMEM((groups, NSA_DIM, seq // CMP_STRIDE), BF16),
                        pltpu.VMEM((groups, n_tiles, NSA_DIM, NSA_TK), BF16),
                        pltpu.VMEM((groups, n_tiles, NSA_DIM, NSA_TK), BF16),
                        pltpu.VMEM((LANES, NSA_TQ), F32),
                        pltpu.VMEM((2, groups, 1, cols), F32),
                        pltpu.VMEM((2, groups, 1, cols), F32),
                        pltpu.VMEM((2, groups, NSA_DIM, cols), F32),
                        pltpu.VMEM((groups, NSA_DIM, cols), F32)],
        compiler_params=_params(("parallel", "arbitrary")),
        name="nsa",
    )(proj, kc, vc, proj, proj, proj, proj, gates, w_out, w_ff2)


def _outproj_kernel(ret_ref, nsa_ref, wr_ref, wn_ref, x_ref, g_ref, b_ref, w2_ref,
                    o_ref, w2b_ref, mu_ref, rstd_ref):
    w2b_ref[...] = w2_ref[...].astype(BF16)

    j = pl.program_id(1)
    nj = pl.num_programs(1)
    tn = x_ref.shape[1]
    y = (jnp.dot(ret_ref[...], wr_ref[...], preferred_element_type=F32)
         + jnp.dot(nsa_ref[...], wn_ref[...], preferred_element_type=F32))
    y = DN_ALPHA * x_ref[...] + y
    o_ref[:, pl.ds(pl.multiple_of(j * tn, tn), tn)] = y

    @pl.when(j == nj - 1)
    def _():
        _layer_norm_rows(o_ref, g_ref, b_ref, mu_ref, rstd_ref)


def _outproj(ret, nsa, w_out, x, g, b, w_ff2, layer, tm, tn):
    m, d = x.shape
    nj = d // tn
    vec = pl.BlockSpec((1, d), lambda i, j: (0, 0))
    w2_in, w2_out, w2_shape = _cast_slab_specs(w_ff2, layer, (m // tm) * nj, lambda i, j: i * nj + j,
                                               FFN_W2_PARTS[0], 0)
    return pl.pallas_call(
        _outproj_kernel,
        grid=(m // tm, d // tn),
        in_specs=[pl.BlockSpec((tm, RET_WIDTH), lambda i, j: (i, 0)),
                  pl.BlockSpec((tm, NSA_WIDTH), lambda i, j: (i, 0)),
                  pl.BlockSpec((RET_WIDTH, tn), lambda i, j: (0, j)),
                  pl.BlockSpec((NSA_WIDTH, tn), lambda i, j: (RET_WIDTH // NSA_WIDTH, j)),
                  pl.BlockSpec((tm, tn), lambda i, j: (i, j)),
                  vec, vec, w2_in],
        out_specs=[pl.BlockSpec((tm, d), lambda i, j: (i, 0)), w2_out],
        out_shape=[jax.ShapeDtypeStruct((m, d), F32), w2_shape],
        scratch_shapes=_row_stats(tm),
        compiler_params=_params(("parallel", "arbitrary")),
        name="outproj_ln",
    )(ret, nsa, w_out, w_out, x, g, b, w_ff2)


def _xattn_kernel(h_ref, kv_ref, wq_ref, wo_ref, g_ref, b_ref, w2_ref, o_ref, w2b_ref, mu_ref, rstd_ref):
    w2b_ref[...] = w2_ref[...].astype(BF16)

    q = jnp.dot(h_ref[...].astype(BF16), wq_ref[...], preferred_element_type=F32).astype(BF16)
    outs = []
    for hd in range(XA_HEADS):
        sl = slice(hd * XA_DIM, (hd + 1) * XA_DIM)
        k = kv_ref[:, sl]
        v = kv_ref[:, XA_WIDTH + hd * XA_DIM:XA_WIDTH + (hd + 1) * XA_DIM]
        s = lax.dot_general(q[:, sl], k, _NT, preferred_element_type=F32) * (XA_DIM ** -0.5)
        e = jnp.exp(s - jnp.max(s, axis=-1, keepdims=True))
        p = e / jnp.sum(e, axis=-1, keepdims=True)
        outs.append(jnp.dot(p.astype(BF16), v, preferred_element_type=F32))
    o = jnp.concatenate(outs, axis=-1).astype(BF16)
    o_ref[...] = DN_ALPHA * h_ref[...] + jnp.dot(o, wo_ref[...], preferred_element_type=F32)
    _layer_norm_rows(o_ref, g_ref, b_ref, mu_ref, rstd_ref)


def _xattn(h, kv, wq, wo, g, b, w_ff2, layer, seq, mem_len, tm):
    m, d = h.shape
    per_batch = seq // tm
    steps = m // tm
    full = lambda a: pl.BlockSpec(a.shape, lambda i: (0, 0), pipeline_mode=pl.Buffered(1))
    w2_in, w2_out, w2_shape = _cast_slab_specs(w_ff2, layer, steps, lambda i: i, FFN_W2_PARTS[1],
                                               FFN_W2_PARTS[0] // FFN_W2_PARTS[1])
    return pl.pallas_call(
        _xattn_kernel,
        grid=(steps,),
        in_specs=[pl.BlockSpec((tm, d), lambda i: (i, 0)),
                  pl.BlockSpec((mem_len, 2 * XA_WIDTH), lambda i: (i // per_batch, 0)),
                  full(wq), full(wo), full(g), full(b), w2_in],
        out_specs=[pl.BlockSpec((tm, d), lambda i: (i, 0)), w2_out],
        out_shape=[jax.ShapeDtypeStruct((m, d), F32), w2_shape],
        scratch_shapes=_row_stats(tm),
        compiler_params=_params(("parallel",)),
        name="xattn_ln",
    )(h, kv, wq, wo, g, b, w_ff2)


def _ffn_kernel(h_ref, w1_ref, *refs):
    n_parts = len(FFN_W2_PARTS)
    w2_refs, (g_ref, b_ref, o_ref, hb_ref, mu_ref, rstd_ref) = refs[:n_parts], refs[n_parts:]
    f = pl.program_id(1)
    nf = pl.num_programs(1)

    @pl.when(f == 0)
    def _():
        hb_ref[...] = h_ref[...].astype(BF16)
        o_ref[...] = DN_ALPHA * h_ref[...]

    u = jnp.maximum(jnp.dot(hb_ref[...], w1_ref[...], preferred_element_type=F32), 0.0)
    u = (u * u).astype(BF16)
    for c in range(o_ref.shape[1] // FFN_TN):
        cols = slice(c * FFN_TN, (c + 1) * FFN_TN)
        part = max(p for p in range(n_parts) if sum(FFN_W2_PARTS[:p]) <= c * FFN_TN)
        first = sum(FFN_W2_PARTS[:part])
        w2 = w2_refs[part][:, c * FFN_TN - first:(c + 1) * FFN_TN - first]
        o_ref[:, cols] += jnp.dot(u, w2, preferred_element_type=F32)

    @pl.when(f == nf - 1)
    def _():
        _layer_norm_rows(o_ref, g_ref, b_ref, mu_ref, rstd_ref)


def _ffn(h, w1, w2_parts, g, b, tm, tf):
    m, d = h.shape
    d_ff = w1.shape[1]
    vec = pl.BlockSpec((1, d), lambda i, f: (0, 0))
    return pl.pallas_call(
        _ffn_kernel,
        grid=(m // tm, d_ff // tf),
        in_specs=[pl.BlockSpec((tm, d), lambda i, f: (i, 0)),
                  pl.BlockSpec((d, tf), lambda i, f: (0, f)),
                  *[pl.BlockSpec((tf, w2.shape[1]), lambda i, f: (f, 0)) for w2 in w2_parts],
                  vec, vec],
        out_specs=pl.BlockSpec((tm, d), lambda i, f: (i, 0)),
        out_shape=jax.ShapeDtypeStruct((m, d), F32),
        scratch_shapes=[pltpu.VMEM((tm, d), BF16)] + _row_stats(tm),
        compiler_params=_params(("parallel", "arbitrary")),
        name="ffn_ln",
    )(h, w1, *w2_parts, g, b)


def kernel(x, mem, w_in, w_out, cmp_pos, cmp_w1, cmp_w2, xa_wq, xa_wkv, xa_wo,
           w_ff1, w_ff2, ln_g, ln_b):
    batch, seq, d = x.shape
    mem_len = mem.shape[1]
    h = x.reshape(batch * seq, d)
    mem2 = mem.reshape(batch * mem_len, d).astype(BF16)
    for l in range(DEPTH):
        w_in_t = jnp.swapaxes(w_in[l], 0, 1).astype(BF16)
        wg_t = jnp.pad(w_in_t[OFF_GATES:], ((0, LANES - GATE_WIDTH), (0, 0)))
        hb, gates = _cast_gates(h, wg_t, CAST_TM)
        proj, w_ff1_b = _inproj(hb, w_in_t, OFF_GATES, w_ff1, l, INPROJ_TM, INPROJ_TN)
        ret = _retention(proj, batch, seq)
        kc, vc = _compress(proj, cmp_pos[l], cmp_w1[l].astype(BF16), cmp_w2[l].astype(BF16),
                           batch, seq)
        sparse, w_out_b, w_ff2_c = _nsa(proj, gates, kc, vc, w_out, w_ff2, l, batch, seq)
        vecs = lambda a, k: a[l, k].reshape(1, d)
        h, w_ff2_a = _outproj(ret, sparse, w_out_b, h, vecs(ln_g, 0), vecs(ln_b, 0), w_ff2, l,
                              OUTPROJ_TM, OUTPROJ_TN)
        kv = _matmul(mem2, xa_wkv[l].astype(BF16), 2 * XA_WIDTH, KV_TM, KV_TN, BF16)
        h, w_ff2_b = _xattn(h, kv, xa_wq[l].astype(BF16), xa_wo[l].astype(BF16),
                            vecs(ln_g, 1), vecs(ln_b, 1), w_ff2, l, seq, mem_len, XATTN_TM)
        h = _ffn(h, w_ff1_b, (w_ff2_a, w_ff2_b, w_ff2_c), vecs(ln_g, 2), vecs(ln_b, 2), FFN_TM, FFN_TF)
    return h.reshape(batch, seq, d)
```

```python
import math

import jax
import jax.numpy as jnp
from jax import lax
from jax.experimental import pallas as pl
from jax.experimental.pallas import tpu as pltpu

F32 = jnp.float32
BF16 = jnp.bfloat16

RET_HEADS = 8
RET_DIM = 256
RET_WIDTH = RET_HEADS * RET_DIM
RET_CHUNK = 128
NSA_HEADS = 16
NSA_KV_GROUPS = 4
NSA_REP = NSA_HEADS // NSA_KV_GROUPS
NSA_DIM = 128
NSA_WIDTH = NSA_HEADS * NSA_DIM
NSA_KV_WIDTH = NSA_KV_GROUPS * NSA_DIM
CMP_LEN = 32
CMP_STRIDE = 16
SLC_LEN = 64
SLC_TOPK = 16
WIN_LEN = 512
GATE_WIDTH = 3 * NSA_HEADS
OFF_RQ, OFF_RK, OFF_RV, OFF_RG = 0, RET_WIDTH, 2 * RET_WIDTH, 3 * RET_WIDTH
OFF_NQ = 4 * RET_WIDTH
OFF_KC = OFF_NQ + NSA_WIDTH
OFF_VC = OFF_KC + NSA_KV_WIDTH
OFF_KS = OFF_VC + NSA_KV_WIDTH
OFF_VS = OFF_KS + NSA_KV_WIDTH
OFF_KW = OFF_VS + NSA_KV_WIDTH
OFF_VW = OFF_KW + NSA_KV_WIDTH
OFF_GATES = OFF_VW + NSA_KV_WIDTH
XA_HEADS = 4
XA_DIM = 128
XA_WIDTH = XA_HEADS * XA_DIM
LN_EPS = 1e-5
DEPTH = 1
DN_ALPHA = (2.0 * DEPTH) ** 0.25
NEG_INF = -1e30
MASKED = 2.0 * NEG_INF
FORCE = 1e9
LOG2E = math.log2(math.e)
SLOPE_TERMS = 3

LANES = 128
VMEM_LIMIT = 56 * 1024 * 1024
CAST_TM = 512
INPROJ_TM, INPROJ_TN = 1024, 1024
INPROJ_SIDE_STEPS = 8
RET_CHUNKS_PER_STEP = 4
NSA_TQ = 256
NSA_TK = 256
OUTPROJ_TM, OUTPROJ_TN = 512, 1024
KV_TM, KV_TN = 512, 1024
XATTN_TM = 256
FFN_TM, FFN_TF = 512, 512
FFN_TN = 1024
FFN_W2_PARTS = (2048, 1024, 1024)
LN_ROWS = 8
LN_STATS_UNROLL = 16
LN_APPLY_UNROLL = 4

_NT = (((1,), (1,)), ((), ()))
_TN = (((0,), (0,)), ((), ()))


def _params(sem):
    return pltpu.CompilerParams(dimension_semantics=sem, vmem_limit_bytes=VMEM_LIMIT)


def _cast_slab_specs(w, layer, steps, step_index, cols=None, col_block=0):
    rows, cols = w.shape[1] // steps, cols or w.shape[2]
    in_spec = pl.BlockSpec((None, rows, cols), lambda *ids: (layer, step_index(*ids), col_block))
    out_spec = pl.BlockSpec((rows, cols), lambda *ids: (step_index(*ids), 0))
    return in_spec, out_spec, jax.ShapeDtypeStruct((w.shape[1], cols), BF16)


def _row_stats(tm):
    return [pltpu.VMEM((tm, LANES), F32), pltpu.VMEM((tm, LANES), F32)]


def _layer_norm_rows(o_ref, g_ref, b_ref, mu_ref, rstd_ref):
    tm, d = o_ref.shape
    n = tm // LN_ROWS

    def stats(c, carry):
        rows = pl.ds(pl.multiple_of(c * LN_ROWS, LN_ROWS), LN_ROWS)
        y = o_ref[rows, :]
        mu = jnp.mean(y, axis=-1, keepdims=True)
        dev = y - mu
        rstd = lax.rsqrt(jnp.mean(dev * dev, axis=-1, keepdims=True) + LN_EPS)
        mu_ref[rows, :] = jnp.broadcast_to(mu, (LN_ROWS, LANES))
        rstd_ref[rows, :] = jnp.broadcast_to(rstd, (LN_ROWS, LANES))
        return carry

    def apply(c, carry):
        rows = pl.ds(pl.multiple_of(c * LN_ROWS, LN_ROWS), LN_ROWS)
        mu, rstd = mu_ref[rows, :], rstd_ref[rows, :]
        for j in range(d // LANES):
            cols = slice(j * LANES, (j + 1) * LANES)
            o_ref[rows, cols] = (o_ref[rows, cols] - mu) * rstd * g_ref[:, cols] + b_ref[:, cols]
        return carry

    lax.fori_loop(0, n, stats, 0, unroll=min(LN_STATS_UNROLL, n))
    lax.fori_loop(0, n, apply, 0, unroll=LN_APPLY_UNROLL)


def _mm_kernel(x_ref, w_ref, o_ref):
    o_ref[...] = jnp.dot(x_ref[...], w_ref[...], preferred_element_type=F32).astype(o_ref.dtype)


def _matmul(x, w, n_out, tm, tn, out_dtype):
    m, k = x.shape
    return pl.pallas_call(
        _mm_kernel,
        grid=(m // tm, n_out // tn),
        in_specs=[pl.BlockSpec((tm, k), lambda i, j: (i, 0)),
                  pl.BlockSpec((k, tn), lambda i, j: (0, j))],
        out_specs=pl.BlockSpec((tm, tn), lambda i, j: (i, j)),
        out_shape=jax.ShapeDtypeStruct((m, n_out), out_dtype),
        compiler_params=_params(("parallel", "parallel")),
        name="matmul",
    )(x, w)


def _cast_gates_kernel(x_ref, wgt_ref, xb_ref, g_ref):
    xb = x_ref[...].astype(BF16)
    xb_ref[...] = xb
    g_ref[...] = lax.dot_general(xb, wgt_ref[...], _NT, preferred_element_type=F32)


def _cast_gates(x, wg_t, tm):
    m, k = x.shape
    return pl.pallas_call(
        _cast_gates_kernel,
        grid=(m // tm,),
        in_specs=[pl.BlockSpec((tm, k), lambda i: (i, 0)),
                  pl.BlockSpec((LANES, k), lambda i: (0, 0), pipeline_mode=pl.Buffered(1))],
        out_specs=[pl.BlockSpec((tm, k), lambda i: (i, 0)), pl.BlockSpec((tm, LANES), lambda i: (i, 0))],
        out_shape=[jax.ShapeDtypeStruct((m, k), BF16), jax.ShapeDtypeStruct((m, LANES), F32)],
        compiler_params=_params(("parallel",)),
        name="cast_gates",
    )(x, wg_t)


def _inproj_kernel(x_ref, w_ref, w1_ref, o_ref, w1b_ref):
    @pl.when(pl.program_id(1) < INPROJ_SIDE_STEPS)
    def _():
        w1b_ref[...] = w1_ref[...].astype(BF16)

    o_ref[...] = lax.dot_general(x_ref[...], w_ref[...], _NT, preferred_element_type=F32).astype(o_ref.dtype)


def _inproj(x, w_t, n_out, w_ff1, layer, tm, tn):
    m, k = x.shape
    side = INPROJ_SIDE_STEPS
    slab = lambda i, j: i * side + jnp.minimum(j, side - 1)
    w1_in, w1_out, w1_shape = _cast_slab_specs(w_ff1, layer, (m // tm) * side, slab)
    return pl.pallas_call(
        _inproj_kernel,
        grid=(m // tm, n_out // tn),
        in_specs=[pl.BlockSpec((tm, k), lambda i, j: (i, 0)),
                  pl.BlockSpec((tn, k), lambda i, j: (j, 0)), w1_in],
        out_specs=[pl.BlockSpec((tm, tn), lambda i, j: (i, j)), w1_out],
        out_shape=[jax.ShapeDtypeStruct((m, n_out), BF16), w1_shape],
        compiler_params=_params(("parallel", "arbitrary")),
        name="inproj",
    )(x, w_t, w_ff1)


def _retention_kernel(q_ref, k_ref, v_ref, g_ref, o_ref, state_ref, decay_ref, ktail_ref, qhead_ref):
    c = RET_CHUNK

    @pl.when(pl.program_id(1) == 0)
    def _():
        state_ref[...] = jnp.zeros_like(state_ref)
        diff = (lax.broadcasted_iota(jnp.int32, (c, c), 0)
                - lax.broadcasted_iota(jnp.int32, (c, c), 1)).astype(F32)
        pos = lax.broadcasted_iota(jnp.int32, (c, RET_DIM), 0).astype(F32)
        for h in range(RET_HEADS):
            log_g = math.log1p(-(2.0 ** (-5.0 - h)))
            decay_ref[h] = jnp.where(diff >= 0, jnp.exp(log_g * jnp.maximum(diff, 0.0)), 0.0)
            ktail_ref[h] = jnp.exp(log_g * (c - 1.0 - pos))
            qhead_ref[h] = jnp.exp(log_g * (pos + 1.0))

    for chunk, h in [(ch, hd) for ch in range(q_ref.shape[0] // c) for hd in range(RET_HEADS)]:
        log_g = math.log1p(-(2.0 ** (-5.0 - h)))
        sl = (slice(chunk * c, (chunk + 1) * c), slice(h * RET_DIM, (h + 1) * RET_DIM))
        q = q_ref[sl].astype(F32)
        k = k_ref[sl].astype(F32) * (RET_DIM ** -0.5)
        v = v_ref[sl]
        scores = lax.dot_general(q.astype(BF16), k.astype(BF16), _NT,
                                 preferred_element_type=F32) * decay_ref[h]
        o_inner = jnp.dot(scores.astype(BF16), v, preferred_element_type=F32)
        k_tail = k * ktail_ref[h]
        kv = lax.dot_general(k_tail.astype(BF16), v, _TN, preferred_element_type=F32)
        q_head = q * qhead_ref[h]
        state = state_ref[h]
        o = o_inner + jnp.dot(q_head.astype(BF16), state.astype(BF16), preferred_element_type=F32)
        state_ref[h] = math.exp(log_g * c) * state + kv
        mu = jnp.mean(o, axis=-1, keepdims=True)
        d = o - mu
        var = jnp.mean(d * d, axis=-1, keepdims=True)
        o = d * lax.rsqrt(var + LN_EPS)
        gate = g_ref[sl].astype(F32)
        o_ref[sl] = (o * (gate * jax.nn.sigmoid(gate))).astype(o_ref.dtype)


def _retention(proj, batch, seq):
    rows = RET_CHUNKS_PER_STEP * RET_CHUNK
    n = seq // rows
    blk = lambda col: pl.BlockSpec((rows, RET_WIDTH), lambda b, i: (b * n + i, col))
    return pl.pallas_call(
        _retention_kernel,
        grid=(batch, n),
        in_specs=[blk(OFF_RQ // RET_WIDTH), blk(OFF_RK // RET_WIDTH),
                  blk(OFF_RV // RET_WIDTH), blk(OFF_RG // RET_WIDTH)],
        out_specs=pl.BlockSpec((rows, RET_WIDTH), lambda b, i: (b * n + i, 0)),
        out_shape=jax.ShapeDtypeStruct((batch * seq, RET_WIDTH), BF16),
        scratch_shapes=[pltpu.VMEM((RET_HEADS, RET_DIM, RET_DIM), F32),
                        pltpu.VMEM((RET_HEADS, RET_CHUNK, RET_CHUNK), F32),
                        pltpu.VMEM((RET_HEADS, RET_CHUNK, RET_DIM), F32),
                        pltpu.VMEM((RET_HEADS, RET_CHUNK, RET_DIM), F32)],
        compiler_params=_params(("parallel", "arbitrary")),
        name="retention",
    )(proj, proj, proj, proj)


def _gelu_tanh(x):
    return 0.5 * x * (1.0 + jnp.tanh(math.sqrt(2.0 / math.pi) * (x + 0.044715 * (x * x * x))))


def _compress_kernel(zk_ref, zv_ref, pos_ref, w1_ref, w2_ref, kc_ref, vc_ref, zf_ref):
    nsub = zf_ref.shape[0] // CMP_STRIDE
    for kv, (z_ref, o_ref) in enumerate(((zk_ref, kc_ref), (zv_ref, vc_ref))):
        zf_ref[...] = z_ref[...].astype(F32)
        hid_a = jnp.zeros((nsub, LANES), F32)
        hid_b = jnp.zeros((nsub, LANES), F32)
        for i in range(CMP_STRIDE):
            rows = zf_ref[pl.ds(i, nsub, stride=CMP_STRIDE), :]
            xa = (rows + pos_ref[kv, i:i + 1, :]).astype(BF16)
            xb = (rows + pos_ref[kv, CMP_STRIDE + i:CMP_STRIDE + i + 1, :]).astype(BF16)
            wa = w1_ref[kv, i * NSA_DIM:(i + 1) * NSA_DIM, :]
            wb = w1_ref[kv, (CMP_STRIDE + i) * NSA_DIM:(CMP_STRIDE + i + 1) * NSA_DIM, :]
            hid_a = hid_a + jnp.dot(xa, wa, preferred_element_type=F32)
            hid_b = hid_b + jnp.dot(xb, wb, preferred_element_type=F32)
        hid = _gelu_tanh(hid_a + pltpu.roll(hid_b, nsub - 1, 0))
        o_ref[...] = jnp.dot(hid.astype(BF16), w2_ref[kv], preferred_element_type=F32).astype(o_ref.dtype)


def _compress(proj, cmp_pos, cmp_w1, cmp_w2, batch, seq):
    nsub = seq // CMP_STRIDE
    zspec = lambda col0: pl.BlockSpec((seq, NSA_DIM), lambda b, g: (b, col0 + g))
    full = lambda a: pl.BlockSpec(a.shape, lambda b, g: (0,) * a.ndim)
    ospec = pl.BlockSpec((None, None, nsub, NSA_DIM), lambda b, g: (b, g, 0, 0))
    oshape = jax.ShapeDtypeStruct((batch, NSA_KV_GROUPS, nsub, NSA_DIM), BF16)
    return pl.pallas_call(
        _compress_kernel,
        grid=(batch, NSA_KV_GROUPS),
        in_specs=[zspec(OFF_KC // NSA_DIM), zspec(OFF_VC // NSA_DIM),
                  full(cmp_pos), full(cmp_w1), full(cmp_w2)],
        out_specs=[ospec, ospec],
        out_shape=[oshape, oshape],
        scratch_shapes=[pltpu.VMEM((seq, NSA_DIM), F32)],
        compiler_params=_params(("parallel", "parallel")),
        name="compress",
    )(proj, proj, cmp_pos, cmp_w1, cmp_w2)


def _nsa_kernel(q_ref, kc_ref, vc_ref, ks_ref, vs_ref, kw_ref, vw_ref, gate_ref, wo_ref, w2_ref,
                o_ref, wob_ref, w2b_ref,
                q4_ref, slope_ref, vct_ref, vst_ref, vwt_ref, sig_ref, m_ref, l_ref, acc_ref, out_ref):
    tq, tk, rep, groups = NSA_TQ, NSA_TK, NSA_REP, NSA_KV_GROUPS
    cols = rep * tq
    n_tiles = ks_ref.shape[0] // tk
    n_blocks = ks_ref.shape[0] // SLC_LEN
    n_cmp = kc_ref.shape[1]
    slc, win = 0, 1
    i = pl.program_id(1)
    t0 = i * tq
    gcols = lambda g: slice(g * NSA_DIM, (g + 1) * NSA_DIM)

    wob_ref[...] = wo_ref[...].astype(BF16)
    w2b_ref[...] = w2_ref[...].astype(BF16)

    pos_lane, start_lane = n_blocks, n_blocks + 2 * SLOPE_TERMS

    def alibi_slope(head):
        return LOG2E * jnp.exp((head + 1).astype(F32) * (-0.5 * math.log(2.0)))

    @pl.when(i == 0)
    def _():
        for g in range(groups):
            vct_ref[g] = vc_ref[g].astype(F32).T.astype(BF16)
        row_head = lax.broadcasted_iota(jnp.int32, (cols, NSA_DIM), 0) >> (tq.bit_length() - 1)
        row_lane = lax.broadcasted_iota(jnp.int32, (cols, NSA_DIM), 1)
        slope_term = jnp.where(row_lane < start_lane, (row_lane - pos_lane) >> 1, row_lane - start_lane)
        for g in range(groups):
            rest = alibi_slope(g * rep + row_head)
            terms = jnp.zeros((cols, NSA_DIM), F32)
            for n in range(SLOPE_TERMS):
                term = rest.astype(BF16).astype(F32)
                rest = rest - term
                terms = jnp.where((row_lane >= pos_lane) & (slope_term == n), term, terms)
            slope_ref[g] = terms.astype(BF16)

        def transpose_tile(t, carry):
            rows = pl.ds(pl.multiple_of(t * tk, tk), tk)
            for g in range(groups):
                vst_ref[g, t] = vs_ref[rows, gcols(g)].astype(F32).T.astype(BF16)
                vwt_ref[g, t] = vw_ref[rows, gcols(g)].astype(F32).T.astype(BF16)
            return carry

        lax.fori_loop(0, n_tiles, transpose_tile, 0)

    sub = lax.broadcasted_iota(jnp.int32, (tk, cols), 0)
    t_loc = lax.broadcasted_iota(jnp.int32, (tk, cols), 1) & (tq - 1)
    lane_head = lax.broadcasted_iota(jnp.int32, (1, cols), 1) >> (tq.bit_length() - 1)
    causal = sub <= t_loc
    c_end = lax.broadcasted_iota(jnp.int32, (n_cmp, cols), 0) * CMP_STRIDE + (CMP_LEN - 1)
    c_mask = c_end <= t0 + (lax.broadcasted_iota(jnp.int32, (n_cmp, cols), 1) & (tq - 1))
    s_idx = lax.broadcasted_iota(jnp.int32, (n_blocks, n_cmp), 0)
    c_idx = lax.broadcasted_iota(jnp.int32, (n_blocks, n_cmp), 1)
    overlap = jnp.maximum(jnp.minimum(c_idx * CMP_STRIDE + CMP_LEN, s_idx * SLC_LEN + SLC_LEN)
                          - jnp.maximum(c_idx * CMP_STRIDE, s_idx * SLC_LEN), 0)
    overlap = (overlap.astype(F32) * (1.0 / CMP_LEN)).astype(BF16)
    blk = lax.broadcasted_iota(jnp.int32, (n_blocks, tq), 0)
    cur = (t0 + lax.broadcasted_iota(jnp.int32, (n_blocks, tq), 1)) >> (SLC_LEN.bit_length() - 1)
    forced = (blk == 0) | (blk == cur) | (blk == cur - 1)
    future = blk > cur

    sig_ref[...] = jax.nn.sigmoid(gate_ref[...]).T

    def gate_row(g, branch):
        first = branch * NSA_HEADS + g * rep
        return jnp.concatenate([sig_ref[first + r:first + r + 1, :] for r in range(rep)], axis=1)

    pos = lax.broadcasted_iota(jnp.int32, (tk, NSA_DIM), 0)
    key_lane = lax.broadcasted_iota(jnp.int32, (tk, NSA_DIM), 1)
    in_pos = (key_lane >= pos_lane) & (key_lane < start_lane)
    in_start = (key_lane >= start_lane) & (key_lane < start_lane + SLOPE_TERMS)
    pos_terms = jnp.where(((key_lane - pos_lane) & 1) == 0, (pos // SLC_LEN) * SLC_LEN, pos % SLC_LEN)
    pos_terms = jnp.where(in_pos, pos_terms, 0).astype(F32)

    def key_cols(kt, selected):
        start = kt * tk
        extra = pos_terms + jnp.where(in_start, start.astype(F32), 0.0)
        if selected:
            extra = extra + jnp.where(key_lane == ((start + pos) >> (SLC_LEN.bit_length() - 1)), 1.0, 0.0)
        return extra.astype(BF16)

    bias_lane = lax.broadcasted_iota(jnp.int32, (cols, NSA_DIM), 1) < n_blocks

    for g in range(groups):
        q4 = jnp.concatenate([q_ref[:, (g * rep + r) * NSA_DIM:(g * rep + r + 1) * NSA_DIM]
                              for r in range(rep)], axis=0)
        q4 = (q4.astype(F32) * (LOG2E * NSA_DIM ** -0.5)).astype(BF16)
        slope = alibi_slope(g * rep + lane_head)

        st = lax.dot_general(kc_ref[g], q4, _NT, preferred_element_type=F32)
        st = jnp.where(c_mask, st + slope * c_end.astype(F32), MASKED)
        e = jnp.exp2(st - jnp.max(st, axis=0, keepdims=True))
        p_cmp = jnp.where(c_mask, e * (1.0 / jnp.sum(e, axis=0, keepdims=True)), 0.0)
        o_cmp = jnp.dot(vct_ref[g], p_cmp.astype(BF16), preferred_element_type=F32)
        out_ref[g] = gate_row(g, 0) * o_cmp

        p_sum = p_cmp[:, 0:tq]
        for r in range(1, rep):
            p_sum = p_sum + p_cmp[:, r * tq:(r + 1) * tq]
        p_hi = p_sum.astype(BF16)
        p_lo = (p_sum - p_hi.astype(F32)).astype(BF16)
        imp = (jnp.dot(overlap, p_hi, preferred_element_type=F32)
               + jnp.dot(overlap, p_lo, preferred_element_type=F32))
        imp = jnp.where(forced, FORCE, jnp.where(future, -FORCE, imp))
        rank = jnp.zeros((n_blocks, tq), F32)
        for j in range(n_blocks):
            row = imp[j:j + 1, :]
            before = (row > imp) | ((row == imp) & (blk > j))
            rank = rank + jnp.where(before, 1.0, 0.0)
        sel_bias = jnp.where(rank < SLC_TOPK, 0.0, MASKED)

        sel_t = jnp.concatenate([sel_bias, jnp.zeros((NSA_DIM - n_blocks, tq), F32)], axis=0).T
        extra = jnp.where(bias_lane, jnp.concatenate([sel_t] * rep, axis=0), slope_ref[g].astype(F32))
        q4_ref[g] = jnp.concatenate([q4, extra.astype(BF16)], axis=1)

    def scores(g, k, extra):
        k = jnp.concatenate([k, extra], axis=1)
        return lax.dot_general(k, q4_ref[g], _NT, preferred_element_type=F32)

    def update(branch, g, st, vt, keep, first, after):
        if after is not None:
            vt = vt + (0.0 * after[0:1, 0:tk]).astype(BF16)
        if keep is not None:
            st = jnp.where(keep, st, MASKED)
        m_tile = jnp.max(st, axis=0, keepdims=True)
        if first:
            p = jnp.exp2(st - m_tile)
            m_ref[branch, g] = m_tile
            l_ref[branch, g] = jnp.sum(p, axis=0, keepdims=True)
            acc_ref[branch, g] = jnp.dot(vt, p.astype(BF16), preferred_element_type=F32)
        else:
            m_old = m_ref[branch, g]
            m_new = jnp.maximum(m_old, m_tile)
            a = jnp.exp2(m_old - m_new)
            p = jnp.exp2(st - m_new)
            m_ref[branch, g] = m_new
            l_ref[branch, g] = a * l_ref[branch, g] + jnp.sum(p, axis=0, keepdims=True)
            acc_ref[branch, g] = a * acc_ref[branch, g] + jnp.dot(vt, p.astype(BF16), preferred_element_type=F32)

    def slc_chains(kt):
        rows = pl.ds(pl.multiple_of(kt * tk, tk), tk)
        extra = key_cols(kt, True)
        return [(slc, g, ks_ref[rows, gcols(g)], extra, vst_ref[g, kt]) for g in range(groups)]

    def win_chains(kt):
        rows = pl.ds(pl.multiple_of(kt * tk, tk), tk)
        extra = key_cols(kt, False)
        return [(win, g, kw_ref[rows, gcols(g)], extra, vwt_ref[g, kt]) for g in range(groups)]

    def run(chains, keep, first):
        sts = [scores(g, k, extra) for (_, g, k, extra, _) in chains]
        for n, (branch, g, _, _, vt) in enumerate(chains):
            update(branch, g, sts[n], vt, keep, first, sts[n + 1] if n + 1 < len(sts) else None)

    run(slc_chains(i) + win_chains(i), causal, True)

    def slc_body(kt, carry):
        run(slc_chains(kt), None, False)
        return carry

    lax.fori_loop(0, i, slc_body, 0)

    n_back = WIN_LEN // tk

    @pl.when(i >= n_back)
    def _():
        run(win_chains(i - n_back), sub > t_loc, False)

    def win_body(kt, carry):
        run(win_chains(kt), None, False)
        return carry

    lax.fori_loop(jnp.maximum(i - (n_back - 1), 0), i, win_body, 0)

    for g in range(groups):
        o = out_ref[g]
        for branch, state in ((1, slc), (2, win)):
            o = o + (gate_row(g, branch) * (1.0 / l_ref[state, g])) * acc_ref[state, g]
        for r in range(rep):
            h = g * rep + r
            o_ref[:, h * NSA_DIM:(h + 1) * NSA_DIM] = o[:, r * tq:(r + 1) * tq].T.astype(o_ref.dtype)


def _nsa(proj, gates, kc, vc, w_out, w_ff2, layer, batch, seq):
    nq = seq // NSA_TQ
    step = lambda b, i: b * nq + i
    wo_in, wo_out, wo_shape = _cast_slab_specs(w_out, layer, batch * nq, step)
    w2_in, w2_out, w2_shape = _cast_slab_specs(w_ff2, layer, batch * nq, step, FFN_W2_PARTS[2],
                                               sum(FFN_W2_PARTS[:2]) // FFN_W2_PARTS[2])
    groups, cols = NSA_KV_GROUPS, NSA_REP * NSA_TQ
    n_tiles = seq // NSA_TK
    kvspec = lambda off: pl.BlockSpec((seq, NSA_KV_WIDTH), lambda b, i: (b, off // NSA_KV_WIDTH))
    cspec = lambda a: pl.BlockSpec((None,) + a.shape[1:], lambda b, i: (b, 0, 0, 0))
    return pl.pallas_call(
        _nsa_kernel,
        grid=(batch, nq),
        in_specs=[pl.BlockSpec((NSA_TQ, NSA_WIDTH), lambda b, i: (b * nq + i, OFF_NQ // NSA_WIDTH)),
                  cspec(kc), cspec(vc),
                  kvspec(OFF_KS), kvspec(OFF_VS), kvspec(OFF_KW), kvspec(OFF_VW),
                  pl.BlockSpec((NSA_TQ, LANES), lambda b, i: (b * nq + i, 0)), wo_in, w2_in],
        out_specs=[pl.BlockSpec((NSA_TQ, NSA_WIDTH), lambda b, i: (b * nq + i, 0)), wo_out, w2_out],
        out_shape=[jax.ShapeDtypeStruct((batch * seq, NSA_WIDTH), BF16), wo_shape, w2_shape],
        scratch_shapes=[pltpu.VMEM((groups, cols, 2 * NSA_DIM), BF16),
                        pltpu.VMEM((groups, cols, NSA_DIM), BF16),
                        pltpu.VMEM((groups, NSA_DIM, seq // CMP_STRIDE), BF16),
                        pltpu.VMEM((groups, n_tiles, NSA_DIM, NSA_TK), BF16),
                        pltpu.VMEM((groups, n_tiles, NSA_DIM, NSA_TK), BF16),
                        pltpu.VMEM((LANES, NSA_TQ), F32),
                        pltpu.VMEM((2, groups, 1, cols), F32),
                        pltpu.VMEM((2, groups, 1, cols), F32),
                        pltpu.VMEM((2, groups, NSA_DIM, cols), F32),
                        pltpu.VMEM((groups, NSA_DIM, cols), F32)],
        compiler_params=_params(("parallel", "arbitrary")),
        name="nsa",
    )(proj, kc, vc, proj, proj, proj, proj, gates, w_out, w_ff2)


def _outproj_kernel(ret_ref, nsa_ref, wr_ref, wn_ref, x_ref, g_ref, b_ref, w2_ref,
                    o_ref, w2b_ref, mu_ref, rstd_ref):
    w2b_ref[...] = w2_ref[...].astype(BF16)

    j = pl.program_id(1)
    nj = pl.num_programs(1)
    tn = x_ref.shape[1]
    y = (jnp.dot(ret_ref[...], wr_ref[...], preferred_element_type=F32)
         + jnp.dot(nsa_ref[...], wn_ref[...], preferred_element_type=F32))
    y = DN_ALPHA * x_ref[...] + y
    o_ref[:, pl.ds(pl.multiple_of(j * tn, tn), tn)] = y

    @pl.when(j == nj - 1)
    def _():
        _layer_norm_rows(o_ref, g_ref, b_ref, mu_ref, rstd_ref)


def _outproj(ret, nsa, w_out, x, g, b, w_ff2, layer, tm, tn):
    m, d = x.shape
    nj = d // tn
    vec = pl.BlockSpec((1, d), lambda i, j: (0, 0))
    w2_in, w2_out, w2_shape = _cast_slab_specs(w_ff2, layer, (m // tm) * nj, lambda i, j: i * nj + j,
                                               FFN_W2_PARTS[1], FFN_W2_PARTS[0] // FFN_W2_PARTS[1])
    return pl.pallas_call(
        _outproj_kernel,
        grid=(m // tm, d // tn),
        in_specs=[pl.BlockSpec((tm, RET_WIDTH), lambda i, j: (i, 0)),
                  pl.BlockSpec((tm, NSA_WIDTH), lambda i, j: (i, 0)),
                  pl.BlockSpec((RET_WIDTH, tn), lambda i, j: (0, j)),
                  pl.BlockSpec((NSA_WIDTH, tn), lambda i, j: (RET_WIDTH // NSA_WIDTH, j)),
                  pl.BlockSpec((tm, tn), lambda i, j: (i, j)),
                  vec, vec, w2_in],
        out_specs=[pl.BlockSpec((tm, d), lambda i, j: (i, 0)), w2_out],
        out_shape=[jax.ShapeDtypeStruct((m, d), F32), w2_shape],
        scratch_shapes=_row_stats(tm),
        compiler_params=_params(("parallel", "arbitrary")),
        name="outproj_ln",
    )(ret, nsa, w_out, w_out, x, g, b, w_ff2)


def _xattn_kernel(h_ref, kv_ref, wq_ref, wo_ref, g_ref, b_ref, w2_ref, o_ref, w2b_ref, mu_ref, rstd_ref):
    w2b_ref[...] = w2_ref[...].astype(BF16)

    q = jnp.dot(h_ref[...].astype(BF16), wq_ref[...], preferred_element_type=F32).astype(BF16)
    outs = []
    for hd in range(XA_HEADS):
        sl = slice(hd * XA_DIM, (hd + 1) * XA_DIM)
        k = kv_ref[:, sl]
        v = kv_ref[:, XA_WIDTH + hd * XA_DIM:XA_WIDTH + (hd + 1) * XA_DIM]
        s = lax.dot_general(q[:, sl], k, _NT, preferred_element_type=F32) * (XA_DIM ** -0.5)
        e = jnp.exp(s - jnp.max(s, axis=-1, keepdims=True))
        p = e / jnp.sum(e, axis=-1, keepdims=True)
        outs.append(jnp.dot(p.astype(BF16), v, preferred_element_type=F32))
    o = jnp.concatenate(outs, axis=-1).astype(BF16)
    o_ref[...] = DN_ALPHA * h_ref[...] + jnp.dot(o, wo_ref[...], preferred_element_type=F32)
    _layer_norm_rows(o_ref, g_ref, b_ref, mu_ref, rstd_ref)


def _xattn(h, kv, wq, wo, g, b, w_ff2, layer, seq, mem_len, tm):
    m, d = h.shape
    per_batch = seq // tm
    steps = m // tm
    full = lambda a: pl.BlockSpec(a.shape, lambda i: (0, 0), pipeline_mode=pl.Buffered(1))
    w2_in, w2_out, w2_shape = _cast_slab_specs(w_ff2, layer, steps, lambda i: i, FFN_W2_PARTS[0], 0)
    return pl.pallas_call(
        _xattn_kernel,
        grid=(steps,),
        in_specs=[pl.BlockSpec((tm, d), lambda i: (i, 0)),
                  pl.BlockSpec((mem_len, 2 * XA_WIDTH), lambda i: (i // per_batch, 0)),
                  full(wq), full(wo), full(g), full(b), w2_in],
        out_specs=[pl.BlockSpec((tm, d), lambda i: (i, 0)), w2_out],
        out_shape=[jax.ShapeDtypeStruct((m, d), F32), w2_shape],
        scratch_shapes=_row_stats(tm),
        compiler_params=_params(("parallel",)),
        name="xattn_ln",
    )(h, kv, wq, wo, g, b, w_ff2)


def _ffn_kernel(h_ref, w1_ref, *refs):
    n_parts = len(FFN_W2_PARTS)
    w2_refs, (g_ref, b_ref, o_ref, hb_ref, mu_ref, rstd_ref) = refs[:n_parts], refs[n_parts:]
    f = pl.program_id(1)
    nf = pl.num_programs(1)

    @pl.when(f == 0)
    def _():
        hb_ref[...] = h_ref[...].astype(BF16)
        o_ref[...] = DN_ALPHA * h_ref[...]

    u = jnp.maximum(jnp.dot(hb_ref[...], w1_ref[...], preferred_element_type=F32), 0.0)
    u = (u * u).astype(BF16)
    for c in range(o_ref.shape[1] // FFN_TN):
        cols = slice(c * FFN_TN, (c + 1) * FFN_TN)
        part = max(p for p in range(n_parts) if sum(FFN_W2_PARTS[:p]) <= c * FFN_TN)
        first = sum(FFN_W2_PARTS[:part])
        w2 = w2_refs[part][:, c * FFN_TN - first:(c + 1) * FFN_TN - first]
        o_ref[:, cols] += jnp.dot(u, w2, preferred_element_type=F32)

    @pl.when(f == nf - 1)
    def _():
        _layer_norm_rows(o_ref, g_ref, b_ref, mu_ref, rstd_ref)


def _ffn(h, w1, w2_parts, g, b, tm, tf):
    m, d = h.shape
    d_ff = w1.shape[1]
    vec = pl.BlockSpec((1, d), lambda i, f: (0, 0))
    return pl.pallas_call(
        _ffn_kernel,
        grid=(m // tm, d_ff // tf),
        in_specs=[pl.BlockSpec((tm, d), lambda i, f: (i, 0)),
                  pl.BlockSpec((d, tf), lambda i, f: (0, f)),
                  *[pl.BlockSpec((tf, w2.shape[1]), lambda i, f: (f, 0)) for w2 in w2_parts],
                  vec, vec],
        out_specs=pl.BlockSpec((tm, d), lambda i, f: (i, 0)),
        out_shape=jax.ShapeDtypeStruct((m, d), F32),
        scratch_shapes=[pltpu.VMEM((tm, d), BF16)] + _row_stats(tm),
        compiler_params=_params(("parallel", "arbitrary")),
        name="ffn_ln",
    )(h, w1, *w2_parts, g, b)


def kernel(x, mem, w_in, w_out, cmp_pos, cmp_w1, cmp_w2, xa_wq, xa_wkv, xa_wo,
           w_ff1, w_ff2, ln_g, ln_b):
    batch, seq, d = x.shape
    mem_len = mem.shape[1]
    h = x.reshape(batch * seq, d)
    mem2 = mem.reshape(batch * mem_len, d).astype(BF16)
    for l in range(DEPTH):
        w_in_t = jnp.swapaxes(w_in[l], 0, 1).astype(BF16)
        wg_t = jnp.pad(w_in_t[OFF_GATES:], ((0, LANES - GATE_WIDTH), (0, 0)))
        hb, gates = _cast_gates(h, wg_t, CAST_TM)
        proj, w_ff1_b = _inproj(hb, w_in_t, OFF_GATES, w_ff1, l, INPROJ_TM, INPROJ_TN)
        ret = _retention(proj, batch, seq)
        kc, vc = _compress(proj, cmp_pos[l], cmp_w1[l].astype(BF16), cmp_w2[l].astype(BF16),
                           batch, seq)
        sparse, w_out_b, w_ff2_c = _nsa(proj, gates, kc, vc, w_out, w_ff2, l, batch, seq)
        vecs = lambda a, k: a[l, k].reshape(1, d)
        h, w_ff2_b = _outproj(ret, sparse, w_out_b, h, vecs(ln_g, 0), vecs(ln_b, 0), w_ff2, l,
                              OUTPROJ_TM, OUTPROJ_TN)
        kv = _matmul(mem2, xa_wkv[l].astype(BF16), 2 * XA_WIDTH, KV_TM, KV_TN, BF16)
        h, w_ff2_a = _xattn(h, kv, xa_wq[l].astype(BF16), xa_wo[l].astype(BF16),
                            vecs(ln_g, 1), vecs(ln_b, 1), w_ff2, l, seq, mem_len, XATTN_TM)
        h = _ffn(h, w_ff1_b, (w_ff2_a, w_ff2_b, w_ff2_c), vecs(ln_g, 2), vecs(ln_b, 2), FFN_TM, FFN_TF)
    return h.reshape(batch, seq, d)
```

```python
import math

import jax
import jax.numpy as jnp
from jax import lax
from jax.experimental import pallas as pl
from jax.experimental.pallas import tpu as pltpu

F32 = jnp.float32
BF16 = jnp.bfloat16

RET_HEADS = 8
RET_DIM = 256
RET_WIDTH = RET_HEADS * RET_DIM
RET_CHUNK = 128
NSA_HEADS = 16
NSA_KV_GROUPS = 4
NSA_REP = NSA_HEADS // NSA_KV_GROUPS
NSA_DIM = 128
NSA_WIDTH = NSA_HEADS * NSA_DIM
NSA_KV_WIDTH = NSA_KV_GROUPS * NSA_DIM
CMP_LEN = 32
CMP_STRIDE = 16
SLC_LEN = 64
SLC_TOPK = 16
WIN_LEN = 512
GATE_WIDTH = 3 * NSA_HEADS
OFF_RQ, OFF_RK, OFF_RV, OFF_RG = 0, RET_WIDTH, 2 * RET_WIDTH, 3 * RET_WIDTH
OFF_NQ = 4 * RET_WIDTH
OFF_KC = OFF_NQ + NSA_WIDTH
OFF_VC = OFF_KC + NSA_KV_WIDTH
OFF_KS = OFF_VC + NSA_KV_WIDTH
OFF_VS = OFF_KS + NSA_KV_WIDTH
OFF_KW = OFF_VS + NSA_KV_WIDTH
OFF_VW = OFF_KW + NSA_KV_WIDTH
OFF_GATES = OFF_VW + NSA_KV_WIDTH
XA_HEADS = 4
XA_DIM = 128
XA_WIDTH = XA_HEADS * XA_DIM
LN_EPS = 1e-5
DEPTH = 1
DN_ALPHA = (2.0 * DEPTH) ** 0.25
NEG_INF = -1e30
MASKED = 2.0 * NEG_INF
FORCE = 1e9
LOG2E = math.log2(math.e)
SLOPE_TERMS = 3

LANES = 128
VMEM_LIMIT = 56 * 1024 * 1024
CAST_TM = 512
INPROJ_TM, INPROJ_TN = 1024, 1024
INPROJ_SIDE_STEPS = 8
RET_CHUNKS_PER_STEP = 4
NSA_TQ = 256
NSA_TK = 256
NSA_LANE_PARTS = 2
OUTPROJ_TM, OUTPROJ_TN = 512, 1024
KV_TM, KV_TN = 512, 1024
XATTN_TM = 256
FFN_TM, FFN_TF = 512, 512
FFN_TN = 1024
FFN_W2_PARTS = (2048, 1024, 1024)
LN_ROWS = 8
LN_STATS_UNROLL = 16
LN_APPLY_UNROLL = 4

_NT = (((1,), (1,)), ((), ()))
_TN = (((0,), (0,)), ((), ()))


def _params(sem):
    return pltpu.CompilerParams(dimension_semantics=sem, vmem_limit_bytes=VMEM_LIMIT)


def _cast_slab_specs(w, layer, steps, step_index, cols=None, col_block=0):
    rows, cols = w.shape[1] // steps, cols or w.shape[2]
    in_spec = pl.BlockSpec((None, rows, cols), lambda *ids: (layer, step_index(*ids), col_block))
    out_spec = pl.BlockSpec((rows, cols), lambda *ids: (step_index(*ids), 0))
    return in_spec, out_spec, jax.ShapeDtypeStruct((w.shape[1], cols), BF16)


def _row_stats(tm):
    return [pltpu.VMEM((tm, LANES), F32), pltpu.VMEM((tm, LANES), F32)]


def _layer_norm_rows(o_ref, g_ref, b_ref, mu_ref, rstd_ref):
    tm, d = o_ref.shape
    n = tm // LN_ROWS

    def stats(c, carry):
        rows = pl.ds(pl.multiple_of(c * LN_ROWS, LN_ROWS), LN_ROWS)
        y = o_ref[rows, :]
        mu = jnp.mean(y, axis=-1, keepdims=True)
        dev = y - mu
        rstd = lax.rsqrt(jnp.mean(dev * dev, axis=-1, keepdims=True) + LN_EPS)
        mu_ref[rows, :] = jnp.broadcast_to(mu, (LN_ROWS, LANES))
        rstd_ref[rows, :] = jnp.broadcast_to(rstd, (LN_ROWS, LANES))
        return carry

    def apply(c, carry):
        rows = pl.ds(pl.multiple_of(c * LN_ROWS, LN_ROWS), LN_ROWS)
        mu, rstd = mu_ref[rows, :], rstd_ref[rows, :]
        for j in range(d // LANES):
            cols = slice(j * LANES, (j + 1) * LANES)
            o_ref[rows, cols] = (o_ref[rows, cols] - mu) * rstd * g_ref[:, cols] + b_ref[:, cols]
        return carry

    lax.fori_loop(0, n, stats, 0, unroll=min(LN_STATS_UNROLL, n))
    lax.fori_loop(0, n, apply, 0, unroll=LN_APPLY_UNROLL)


def _mm_kernel(x_ref, w_ref, o_ref):
    o_ref[...] = jnp.dot(x_ref[...], w_ref[...], preferred_element_type=F32).astype(o_ref.dtype)


def _matmul(x, w, n_out, tm, tn, out_dtype):
    m, k = x.shape
    return pl.pallas_call(
        _mm_kernel,
        grid=(m // tm, n_out // tn),
        in_specs=[pl.BlockSpec((tm, k), lambda i, j: (i, 0)),
                  pl.BlockSpec((k, tn), lambda i, j: (0, j))],
        out_specs=pl.BlockSpec((tm, tn), lambda i, j: (i, j)),
        out_shape=jax.ShapeDtypeStruct((m, n_out), out_dtype),
        compiler_params=_params(("parallel", "parallel")),
        name="matmul",
    )(x, w)


def _cast_gates_kernel(x_ref, wgt_ref, xb_ref, g_ref):
    xb = x_ref[...].astype(BF16)
    xb_ref[...] = xb
    g_ref[...] = lax.dot_general(xb, wgt_ref[...], _NT, preferred_element_type=F32)


def _cast_gates(x, wg_t, tm):
    m, k = x.shape
    return pl.pallas_call(
        _cast_gates_kernel,
        grid=(m // tm,),
        in_specs=[pl.BlockSpec((tm, k), lambda i: (i, 0)),
                  pl.BlockSpec((LANES, k), lambda i: (0, 0), pipeline_mode=pl.Buffered(1))],
        out_specs=[pl.BlockSpec((tm, k), lambda i: (i, 0)), pl.BlockSpec((tm, LANES), lambda i: (i, 0))],
        out_shape=[jax.ShapeDtypeStruct((m, k), BF16), jax.ShapeDtypeStruct((m, LANES), F32)],
        compiler_params=_params(("parallel",)),
        name="cast_gates",
    )(x, wg_t)


def _inproj_kernel(x_ref, w_ref, w1_ref, o_ref, w1b_ref):
    @pl.when(pl.program_id(1) < INPROJ_SIDE_STEPS)
    def _():
        w1b_ref[...] = w1_ref[...].astype(BF16)

    o_ref[...] = lax.dot_general(x_ref[...], w_ref[...], _NT, preferred_element_type=F32).astype(o_ref.dtype)


def _inproj(x, w_t, n_out, w_ff1, layer, tm, tn):
    m, k = x.shape
    side = INPROJ_SIDE_STEPS
    slab = lambda i, j: i * side + jnp.minimum(j, side - 1)
    w1_in, w1_out, w1_shape = _cast_slab_specs(w_ff1, layer, (m // tm) * side, slab)
    return pl.pallas_call(
        _inproj_kernel,
        grid=(m // tm, n_out // tn),
        in_specs=[pl.BlockSpec((tm, k), lambda i, j: (i, 0)),
                  pl.BlockSpec((tn, k), lambda i, j: (j, 0)), w1_in],
        out_specs=[pl.BlockSpec((tm, tn), lambda i, j: (i, j)), w1_out],
        out_shape=[jax.ShapeDtypeStruct((m, n_out), BF16), w1_shape],
        compiler_params=_params(("parallel", "arbitrary")),
        name="inproj",
    )(x, w_t, w_ff1)


def _retention_kernel(q_ref, k_ref, v_ref, g_ref, o_ref, state_ref, decay_ref, ktail_ref, qhead_ref):
    c = RET_CHUNK

    @pl.when(pl.program_id(1) == 0)
    def _():
        state_ref[...] = jnp.zeros_like(state_ref)
        diff = (lax.broadcasted_iota(jnp.int32, (c, c), 0)
                - lax.broadcasted_iota(jnp.int32, (c, c), 1)).astype(F32)
        pos = lax.broadcasted_iota(jnp.int32, (c, RET_DIM), 0).astype(F32)
        for h in range(RET_HEADS):
            log_g = math.log1p(-(2.0 ** (-5.0 - h)))
            decay_ref[h] = jnp.where(diff >= 0, jnp.exp(log_g * jnp.maximum(diff, 0.0)), 0.0)
            ktail_ref[h] = jnp.exp(log_g * (c - 1.0 - pos))
            qhead_ref[h] = jnp.exp(log_g * (pos + 1.0))

    for chunk, h in [(ch, hd) for ch in range(q_ref.shape[0] // c) for hd in range(RET_HEADS)]:
        log_g = math.log1p(-(2.0 ** (-5.0 - h)))
        sl = (slice(chunk * c, (chunk + 1) * c), slice(h * RET_DIM, (h + 1) * RET_DIM))
        q = q_ref[sl].astype(F32)
        k = k_ref[sl].astype(F32) * (RET_DIM ** -0.5)
        v = v_ref[sl]
        scores = lax.dot_general(q.astype(BF16), k.astype(BF16), _NT,
                                 preferred_element_type=F32) * decay_ref[h]
        o_inner = jnp.dot(scores.astype(BF16), v, preferred_element_type=F32)
        k_tail = k * ktail_ref[h]
        kv = lax.dot_general(k_tail.astype(BF16), v, _TN, preferred_element_type=F32)
        q_head = q * qhead_ref[h]
        state = state_ref[h]
        o = o_inner + jnp.dot(q_head.astype(BF16), state.astype(BF16), preferred_element_type=F32)
        state_ref[h] = math.exp(log_g * c) * state + kv
        mu = jnp.mean(o, axis=-1, keepdims=True)
        d = o - mu
        var = jnp.mean(d * d, axis=-1, keepdims=True)
        o = d * lax.rsqrt(var + LN_EPS)
        gate = g_ref[sl].astype(F32)
        o_ref[sl] = (o * (gate * jax.nn.sigmoid(gate))).astype(o_ref.dtype)


def _retention(proj, batch, seq):
    rows = RET_CHUNKS_PER_STEP * RET_CHUNK
    n = seq // rows
    blk = lambda col: pl.BlockSpec((rows, RET_WIDTH), lambda b, i: (b * n + i, col))
    return pl.pallas_call(
        _retention_kernel,
        grid=(batch, n),
        in_specs=[blk(OFF_RQ // RET_WIDTH), blk(OFF_RK // RET_WIDTH),
                  blk(OFF_RV // RET_WIDTH), blk(OFF_RG // RET_WIDTH)],
        out_specs=pl.BlockSpec((rows, RET_WIDTH), lambda b, i: (b * n + i, 0)),
        out_shape=jax.ShapeDtypeStruct((batch * seq, RET_WIDTH), BF16),
        scratch_shapes=[pltpu.VMEM((RET_HEADS, RET_DIM, RET_DIM), F32),
                        pltpu.VMEM((RET_HEADS, RET_CHUNK, RET_CHUNK), F32),
                        pltpu.VMEM((RET_HEADS, RET_CHUNK, RET_DIM), F32),
                        pltpu.VMEM((RET_HEADS, RET_CHUNK, RET_DIM), F32)],
        compiler_params=_params(("parallel", "arbitrary")),
        name="retention",
    )(proj, proj, proj, proj)


def _gelu_tanh(x):
    return 0.5 * x * (1.0 + jnp.tanh(math.sqrt(2.0 / math.pi) * (x + 0.044715 * (x * x * x))))


def _compress_kernel(zk_ref, zv_ref, pos_ref, w1_ref, w2_ref, kc_ref, vc_ref, zf_ref):
    nsub = zf_ref.shape[0] // CMP_STRIDE
    for kv, (z_ref, o_ref) in enumerate(((zk_ref, kc_ref), (zv_ref, vc_ref))):
        zf_ref[...] = z_ref[...].astype(F32)
        hid_a = jnp.zeros((nsub, LANES), F32)
        hid_b = jnp.zeros((nsub, LANES), F32)
        for i in range(CMP_STRIDE):
            rows = zf_ref[pl.ds(i, nsub, stride=CMP_STRIDE), :]
            xa = (rows + pos_ref[kv, i:i + 1, :]).astype(BF16)
            xb = (rows + pos_ref[kv, CMP_STRIDE + i:CMP_STRIDE + i + 1, :]).astype(BF16)
            wa = w1_ref[kv, i * NSA_DIM:(i + 1) * NSA_DIM, :]
            wb = w1_ref[kv, (CMP_STRIDE + i) * NSA_DIM:(CMP_STRIDE + i + 1) * NSA_DIM, :]
            hid_a = hid_a + jnp.dot(xa, wa, preferred_element_type=F32)
            hid_b = hid_b + jnp.dot(xb, wb, preferred_element_type=F32)
        hid = _gelu_tanh(hid_a + pltpu.roll(hid_b, nsub - 1, 0))
        o_ref[...] = jnp.dot(hid.astype(BF16), w2_ref[kv], preferred_element_type=F32).astype(o_ref.dtype)


def _compress(proj, cmp_pos, cmp_w1, cmp_w2, batch, seq):
    nsub = seq // CMP_STRIDE
    zspec = lambda col0: pl.BlockSpec((seq, NSA_DIM), lambda b, g: (b, col0 + g))
    full = lambda a: pl.BlockSpec(a.shape, lambda b, g: (0,) * a.ndim)
    ospec = pl.BlockSpec((None, None, nsub, NSA_DIM), lambda b, g: (b, g, 0, 0))
    oshape = jax.ShapeDtypeStruct((batch, NSA_KV_GROUPS, nsub, NSA_DIM), BF16)
    return pl.pallas_call(
        _compress_kernel,
        grid=(batch, NSA_KV_GROUPS),
        in_specs=[zspec(OFF_KC // NSA_DIM), zspec(OFF_VC // NSA_DIM),
                  full(cmp_pos), full(cmp_w1), full(cmp_w2)],
        out_specs=[ospec, ospec],
        out_shape=[oshape, oshape],
        scratch_shapes=[pltpu.VMEM((seq, NSA_DIM), F32)],
        compiler_params=_params(("parallel", "parallel")),
        name="compress",
    )(proj, proj, cmp_pos, cmp_w1, cmp_w2)


def _nsa_kernel(q_ref, kc_ref, vc_ref, ks_ref, vs_ref, kw_ref, vw_ref, gate_ref, wo_ref, w2_ref,
                o_ref, wob_ref, w2b_ref,
                q4_ref, slope_ref, vct_ref, vst_ref, vwt_ref, sig_ref, m_ref, l_ref, acc_ref, out_ref):
    tq, tk, rep, groups = NSA_TQ, NSA_TK, NSA_REP, NSA_KV_GROUPS
    cols = rep * tq
    n_tiles = ks_ref.shape[0] // tk
    n_blocks = ks_ref.shape[0] // SLC_LEN
    n_cmp = kc_ref.shape[1]
    slc, win = 0, 1
    i = pl.program_id(1)
    t0 = i * tq
    gcols = lambda g: slice(g * NSA_DIM, (g + 1) * NSA_DIM)

    wob_ref[...] = wo_ref[...].astype(BF16)
    w2b_ref[...] = w2_ref[...].astype(BF16)

    pos_lane, start_lane = n_blocks, n_blocks + 2 * SLOPE_TERMS

    def alibi_slope(head):
        return LOG2E * jnp.exp((head + 1).astype(F32) * (-0.5 * math.log(2.0)))

    @pl.when(i == 0)
    def _():
        for g in range(groups):
            vct_ref[g] = vc_ref[g].astype(F32).T.astype(BF16)
        row_head = lax.broadcasted_iota(jnp.int32, (cols, NSA_DIM), 0) >> (tq.bit_length() - 1)
        row_lane = lax.broadcasted_iota(jnp.int32, (cols, NSA_DIM), 1)
        slope_term = jnp.where(row_lane < start_lane, (row_lane - pos_lane) >> 1, row_lane - start_lane)
        for g in range(groups):
            rest = alibi_slope(g * rep + row_head)
            terms = jnp.zeros((cols, NSA_DIM), F32)
            for n in range(SLOPE_TERMS):
                term = rest.astype(BF16).astype(F32)
                rest = rest - term
                terms = jnp.where((row_lane >= pos_lane) & (slope_term == n), term, terms)
            slope_ref[g] = terms.astype(BF16)

        def transpose_tile(t, carry):
            rows = pl.ds(pl.multiple_of(t * tk, tk), tk)
            for g in range(groups):
                vst_ref[g, t] = vs_ref[rows, gcols(g)].astype(F32).T.astype(BF16)
                vwt_ref[g, t] = vw_ref[rows, gcols(g)].astype(F32).T.astype(BF16)
            return carry

        lax.fori_loop(0, n_tiles, transpose_tile, 0)

    sub = lax.broadcasted_iota(jnp.int32, (tk, cols), 0)
    t_loc = lax.broadcasted_iota(jnp.int32, (tk, cols), 1) & (tq - 1)
    lane_head = lax.broadcasted_iota(jnp.int32, (1, cols), 1) >> (tq.bit_length() - 1)
    causal = sub <= t_loc
    c_end = lax.broadcasted_iota(jnp.int32, (n_cmp, cols), 0) * CMP_STRIDE + (CMP_LEN - 1)
    c_mask = c_end <= t0 + (lax.broadcasted_iota(jnp.int32, (n_cmp, cols), 1) & (tq - 1))
    s_idx = lax.broadcasted_iota(jnp.int32, (n_blocks, n_cmp), 0)
    c_idx = lax.broadcasted_iota(jnp.int32, (n_blocks, n_cmp), 1)
    overlap = jnp.maximum(jnp.minimum(c_idx * CMP_STRIDE + CMP_LEN, s_idx * SLC_LEN + SLC_LEN)
                          - jnp.maximum(c_idx * CMP_STRIDE, s_idx * SLC_LEN), 0)
    overlap = (overlap.astype(F32) * (1.0 / CMP_LEN)).astype(BF16)
    blk = lax.broadcasted_iota(jnp.int32, (n_blocks, tq), 0)
    cur = (t0 + lax.broadcasted_iota(jnp.int32, (n_blocks, tq), 1)) >> (SLC_LEN.bit_length() - 1)
    forced = (blk == 0) | (blk == cur) | (blk == cur - 1)
    future = blk > cur

    sig_ref[...] = jax.nn.sigmoid(gate_ref[...]).T

    def gate_row(g, branch):
        first = branch * NSA_HEADS + g * rep
        return jnp.concatenate([sig_ref[first + r:first + r + 1, :] for r in range(rep)], axis=1)

    pos = lax.broadcasted_iota(jnp.int32, (tk, NSA_DIM), 0)
    key_lane = lax.broadcasted_iota(jnp.int32, (tk, NSA_DIM), 1)
    in_pos = (key_lane >= pos_lane) & (key_lane < start_lane)
    in_start = (key_lane >= start_lane) & (key_lane < start_lane + SLOPE_TERMS)
    pos_terms = jnp.where(((key_lane - pos_lane) & 1) == 0, (pos // SLC_LEN) * SLC_LEN, pos % SLC_LEN)
    pos_terms = jnp.where(in_pos, pos_terms, 0).astype(F32)

    def key_cols(kt, selected):
        start = kt * tk
        extra = pos_terms + jnp.where(in_start, start.astype(F32), 0.0)
        if selected:
            extra = extra + jnp.where(key_lane == ((start + pos) >> (SLC_LEN.bit_length() - 1)), 1.0, 0.0)
        return extra.astype(BF16)

    bias_lane = lax.broadcasted_iota(jnp.int32, (cols, NSA_DIM), 1) < n_blocks

    for g in range(groups):
        q4 = jnp.concatenate([q_ref[:, (g * rep + r) * NSA_DIM:(g * rep + r + 1) * NSA_DIM]
                              for r in range(rep)], axis=0)
        q4 = (q4.astype(F32) * (LOG2E * NSA_DIM ** -0.5)).astype(BF16)
        slope = alibi_slope(g * rep + lane_head)

        st = lax.dot_general(kc_ref[g], q4, _NT, preferred_element_type=F32)
        st = jnp.where(c_mask, st + slope * c_end.astype(F32), MASKED)
        e = jnp.exp2(st - jnp.max(st, axis=0, keepdims=True))
        p_cmp = jnp.where(c_mask, e * (1.0 / jnp.sum(e, axis=0, keepdims=True)), 0.0)
        o_cmp = jnp.dot(vct_ref[g], p_cmp.astype(BF16), preferred_element_type=F32)
        out_ref[g] = gate_row(g, 0) * o_cmp

        p_sum = p_cmp[:, 0:tq]
        for r in range(1, rep):
            p_sum = p_sum + p_cmp[:, r * tq:(r + 1) * tq]
        p_hi = p_sum.astype(BF16)
        p_lo = (p_sum - p_hi.astype(F32)).astype(BF16)
        imp = (jnp.dot(overlap, p_hi, preferred_element_type=F32)
               + jnp.dot(overlap, p_lo, preferred_element_type=F32))
        imp = jnp.where(forced, FORCE, jnp.where(future, -FORCE, imp))
        rank = jnp.zeros((n_blocks, tq), F32)
        for j in range(n_blocks):
            row = imp[j:j + 1, :]
            before = (row > imp) | ((row == imp) & (blk > j))
            rank = rank + jnp.where(before, 1.0, 0.0)
        sel_bias = jnp.where(rank < SLC_TOPK, 0.0, MASKED)

        sel_t = jnp.concatenate([sel_bias, jnp.zeros((NSA_DIM - n_blocks, tq), F32)], axis=0).T
        extra = jnp.where(bias_lane, jnp.concatenate([sel_t] * rep, axis=0), slope_ref[g].astype(F32))
        q4_ref[g] = jnp.concatenate([q4, extra.astype(BF16)], axis=1)

    part_cols = cols // NSA_LANE_PARTS

    def scores(g, part, k, extra):
        k = jnp.concatenate([k, extra], axis=1)
        q = q4_ref[g, part * part_cols:(part + 1) * part_cols, :]
        return lax.dot_general(k, q, _NT, preferred_element_type=F32)

    def update(branch, g, part, st, vt, keep, first, after):
        lanes = slice(part * part_cols, (part + 1) * part_cols)
        if after is not None:
            vt = vt + (0.0 * after[0:1, 0:tk]).astype(BF16)
        if keep is not None:
            st = jnp.where(keep[:, :part_cols], st, MASKED)
        m_tile = jnp.max(st, axis=0, keepdims=True)
        if first:
            p = jnp.exp2(st - m_tile)
            m_ref[branch, g, :, lanes] = m_tile
            l_ref[branch, g, :, lanes] = jnp.sum(p, axis=0, keepdims=True)
            acc_ref[branch, g, :, lanes] = jnp.dot(vt, p.astype(BF16), preferred_element_type=F32)
        else:
            m_old = m_ref[branch, g, :, lanes]
            m_new = jnp.maximum(m_old, m_tile)
            a = jnp.exp2(m_old - m_new)
            p = jnp.exp2(st - m_new)
            m_ref[branch, g, :, lanes] = m_new
            l_ref[branch, g, :, lanes] = a * l_ref[branch, g, :, lanes] + jnp.sum(p, axis=0, keepdims=True)
            acc_ref[branch, g, :, lanes] = (a * acc_ref[branch, g, :, lanes]
                                            + jnp.dot(vt, p.astype(BF16), preferred_element_type=F32))

    def slc_chains(kt):
        rows = pl.ds(pl.multiple_of(kt * tk, tk), tk)
        extra = key_cols(kt, True)
        return [(slc, g, part, ks_ref[rows, gcols(g)], extra, vst_ref[g, kt])
                for g in range(groups) for part in range(NSA_LANE_PARTS)]

    def win_chains(kt):
        rows = pl.ds(pl.multiple_of(kt * tk, tk), tk)
        extra = key_cols(kt, False)
        return [(win, g, part, kw_ref[rows, gcols(g)], extra, vwt_ref[g, kt])
                for g in range(groups) for part in range(NSA_LANE_PARTS)]

    def run(chains, keep, first):
        sts = [scores(g, part, k, extra) for (_, g, part, k, extra, _) in chains]
        for n, (branch, g, part, _, _, vt) in enumerate(chains):
            update(branch, g, part, sts[n], vt, keep, first, sts[n + 1] if n + 1 < len(sts) else None)

    run(slc_chains(i) + win_chains(i), causal, True)

    def slc_body(kt, carry):
        run(slc_chains(kt), None, False)
        return carry

    lax.fori_loop(0, i, slc_body, 0)

    n_back = WIN_LEN // tk

    @pl.when(i >= n_back)
    def _():
        run(win_chains(i - n_back), sub > t_loc, False)

    def win_body(kt, carry):
        run(win_chains(kt), None, False)
        return carry

    lax.fori_loop(jnp.maximum(i - (n_back - 1), 0), i, win_body, 0)

    for g in range(groups):
        o = out_ref[g]
        for branch, state in ((1, slc), (2, win)):
            o = o + (gate_row(g, branch) * (1.0 / l_ref[state, g])) * acc_ref[state, g]
        for r in range(rep):
            h = g * rep + r
            o_ref[:, h * NSA_DIM:(h + 1) * NSA_DIM] = o[:, r * tq:(r + 1) * tq].T.astype(o_ref.dtype)


def _nsa(proj, gates, kc, vc, w_out, w_ff2, layer, batch, seq):
    nq = seq // NSA_TQ
    step = lambda b, i: b * nq + i
    wo_in, wo_out, wo_shape = _cast_slab_specs(w_out, layer, batch * nq, step)
    w2_in, w2_out, w2_shape = _cast_slab_specs(w_ff2, layer, batch * nq, step, FFN_W2_PARTS[2],
                                               sum(FFN_W2_PARTS[:2]) // FFN_W2_PARTS[2])
    groups, cols = NSA_KV_GROUPS, NSA_REP * NSA_TQ
    n_tiles = seq // NSA_TK
    kvspec = lambda off: pl.BlockSpec((seq, NSA_KV_WIDTH), lambda b, i: (b, off // NSA_KV_WIDTH))
    cspec = lambda a: pl.BlockSpec((None,) + a.shape[1:], lambda b, i: (b, 0, 0, 0))
    return pl.pallas_call(
        _nsa_kernel,
        grid=(batch, nq),
        in_specs=[pl.BlockSpec((NSA_TQ, NSA_WIDTH), lambda b, i: (b * nq + i, OFF_NQ // NSA_WIDTH)),
                  cspec(kc), cspec(vc),
                  kvspec(OFF_KS), kvspec(OFF_VS), kvspec(OFF_KW), kvspec(OFF_VW),
                  pl.BlockSpec((NSA_TQ, LANES), lambda b, i: (b * nq + i, 0)), wo_in, w2_in],
        out_specs=[pl.BlockSpec((NSA_TQ, NSA_WIDTH), lambda b, i: (b * nq + i, 0)), wo_out, w2_out],
        out_shape=[jax.ShapeDtypeStruct((batch * seq, NSA_WIDTH), BF16), wo_shape, w2_shape],
        scratch_shapes=[pltpu.VMEM((groups, cols, 2 * NSA_DIM), BF16),
                        pltpu.VMEM((groups, cols, NSA_DIM), BF16),
                        pltpu.VMEM((groups, NSA_DIM, seq // CMP_STRIDE), BF16),
                        pltpu.VMEM((groups, n_tiles, NSA_DIM, NSA_TK), BF16),
                        pltpu.VMEM((groups, n_tiles, NSA_DIM, NSA_TK), BF16),
                        pltpu.VMEM((LANES, NSA_TQ), F32),
                        pltpu.VMEM((2, groups, 1, cols), F32),
                        pltpu.VMEM((2, groups, 1, cols), F32),
                        pltpu.VMEM((2, groups, NSA_DIM, cols), F32),
                        pltpu.VMEM((groups, NSA_DIM, cols), F32)],
        compiler_params=_params(("parallel", "arbitrary")),
        name="nsa",
    )(proj, kc, vc, proj, proj, proj, proj, gates, w_out, w_ff2)


def _outproj_kernel(ret_ref, nsa_ref, wr_ref, wn_ref, x_ref, g_ref, b_ref, w2_ref,
                    o_ref, w2b_ref, mu_ref, rstd_ref):
    w2b_ref[...] = w2_ref[...].astype(BF16)

    j = pl.program_id(1)
    nj = pl.num_programs(1)
    tn = x_ref.shape[1]
    y = (jnp.dot(ret_ref[...], wr_ref[...], preferred_element_type=F32)
         + jnp.dot(nsa_ref[...], wn_ref[...], preferred_element_type=F32))
    y = DN_ALPHA * x_ref[...] + y
    o_ref[:, pl.ds(pl.multiple_of(j * tn, tn), tn)] = y

    @pl.when(j == nj - 1)
    def _():
        _layer_norm_rows(o_ref, g_ref, b_ref, mu_ref, rstd_ref)


def _outproj(ret, nsa, w_out, x, g, b, w_ff2, layer, tm, tn):
    m, d = x.shape
    nj = d // tn
    vec = pl.BlockSpec((1, d), lambda i, j: (0, 0))
    w2_in, w2_out, w2_shape = _cast_slab_specs(w_ff2, layer, (m // tm) * nj, lambda i, j: i * nj + j,
                                               FFN_W2_PARTS[1], FFN_W2_PARTS[0] // FFN_W2_PARTS[1])
    return pl.pallas_call(
        _outproj_kernel,
        grid=(m // tm, d // tn),
        in_specs=[pl.BlockSpec((tm, RET_WIDTH), lambda i, j: (i, 0)),
                  pl.BlockSpec((tm, NSA_WIDTH), lambda i, j: (i, 0)),
                  pl.BlockSpec((RET_WIDTH, tn), lambda i, j: (0, j)),
                  pl.BlockSpec((NSA_WIDTH, tn), lambda i, j: (RET_WIDTH // NSA_WIDTH, j)),
                  pl.BlockSpec((tm, tn), lambda i, j: (i, j)),
                  vec, vec, w2_in],
        out_specs=[pl.BlockSpec((tm, d), lambda i, j: (i, 0)), w2_out],
        out_shape=[jax.ShapeDtypeStruct((m, d), F32), w2_shape],
        scratch_shapes=_row_stats(tm),
        compiler_params=_params(("parallel", "arbitrary")),
        name="outproj_ln",
    )(ret, nsa, w_out, w_out, x, g, b, w_ff2)


def _xattn_kernel(h_ref, kv_ref, wq_ref, wo_ref, g_ref, b_ref, w2_ref, o_ref, w2b_ref, mu_ref, rstd_ref):
    w2b_ref[...] = w2_ref[...].astype(BF16)

    q = jnp.dot(h_ref[...].astype(BF16), wq_ref[...], preferred_element_type=F32).astype(BF16)
    outs = []
    for hd in range(XA_HEADS):
        sl = slice(hd * XA_DIM, (hd + 1) * XA_DIM)
        k = kv_ref[:, sl]
        v = kv_ref[:, XA_WIDTH + hd * XA_DIM:XA_WIDTH + (hd + 1) * XA_DIM]
        s = lax.dot_general(q[:, sl], k, _NT, preferred_element_type=F32) * (XA_DIM ** -0.5)
        e = jnp.exp(s - jnp.max(s, axis=-1, keepdims=True))
        p = e / jnp.sum(e, axis=-1, keepdims=True)
        outs.append(jnp.dot(p.astype(BF16), v, preferred_element_type=F32))
    o = jnp.concatenate(outs, axis=-1).astype(BF16)
    o_ref[...] = DN_ALPHA * h_ref[...] + jnp.dot(o, wo_ref[...], preferred_element_type=F32)
    _layer_norm_rows(o_ref, g_ref, b_ref, mu_ref, rstd_ref)


def _xattn(h, kv, wq, wo, g, b, w_ff2, layer, seq, mem_len, tm):
    m, d = h.shape
    per_batch = seq // tm
    steps = m // tm
    full = lambda a: pl.BlockSpec(a.shape, lambda i: (0, 0), pipeline_mode=pl.Buffered(1))
    w2_in, w2_out, w2_shape = _cast_slab_specs(w_ff2, layer, steps, lambda i: i, FFN_W2_PARTS[0], 0)
    return pl.pallas_call(
        _xattn_kernel,
        grid=(steps,),
        in_specs=[pl.BlockSpec((tm, d), lambda i: (i, 0)),
                  pl.BlockSpec((mem_len, 2 * XA_WIDTH), lambda i: (i // per_batch, 0)),
                  full(wq), full(wo), full(g), full(b), w2_in],
        out_specs=[pl.BlockSpec((tm, d), lambda i: (i, 0)), w2_out],
        out_shape=[jax.ShapeDtypeStruct((m, d), F32), w2_shape],
        scratch_shapes=_row_stats(tm),
        compiler_params=_params(("parallel",)),
        name="xattn_ln",
    )(h, kv, wq, wo, g, b, w_ff2)


def _ffn_kernel(h_ref, w1_ref, *refs):
    n_parts = len(FFN_W2_PARTS)
    w2_refs, (g_ref, b_ref, o_ref, hb_ref, mu_ref, rstd_ref) = refs[:n_parts], refs[n_parts:]
    f = pl.program_id(1)
    nf = pl.num_programs(1)

    @pl.when(f == 0)
    def _():
        hb_ref[...] = h_ref[...].astype(BF16)
        o_ref[...] = DN_ALPHA * h_ref[...]

    u = jnp.maximum(jnp.dot(hb_ref[...], w1_ref[...], preferred_element_type=F32), 0.0)
    u = (u * u).astype(BF16)
    for c in range(o_ref.shape[1] // FFN_TN):
        cols = slice(c * FFN_TN, (c + 1) * FFN_TN)
        part = max(p for p in range(n_parts) if sum(FFN_W2_PARTS[:p]) <= c * FFN_TN)
        first = sum(FFN_W2_PARTS[:part])
        w2 = w2_refs[part][:, c * FFN_TN - first:(c + 1) * FFN_TN - first]
        o_ref[:, cols] += jnp.dot(u, w2, preferred_element_type=F32)

    @pl.when(f == nf - 1)
    def _():
        _layer_norm_rows(o_ref, g_ref, b_ref, mu_ref, rstd_ref)


def _ffn(h, w1, w2_parts, g, b, tm, tf):
    m, d = h.shape
    d_ff = w1.shape[1]
    vec = pl.BlockSpec((1, d), lambda i, f: (0, 0))
    return pl.pallas_call(
        _ffn_kernel,
        grid=(m // tm, d_ff // tf),
        in_specs=[pl.BlockSpec((tm, d), lambda i, f: (i, 0)),
                  pl.BlockSpec((d, tf), lambda i, f: (0, f)),
                  *[pl.BlockSpec((tf, w2.shape[1]), lambda i, f: (f, 0)) for w2 in w2_parts],
                  vec, vec],
        out_specs=pl.BlockSpec((tm, d), lambda i, f: (i, 0)),
        out_shape=jax.ShapeDtypeStruct((m, d), F32),
        scratch_shapes=[pltpu.VMEM((tm, d), BF16)] + _row_stats(tm),
        compiler_params=_params(("parallel", "arbitrary")),
        name="ffn_ln",
    )(h, w1, *w2_parts, g, b)


def kernel(x, mem, w_in, w_out, cmp_pos, cmp_w1, cmp_w2, xa_wq, xa_wkv, xa_wo,
           w_ff1, w_ff2, ln_g, ln_b):
    batch, seq, d = x.shape
    mem_len = mem.shape[1]
    h = x.reshape(batch * seq, d)
    mem2 = mem.reshape(batch * mem_len, d).astype(BF16)
    for l in range(DEPTH):
        w_in_t = jnp.swapaxes(w_in[l], 0, 1).astype(BF16)
        wg_t = jnp.pad(w_in_t[OFF_GATES:], ((0, LANES - GATE_WIDTH), (0, 0)))
        hb, gates = _cast_gates(h, wg_t, CAST_TM)
        proj, w_ff1_b = _inproj(hb, w_in_t, OFF_GATES, w_ff1, l, INPROJ_TM, INPROJ_TN)
        ret = _retention(proj, batch, seq)
        kc, vc = _compress(proj, cmp_pos[l], cmp_w1[l].astype(BF16), cmp_w2[l].astype(BF16),
                           batch, seq)
        sparse, w_out_b, w_ff2_c = _nsa(proj, gates, kc, vc, w_out, w_ff2, l, batch, seq)
        vecs = lambda a, k: a[l, k].reshape(1, d)
        h, w_ff2_b = _outproj(ret, sparse, w_out_b, h, vecs(ln_g, 0), vecs(ln_b, 0), w_ff2, l,
                              OUTPROJ_TM, OUTPROJ_TN)
        kv = _matmul(mem2, xa_wkv[l].astype(BF16), 2 * XA_WIDTH, KV_TM, KV_TN, BF16)
        h, w_ff2_a = _xattn(h, kv, xa_wq[l].astype(BF16), xa_wo[l].astype(BF16),
                            vecs(ln_g, 1), vecs(ln_b, 1), w_ff2, l, seq, mem_len, XATTN_TM)
        h = _ffn(h, w_ff1_b, (w_ff2_a, w_ff2_b, w_ff2_c), vecs(ln_g, 2), vecs(ln_b, 2), FFN_TM, FFN_TF)
    return h.reshape(batch, seq, d)
```
